```python
import jax, jax.numpy as jnp
from jax import lax
import numpy as np

D_MODEL = 1024
BATCH = 1
SEQ = 16384
DEPTH = 2

CONV_WIDTH = 512
CONV_K = 3
POOL_WINDOWS = (2, 4, 8, 16)
POOL_WIDTH = 512
POOL_GROUP = POOL_WIDTH // len(POOL_WINDOWS)
EVEN_SIZES = (CONV_WIDTH, CONV_WIDTH, CONV_WIDTH, CONV_WIDTH, POOL_WIDTH, POOL_WIDTH)
EVEN_IN = sum(EVEN_SIZES)
EVEN_OUT = CONV_WIDTH + POOL_WIDTH

ATT_HEADS = 8
ATT_KV_HEADS = 2
ATT_HEAD_DIM = 64
ATT_WIDTH = ATT_HEADS * ATT_HEAD_DIM
ATT_KV_WIDTH = ATT_KV_HEADS * ATT_HEAD_DIM
WINDOW = 128
BLOCK = 128
MLSTM_HEADS = 4
MLSTM_QK_DIM = 64
MLSTM_V_DIM = 128
MLSTM_QK_WIDTH = MLSTM_HEADS * MLSTM_QK_DIM
MLSTM_WIDTH = MLSTM_HEADS * MLSTM_V_DIM
CHUNK = 64
ODD_SIZES = (ATT_WIDTH, ATT_KV_WIDTH, ATT_KV_WIDTH, ATT_WIDTH,
             MLSTM_QK_WIDTH, MLSTM_QK_WIDTH, MLSTM_WIDTH, MLSTM_HEADS, MLSTM_HEADS,
             MLSTM_WIDTH, MLSTM_WIDTH)
ODD_IN = sum(ODD_SIZES)
ODD_OUT = ATT_WIDTH + MLSTM_WIDTH

N_EVEN = (DEPTH + 1) // 2
N_ODD = DEPTH // 2
RMS_EPS = 1e-6

kernel_name = "hybrid_conv_pool_swa_mlstm_trunk"


def rmsnorm(x, g):
    xf = x.astype(jnp.float32)
    y = xf * lax.rsqrt(jnp.mean(xf * xf, axis=-1, keepdims=True) + RMS_EPS)
    return (y * g.astype(jnp.float32)).astype(x.dtype)


def split_cols(p, sizes):
    idx = np.cumsum(sizes)[:-1].tolist()
    return jnp.split(p, idx, axis=-1)


def alibi_slopes(n):
    return jnp.asarray([2.0 ** (-8.0 * (h + 1) / n) for h in range(n)], dtype=jnp.float32)


def short_conv_mixer(xa, bg, cg, conv_w):
    u = cg * xa
    y = lax.conv_general_dilated(
        u, conv_w[:, None, :].astype(u.dtype), window_strides=(1,),
        padding=[(CONV_K - 1, 0)], dimension_numbers=('NWC', 'WIO', 'NWC'),
        feature_group_count=u.shape[-1])
    return bg * y


def pool_mixer(u, pool_w, pool_scale):
    S = u.shape[1]
    t1 = jnp.arange(1, S + 1, dtype=jnp.float32)[None, :, None]
    outs = []
    for ug, w in zip(jnp.split(u, len(POOL_WINDOWS), axis=-1), POOL_WINDOWS):
        uf = ug.astype(jnp.float32)
        cs = jnp.cumsum(uf, axis=1)
        cs_lag = jnp.pad(cs, ((0, 0), (w, 0), (0, 0)))[:, :S]
        outs.append(((cs - cs_lag) / jnp.minimum(t1, float(w)) - uf).astype(u.dtype))
    d = jnp.stack(outs, axis=2)
    y = jnp.einsum('bsgc,gcd->bsgd', d, pool_w).reshape(u.shape)
    return y * pool_scale


def sliding_window_attention(q, k, v, sinks):
    B, S = q.shape[0], q.shape[1]
    NB = S // BLOCK
    R = ATT_HEADS // ATT_KV_HEADS
    qb = q.reshape(B, NB, BLOCK, ATT_KV_HEADS, R, ATT_HEAD_DIM)

    def with_prev(t):
        tb = t.reshape(B, NB, BLOCK, ATT_KV_HEADS, ATT_HEAD_DIM)
        prev = jnp.pad(tb, ((0, 0), (1, 0), (0, 0), (0, 0), (0, 0)))[:, :NB]
        return jnp.concatenate([prev, tb], axis=2)

    kk, vv = with_prev(k), with_prev(v)
    s = jnp.einsum('bnqgrd,bnkgd->bngrqk', qb, kk).astype(jnp.float32) * (ATT_HEAD_DIM ** -0.5)
    qi = jnp.arange(BLOCK)[:, None]
    kj = jnp.arange(2 * BLOCK)[None, :]
    dist = qi + BLOCK - kj
    in_window = (dist >= 0) & (dist < WINDOW)
    key_valid = (jnp.arange(NB)[:, None, None] > 0) | (kj >= BLOCK)[None]
    mask = in_window[None] & key_valid
    slopes = alibi_slopes(ATT_HEADS).reshape(ATT_KV_HEADS, R)
    s = s - slopes[:, :, None, None] * dist.astype(jnp.float32)
    s = jnp.where(mask[None, :, None, None], s, -jnp.inf)
    sink = jnp.broadcast_to(sinks.astype(jnp.float32).reshape(ATT_KV_HEADS, R)[:, :, None, None],
                            s.shape[:-1] + (1,))
    p = jax.nn.softmax(jnp.concatenate([s, sink], axis=-1), axis=-1)[..., :-1]
    o = jnp.einsum('bngrqk,bnkgd->bnqgrd', p.astype(vv.dtype), vv)
    return o.reshape(B, S, ATT_WIDTH)


def mlstm(q, k, v, i_pre, f_pre):
    B, S = q.shape[0], q.shape[1]
    H, L = MLSTM_HEADS, CHUNK
    NC = S // L
    f32 = jnp.float32

    def chunk(t):
        t = t.astype(f32).reshape((B, NC, L) + t.shape[2:])
        return jnp.moveaxis(t, 3, 1)

    qc = chunk(q) * (MLSTM_QK_DIM ** -0.5)
    kc, vc, ic = chunk(k), chunk(v), chunk(i_pre)
    b = jnp.cumsum(jax.nn.log_sigmoid(chunk(f_pre)), axis=-1)
    bL = b[..., -1]
    a = bL[..., None] - b + ic
    m_chunk = jnp.max(a, axis=-1)
    w = jnp.exp(a - m_chunk[..., None])
    Kc = jnp.einsum('bhcl,bhcld,bhcle->bhcde', w, kc, vc)
    Nc = jnp.einsum('bhcl,bhcld->bhcd', w, kc)

    def step(carry, inp):
        C, n, m = carry
        bL_c, m_c, K_c, N_c = inp
        m_new = jnp.maximum(bL_c + m, m_c)
        decay = jnp.exp(bL_c + m - m_new)
        sc = jnp.exp(m_c - m_new)
        C_new = decay[..., None, None] * C + sc[..., None, None] * K_c
        n_new = decay[..., None] * n + sc[..., None] * N_c
        return (C_new, n_new, m_new), (C, n, m)

    init = (jnp.zeros((B, H, MLSTM_QK_DIM, MLSTM_V_DIM), f32),
            jnp.zeros((B, H, MLSTM_QK_DIM), f32),
            jnp.zeros((B, H), f32))
    xs = (jnp.moveaxis(bL, 2, 0), jnp.moveaxis(m_chunk, 2, 0),
          jnp.moveaxis(Kc, 2, 0), jnp.moveaxis(Nc, 2, 0))
    _, (Cp, Np, Mp) = lax.scan(step, init, xs)
    Cp, Np, Mp = jnp.moveaxis(Cp, 0, 2), jnp.moveaxis(Np, 0, 2), jnp.moveaxis(Mp, 0, 2)

    g = b + Mp[..., None]
    Dm = b[..., :, None] - b[..., None, :] + ic[..., None, :]
    Dm = jnp.where(jnp.tril(jnp.ones((L, L), bool)), Dm, -jnp.inf)
    m_t = jnp.maximum(g, jnp.max(Dm, axis=-1))
    inter = jnp.exp(g - m_t)
    qk = jnp.einsum('bhcld,bhcsd->bhcls', qc, kc) * jnp.exp(Dm - m_t[..., None])
    num = inter[..., None] * jnp.einsum('bhcld,bhcde->bhcle', qc, Cp) + jnp.einsum('bhcls,bhcse->bhcle', qk, vc)
    den = inter * jnp.einsum('bhcld,bhcd->bhcl', qc, Np) + jnp.sum(qk, axis=-1)
    h = num / jnp.maximum(jnp.abs(den), jnp.exp(-m_t))[..., None]
    h = jnp.moveaxis(h, 1, 3).reshape(B, S, MLSTM_WIDTH)
    return h


def even_layer(x, norm_g, w_in, conv_w, pool_w, pool_scale, w_out):
    h = rmsnorm(x, norm_g)
    xa, bg, cg, gate_a, u, gate_b = split_cols(h @ w_in, EVEN_SIZES)
    ya = short_conv_mixer(xa, bg, cg, conv_w) * jax.nn.silu(gate_a)
    yb = pool_mixer(u, pool_w, pool_scale) * jax.nn.silu(gate_b)
    return x + jnp.concatenate([ya, yb], axis=-1) @ w_out


def odd_layer(x, norm_g, w_in, i_bias, f_bias, sinks, w_out):
    B, S = x.shape[0], x.shape[1]
    h = rmsnorm(x, norm_g)
    (cq, ck, cv, gate_c, dq, dk, dv, di, df, do, gate_d) = split_cols(h @ w_in, ODD_SIZES)
    att = sliding_window_attention(
        cq.reshape(B, S, ATT_HEADS, ATT_HEAD_DIM),
        ck.reshape(B, S, ATT_KV_HEADS, ATT_HEAD_DIM),
        cv.reshape(B, S, ATT_KV_HEADS, ATT_HEAD_DIM), sinks)
    yc = att * jax.nn.silu(gate_c)
    mem = mlstm(dq.reshape(B, S, MLSTM_HEADS, MLSTM_QK_DIM),
                dk.reshape(B, S, MLSTM_HEADS, MLSTM_QK_DIM),
                dv.reshape(B, S, MLSTM_HEADS, MLSTM_V_DIM),
                di.astype(jnp.float32) + i_bias.astype(jnp.float32),
                df.astype(jnp.float32) + f_bias.astype(jnp.float32))
    yd = (mem * jax.nn.sigmoid(do.astype(jnp.float32))).astype(x.dtype) * jax.nn.silu(gate_d)
    return x + jnp.concatenate([yc, yd], axis=-1) @ w_out


def setup_inputs(seed: int = 0) -> dict:
    key = jax.random.key(seed)
    ks = jax.random.split(key, 16)
    nrm = jax.random.normal
    D = D_MODEL
    return {
        "x": nrm(ks[0], (BATCH, SEQ, D), jnp.float32),
        "even_norm": 1.0 + 0.02 * nrm(ks[1], (N_EVEN, D), jnp.float32),
        "even_w_in": nrm(ks[2], (N_EVEN, D, EVEN_IN), jnp.float32) * D ** -0.5,
        "even_conv_w": nrm(ks[3], (N_EVEN, CONV_K, CONV_WIDTH), jnp.float32) * CONV_K ** -0.5,
        "even_pool_w": nrm(ks[4], (N_EVEN, len(POOL_WINDOWS), POOL_GROUP, POOL_GROUP), jnp.float32) * POOL_GROUP ** -0.5,
        "even_pool_scale": 1.0 + 0.02 * nrm(ks[5], (N_EVEN, POOL_WIDTH), jnp.float32),
        "even_w_out": nrm(ks[6], (N_EVEN, EVEN_OUT, D), jnp.float32) * EVEN_OUT ** -0.5,
        "odd_norm": 1.0 + 0.02 * nrm(ks[7], (N_ODD, D), jnp.float32),
        "odd_w_in": nrm(ks[8], (N_ODD, D, ODD_IN), jnp.float32) * D ** -0.5,
        "odd_i_bias": 0.1 * nrm(ks[9], (N_ODD, MLSTM_HEADS), jnp.float32),
        "odd_f_bias": 3.0 + 0.1 * nrm(ks[10], (N_ODD, MLSTM_HEADS), jnp.float32),
        "odd_sinks": 0.1 * nrm(ks[11], (N_ODD, ATT_HEADS), jnp.float32),
        "odd_w_out": nrm(ks[12], (N_ODD, ODD_OUT, D), jnp.float32) * ODD_OUT ** -0.5,
        "final_norm": 1.0 + 0.02 * nrm(ks[13], (D,), jnp.float32),
    }


def reference(x, even_norm, even_w_in, even_conv_w, even_pool_w, even_pool_scale, even_w_out,
              odd_norm, odd_w_in, odd_i_bias, odd_f_bias, odd_sinks, odd_w_out, final_norm):
    for layer in range(DEPTH):
        j = layer // 2
        if layer % 2 == 0:
            x = even_layer(x, even_norm[j], even_w_in[j], even_conv_w[j], even_pool_w[j],
                           even_pool_scale[j], even_w_out[j])
        else:
            x = odd_layer(x, odd_norm[j], odd_w_in[j], odd_i_bias[j], odd_f_bias[j],
                          odd_sinks[j], odd_w_out[j])
    return rmsnorm(x, final_norm)
```

```python
import functools

import jax
import jax.numpy as jnp
import numpy as np
from jax import lax
from jax.experimental import pallas as pl
from jax.experimental.pallas import tpu as pltpu

F32 = jnp.float32
BF16 = jnp.bfloat16

D_MODEL = 1024
RMS_EPS = 1e-6

CONV_WIDTH = 512
CONV_K = 3
POOL_WINDOWS = (2, 4, 8, 16)
POOL_WIDTH = 512
POOL_GROUP = POOL_WIDTH // len(POOL_WINDOWS)
EVEN_IN = 4 * CONV_WIDTH + 2 * POOL_WIDTH
EVEN_OUT = CONV_WIDTH + POOL_WIDTH
CONV_HALO = 8
POOL_HALO = 16

ATT_HEADS = 8
ATT_KV_HEADS = 2
ATT_GROUP = ATT_HEADS // ATT_KV_HEADS
ATT_HEAD_DIM = 64
ATT_WIDTH = ATT_HEADS * ATT_HEAD_DIM
ATT_KV_WIDTH = ATT_KV_HEADS * ATT_HEAD_DIM
WINDOW = 128
BLOCK = 128
MLSTM_HEADS = 4
MLSTM_QK_DIM = 64
MLSTM_V_DIM = 128
MLSTM_QK_WIDTH = MLSTM_HEADS * MLSTM_QK_DIM
MLSTM_WIDTH = MLSTM_HEADS * MLSTM_V_DIM
ODD_OUT = ATT_WIDTH + MLSTM_WIDTH
LANES = 128
GATE_PAD = LANES

O_CQ = 0
O_CK = O_CQ + ATT_WIDTH
O_CV = O_CK + ATT_KV_WIDTH
O_GC = O_CV + ATT_KV_WIDTH
O_DQ = O_GC + ATT_WIDTH
O_DK = O_DQ + MLSTM_QK_WIDTH
O_DV = O_DK + MLSTM_QK_WIDTH
O_DO = O_DV + MLSTM_WIDTH
O_GD = O_DO + MLSTM_WIDTH
O_IF = O_GD + MLSTM_WIDTH
ODD_IN_PAD = O_IF + GATE_PAD

TILE = 512
MCHUNK = 256
VMEM_LIMIT = 56 * 1024 * 1024


def _rmsnorm(x, g):
    ms = jnp.mean(x * x, axis=-1, keepdims=True)
    return x * lax.rsqrt(ms + RMS_EPS) * g


def _even_kernel(x_ref, g_ref, win_ref, cw_ref, pw_ref, ps_ref, wout_ref, o_ref,
                 proj_scr, uc_scr, up_scr, y_scr):
    i = pl.program_id(0)
    T = x_ref.shape[0]
    C = CONV_WIDTH

    @pl.when(i == 0)
    def _():
        uc_scr[0:CONV_HALO, :] = jnp.zeros((CONV_HALO, C), F32)
        up_scr[0:POOL_HALO, :] = jnp.zeros((POOL_HALO, POOL_WIDTH), F32)

    h = _rmsnorm(x_ref[...], g_ref[...]).astype(BF16)
    proj_scr[...] = jnp.dot(h, win_ref[...], preferred_element_type=F32)

    uc = proj_scr[:, 2 * C:3 * C] * proj_scr[:, 0:C]
    uc_scr[CONV_HALO:CONV_HALO + T, :] = uc
    conv = (cw_ref[2:3, :] * uc
            + cw_ref[1:2, :] * uc_scr[CONV_HALO - 1:CONV_HALO - 1 + T, :]
            + cw_ref[0:1, :] * uc_scr[CONV_HALO - 2:CONV_HALO - 2 + T, :])
    ya = proj_scr[:, C:2 * C] * conv * jax.nn.silu(proj_scr[:, 3 * C:4 * C])
    y_scr[:, 0:C] = ya.astype(BF16)
    uc_scr[0:CONV_HALO, :] = uc_scr[T:T + CONV_HALO, :]

    up_scr[POOL_HALO:POOL_HALO + T, :] = proj_scr[:, 4 * C:4 * C + POOL_WIDTH]
    t1 = (lax.broadcasted_iota(jnp.int32, (T, POOL_GROUP), 0) + (i * T + 1)).astype(F32)
    for g, w in enumerate(POOL_WINDOWS):
        lo, hi = g * POOL_GROUP, (g + 1) * POOL_GROUP
        s = up_scr[:, lo:hi]
        k = 1
        while k < w:
            s = s + pltpu.roll(s, k, 0)
            k *= 2
        u = up_scr[POOL_HALO:POOL_HALO + T, lo:hi]
        d = s[POOL_HALO:, :] / jnp.minimum(t1, float(w)) - u
        yb = jnp.dot(d.astype(BF16), pw_ref[g], preferred_element_type=F32)
        gate = proj_scr[:, 4 * C + POOL_WIDTH + lo:4 * C + POOL_WIDTH + hi]
        y_scr[:, C + lo:C + hi] = (yb * ps_ref[:, lo:hi] * jax.nn.silu(gate)).astype(BF16)
    up_scr[0:POOL_HALO, :] = up_scr[T:T + POOL_HALO, :]

    o_ref[...] = x_ref[...] + jnp.dot(y_scr[...], wout_ref[...], preferred_element_type=F32)


def _even_layer(x, norm_g, w_in, conv_w, pool_w, pool_scale, w_out):
    S = x.shape[0]
    T = min(TILE, S)
    assert S % T == 0
    const = lambda *shape: pl.BlockSpec(shape, lambda i: (0,) * len(shape))
    return pl.pallas_call(
        _even_kernel,
        grid=(S // T,),
        in_specs=[
            pl.BlockSpec((T, D_MODEL), lambda i: (i, 0)),
            const(1, D_MODEL),
            const(D_MODEL, EVEN_IN),
            const(CONV_K, CONV_WIDTH),
            const(len(POOL_WINDOWS), POOL_GROUP, POOL_GROUP),
            const(1, POOL_WIDTH),
            const(EVEN_OUT, D_MODEL),
        ],
        out_specs=pl.BlockSpec((T, D_MODEL), lambda i: (i, 0)),
        out_shape=jax.ShapeDtypeStruct((S, D_MODEL), F32),
        scratch_shapes=[
            pltpu.VMEM((T, EVEN_IN), F32),
            pltpu.VMEM((T + CONV_HALO, CONV_WIDTH), F32),
            pltpu.VMEM((T + POOL_HALO, POOL_WIDTH), F32),
            pltpu.VMEM((T, EVEN_OUT), BF16),
        ],
        compiler_params=pltpu.CompilerParams(
            dimension_semantics=("arbitrary",), vmem_limit_bytes=VMEM_LIMIT),
        name="even_layer",
    )(x, norm_g.reshape(1, D_MODEL), w_in.astype(BF16), conv_w, pool_w.astype(BF16),
      pool_scale.reshape(1, POOL_WIDTH), w_out.astype(BF16))


def _lane_scan(x, op, identity):
    n = x.shape[1]
    lane = lax.broadcasted_iota(jnp.int32, x.shape, 1)
    k = 1
    while k < n:
        x = op(x, jnp.where(lane >= k, pltpu.roll(x, k, 1), identity))
        k *= 2
    return x


def _odd_kernel(sink_ref, x_ref, g_ref, win_ref, bias_ref, wout_ref, fg_ref, o_ref,
                proj_scr, kbuf, vbuf, abias, c_scr, m_scr, y_scr):
    i = pl.program_id(0)
    T = x_ref.shape[0]
    NB = T // BLOCK
    H = MLSTM_HEADS
    L = min(MCHUNK, T)
    NEG = -jnp.inf

    @pl.when(i == 0)
    def _():
        kbuf[0:BLOCK, :] = jnp.zeros((BLOCK, ATT_KV_WIDTH), BF16)
        vbuf[0:BLOCK, :] = jnp.zeros((BLOCK, ATT_KV_WIDTH), BF16)
        c_scr[...] = jnp.zeros(c_scr.shape, F32)
        m_scr[...] = jnp.zeros(m_scr.shape, F32)
        qi = lax.broadcasted_iota(jnp.int32, (BLOCK, 2 * BLOCK), 0)
        kj = lax.broadcasted_iota(jnp.int32, (BLOCK, 2 * BLOCK), 1)
        dist = qi + BLOCK - kj
        ok = (dist >= 0) & (dist < WINDOW)
        distf = dist.astype(F32)
        for hd in range(ATT_HEADS):
            slope = 2.0 ** (-8.0 * (hd + 1) / ATT_HEADS)
            abias[hd * BLOCK:(hd + 1) * BLOCK, :] = jnp.where(ok, -slope * distf, NEG)

    h = _rmsnorm(x_ref[...], g_ref[...]).astype(BF16)
    proj_scr[...] = jnp.dot(h, win_ref[...], preferred_element_type=F32)

    kbuf[BLOCK:BLOCK + T, :] = proj_scr[:, O_CK:O_CK + ATT_KV_WIDTH].astype(BF16)
    vbuf[BLOCK:BLOCK + T, :] = proj_scr[:, O_CV:O_CV + ATT_KV_WIDTH].astype(BF16)
    lane_q = lax.broadcasted_iota(jnp.int32, (BLOCK, LANES), 1)
    low = lane_q < ATT_HEAD_DIM
    key_lane = lax.broadcasted_iota(jnp.int32, (BLOCK, 2 * BLOCK), 1)
    no_prev = jnp.where(key_lane < BLOCK, jnp.where(i == 0, NEG, 0.0), 0.0)
    scale = ATT_HEAD_DIM ** -0.5
    for j in range(NB):
        r0 = j * BLOCK
        kk = kbuf[r0:r0 + 2 * BLOCK, :]
        vv = vbuf[r0:r0 + 2 * BLOCK, :]
        q_lo, q_hi = [], []
        for p in range(ATT_GROUP):
            qp = proj_scr[r0:r0 + BLOCK, O_CQ + p * LANES:O_CQ + (p + 1) * LANES] * scale
            q_lo.append(jnp.where(low, qp, 0.0).astype(BF16))
            q_hi.append(jnp.where(low, 0.0, qp).astype(BF16))
        qs = jnp.concatenate(q_lo + q_hi, axis=0)
        s_all = lax.dot_general(qs, kk, (((1,), (1,)), ((), ())), preferred_element_type=F32)
        ps, ls = [], []
        for hd in range(ATT_HEADS):
            s = s_all[hd * BLOCK:(hd + 1) * BLOCK, :] + abias[hd * BLOCK:(hd + 1) * BLOCK, :]
            if j == 0:
                s = s + no_prev
            sink = sink_ref[hd]
            m = jnp.maximum(jnp.max(s, axis=1, keepdims=True), sink)
            e = jnp.exp(s - m)
            ls.append(jnp.sum(e, axis=1, keepdims=True) + jnp.exp(sink - m))
            ps.append(e.astype(BF16))
        o_all = jnp.dot(jnp.concatenate(ps, axis=0), vv, preferred_element_type=F32)
        for p in range(ATT_GROUP):
            o_lo = o_all[p * BLOCK:(p + 1) * BLOCK, :] / ls[p]
            o_hi = o_all[(ATT_GROUP + p) * BLOCK:(ATT_GROUP + p + 1) * BLOCK, :] / ls[ATT_GROUP + p]
            att = jnp.where(low, o_lo, o_hi)
            gate = proj_scr[r0:r0 + BLOCK, O_GC + p * LANES:O_GC + (p + 1) * LANES]
            y_scr[r0:r0 + BLOCK, p * LANES:(p + 1) * LANES] = (att * jax.nn.silu(gate)).astype(BF16)
    kbuf[0:BLOCK, :] = kbuf[T:T + BLOCK, :]
    vbuf[0:BLOCK, :] = vbuf[T:T + BLOCK, :]

    gt = proj_scr[:, O_IF:O_IF + GATE_PAD].T[0:2 * H, :]
    pre = gt + jnp.concatenate([bias_ref[...]] * (T // LANES), axis=1)
    row = lax.broadcasted_iota(jnp.int32, (2 * H, T), 0)
    top = row < H
    bcum = _lane_scan(jax.nn.log_sigmoid(pre), jnp.add, 0.0)
    bsw = pltpu.roll(bcum, H, 0)
    a_top = pre - bsw
    ag = jnp.where(top, a_top, pltpu.roll(a_top, H, 0))
    bf = jnp.where(top, bsw, bcum)
    m_in = jnp.concatenate([m_scr[...]] * (T // LANES), axis=1)
    mg = jnp.maximum(m_in, _lane_scan(ag, jnp.maximum, NEG))
    m_t = bf + mg
    nrm = jnp.exp(-m_t)
    inter_parts, wk_parts, decay_parts = [], [], []
    for c in range(T // L):
        m_start = m_in[:, 0:1] if c == 0 else mg[:, c * L - 1:c * L]
        m_last = mg[:, (c + 1) * L - 1:(c + 1) * L]
        inter_parts.append(jnp.exp(m_start - mg[:, c * L:(c + 1) * L]))
        wk_parts.append(jnp.exp(ag[:, c * L:(c + 1) * L] - m_last))
        decay_parts.append(jnp.broadcast_to(jnp.exp(m_start - m_last), (2 * H, 2 * MLSTM_V_DIM)))
    inter = jnp.concatenate(inter_parts, axis=1)
    m_scr[...] = jnp.broadcast_to(m_t[:, T - 1:T], m_scr.shape)
    cols = jnp.concatenate([mg, inter, nrm, jnp.zeros((LANES - 6 * H, T), F32)], axis=0).T

    kT = proj_scr[:, O_DK:O_DK + MLSTM_QK_WIDTH].T
    lane_p = lax.broadcasted_iota(jnp.int32, (L, LANES), 1)
    low_p = lane_p < MLSTM_QK_DIM
    tri = (lax.broadcasted_iota(jnp.int32, (L, L), 1) <= lax.broadcasted_iota(jnp.int32, (L, L), 0))
    ones = jnp.ones((L, MLSTM_V_DIM), BF16)
    qscale = MLSTM_QK_DIM ** -0.5
    for c in range(T // L):
        r0 = c * L
        for pair in range(H // 2):
            qp = proj_scr[r0:r0 + L, O_DQ + pair * LANES:O_DQ + (pair + 1) * LANES] * qscale
            kT_pair = kT[pair * LANES:(pair + 1) * LANES, r0:r0 + L].astype(BF16)
            c_pair = c_scr[pair * LANES:(pair + 1) * LANES, :].astype(BF16)
            for half in range(2):
                hd = 2 * pair + half
                q_h = (jnp.where(low_p, qp, 0.0) if half == 0 else jnp.where(low_p, 0.0, qp)).astype(BF16)
                s = jnp.dot(q_h, kT_pair, preferred_element_type=F32)
                arg = ag[hd:hd + 1, r0:r0 + L] - cols[r0:r0 + L, hd:hd + 1]
                e = jnp.exp(jnp.where(tri, arg, NEG))
                p_bf = (s * e).astype(BF16)
                v_aug = jnp.concatenate(
                    [proj_scr[r0:r0 + L, O_DV + hd * MLSTM_V_DIM:O_DV + (hd + 1) * MLSTM_V_DIM].astype(BF16), ones],
                    axis=1)
                num = (jnp.dot(p_bf, v_aug, preferred_element_type=F32)
                       + cols[r0:r0 + L, 2 * H + hd:2 * H + hd + 1]
                       * jnp.dot(q_h, c_pair, preferred_element_type=F32))
                den = jnp.maximum(jnp.abs(num[:, MLSTM_V_DIM:]), cols[r0:r0 + L, 4 * H + hd:4 * H + hd + 1])
                hm = num[:, :MLSTM_V_DIM] / den
                og = proj_scr[r0:r0 + L, O_DO + hd * MLSTM_V_DIM:O_DO + (hd + 1) * MLSTM_V_DIM]
                gd = proj_scr[r0:r0 + L, O_GD + hd * MLSTM_V_DIM:O_GD + (hd + 1) * MLSTM_V_DIM]
                y_scr[r0:r0 + L, ATT_WIDTH + hd * MLSTM_V_DIM:ATT_WIDTH + (hd + 1) * MLSTM_V_DIM] = (
                    hm * jax.nn.sigmoid(og) * jax.nn.silu(gd)).astype(BF16)
                kw = (kT[hd * MLSTM_QK_DIM:(hd + 1) * MLSTM_QK_DIM, r0:r0 + L]
                      * wk_parts[c][hd:hd + 1, :]).astype(BF16)
                upd = jnp.dot(kw, v_aug, preferred_element_type=F32)
                rows = slice(hd * MLSTM_QK_DIM, (hd + 1) * MLSTM_QK_DIM)
                c_scr[rows, :] = decay_parts[c][hd:hd + 1, :] * c_scr[rows, :] + upd

    x2 = x_ref[...] + jnp.dot(y_scr[...], wout_ref[...], preferred_element_type=F32)
    o_ref[...] = _rmsnorm(x2, fg_ref[...])


def _odd_layer(x, norm_g, w_in, i_bias, f_bias, sinks, w_out, final_g):
    S = x.shape[0]
    T = min(TILE, S)
    assert S % T == 0 and T % BLOCK == 0 and T % min(MCHUNK, T) == 0
    perm = np.array([(p + ATT_GROUP * half) * ATT_HEAD_DIM + d
                     for p in range(ATT_GROUP) for half in range(2) for d in range(ATT_HEAD_DIM)])
    o_cq, o_ck, o_cv, o_gc = 0, ATT_WIDTH, ATT_WIDTH + ATT_KV_WIDTH, ATT_WIDTH + 2 * ATT_KV_WIDTH
    o_dq = o_gc + ATT_WIDTH
    o_if = o_dq + 2 * MLSTM_QK_WIDTH + MLSTM_WIDTH
    o_do = o_if + 2 * MLSTM_HEADS
    w = jnp.concatenate([
        w_in[:, o_cq + perm], w_in[:, o_ck:o_gc], w_in[:, o_gc + perm],
        w_in[:, o_dq:o_if], w_in[:, o_do:], w_in[:, o_if:o_do],
        jnp.zeros((D_MODEL, GATE_PAD - 2 * MLSTM_HEADS), w_in.dtype)], axis=1).astype(BF16)
    wo = jnp.concatenate([w_out[perm], w_out[ATT_WIDTH:]], axis=0).astype(BF16)
    bias = jnp.broadcast_to(jnp.concatenate([i_bias, f_bias]).astype(F32)[:, None], (2 * MLSTM_HEADS, LANES))
    const = lambda *shape: pl.BlockSpec(shape, lambda i: (0,) * len(shape))
    return pl.pallas_call(
        _odd_kernel,
        grid=(S // T,),
        in_specs=[
            pl.BlockSpec(memory_space=pltpu.SMEM),
            pl.BlockSpec((T, D_MODEL), lambda i: (i, 0)),
            const(1, D_MODEL),
            const(D_MODEL, ODD_IN_PAD),
            const(2 * MLSTM_HEADS, LANES),
            const(ODD_OUT, D_MODEL),
            const(1, D_MODEL),
        ],
        out_specs=pl.BlockSpec((T, D_MODEL), lambda i: (i, 0)),
        out_shape=jax.ShapeDtypeStruct((S, D_MODEL), F32),
        scratch_shapes=[
            pltpu.VMEM((T, ODD_IN_PAD), F32),
            pltpu.VMEM((T + BLOCK, ATT_KV_WIDTH), BF16),
            pltpu.VMEM((T + BLOCK, ATT_KV_WIDTH), BF16),
            pltpu.VMEM((ATT_HEADS * BLOCK, 2 * BLOCK), F32),
            pltpu.VMEM((MLSTM_HEADS * MLSTM_QK_DIM, 2 * MLSTM_V_DIM), F32),
            pltpu.VMEM((2 * MLSTM_HEADS, LANES), F32),
            pltpu.VMEM((T, ODD_OUT), BF16),
        ],
        compiler_params=pltpu.CompilerParams(
            dimension_semantics=("arbitrary",), vmem_limit_bytes=VMEM_LIMIT),
        name="odd_layer",
    )(sinks.astype(F32), x, norm_g.reshape(1, D_MODEL), w, bias, wo, final_g.reshape(1, D_MODEL))


def kernel(x, even_norm, even_w_in, even_conv_w, even_pool_w, even_pool_scale, even_w_out,
           odd_norm, odd_w_in, odd_i_bias, odd_f_bias, odd_sinks, odd_w_out, final_norm):
    B, S, D = x.shape
    assert B == 1 and D == D_MODEL
    assert even_norm.shape[0] == 1 and odd_norm.shape[0] == 1
    h = x.reshape(S, D)
    h = _even_layer(h, even_norm[0], even_w_in[0], even_conv_w[0], even_pool_w[0],
                    even_pool_scale[0], even_w_out[0])
    h = _odd_layer(h, odd_norm[0], odd_w_in[0], odd_i_bias[0], odd_f_bias[0], odd_sinks[0],
                   odd_w_out[0], final_norm)
    return h.reshape(B, S, D)
```

```python
import functools

import jax
import jax.numpy as jnp
import numpy as np
from jax import lax
from jax.experimental import pallas as pl
from jax.experimental.pallas import tpu as pltpu

F32 = jnp.float32
BF16 = jnp.bfloat16

D_MODEL = 1024
RMS_EPS = 1e-6

CONV_WIDTH = 512
CONV_K = 3
POOL_WINDOWS = (2, 4, 8, 16)
POOL_WIDTH = 512
POOL_GROUP = POOL_WIDTH // len(POOL_WINDOWS)
EVEN_IN = 4 * CONV_WIDTH + 2 * POOL_WIDTH
EVEN_OUT = CONV_WIDTH + POOL_WIDTH
CONV_HALO = 8
POOL_HALO = 16

ATT_HEADS = 8
ATT_KV_HEADS = 2
ATT_GROUP = ATT_HEADS // ATT_KV_HEADS
ATT_HEAD_DIM = 64
ATT_WIDTH = ATT_HEADS * ATT_HEAD_DIM
ATT_KV_WIDTH = ATT_KV_HEADS * ATT_HEAD_DIM
WINDOW = 128
BLOCK = 128
MLSTM_HEADS = 4
MLSTM_QK_DIM = 64
MLSTM_V_DIM = 128
MLSTM_QK_WIDTH = MLSTM_HEADS * MLSTM_QK_DIM
MLSTM_WIDTH = MLSTM_HEADS * MLSTM_V_DIM
ODD_OUT = ATT_WIDTH + MLSTM_WIDTH
LANES = 128
GATE_PAD = LANES

O_CQ = 0
O_CK = O_CQ + ATT_WIDTH
O_CV = O_CK + ATT_KV_WIDTH
O_GC = O_CV + ATT_KV_WIDTH
O_DQ = O_GC + ATT_WIDTH
O_DK = O_DQ + MLSTM_QK_WIDTH
O_DV = O_DK + MLSTM_QK_WIDTH
O_DO = O_DV + MLSTM_WIDTH
O_GD = O_DO + MLSTM_WIDTH
O_IF = O_GD + MLSTM_WIDTH
ODD_IN_PAD = O_IF + GATE_PAD

TILE = 512
MCHUNK = 256
VMEM_LIMIT = 56 * 1024 * 1024


def _rmsnorm(x, g):
    ms = jnp.mean(x * x, axis=-1, keepdims=True)
    return x * lax.rsqrt(ms + RMS_EPS) * g


def _even_kernel(x_ref, g_ref, win_ref, cw_ref, pw_ref, ps_ref, wout_ref, o_ref,
                 proj_scr, uc_scr, up_scr, y_scr):
    i = pl.program_id(0)
    T = x_ref.shape[0]
    C = CONV_WIDTH

    @pl.when(i == 0)
    def _():
        uc_scr[0:CONV_HALO, :] = jnp.zeros((CONV_HALO, C), F32)
        up_scr[0:POOL_HALO, :] = jnp.zeros((POOL_HALO, POOL_WIDTH), F32)

    h = _rmsnorm(x_ref[...], g_ref[...]).astype(BF16)
    proj_scr[...] = jnp.dot(h, win_ref[...], preferred_element_type=F32)

    uc = proj_scr[:, 2 * C:3 * C] * proj_scr[:, 0:C]
    uc_scr[CONV_HALO:CONV_HALO + T, :] = uc
    conv = (cw_ref[2:3, :] * uc
            + cw_ref[1:2, :] * uc_scr[CONV_HALO - 1:CONV_HALO - 1 + T, :]
            + cw_ref[0:1, :] * uc_scr[CONV_HALO - 2:CONV_HALO - 2 + T, :])
    ya = proj_scr[:, C:2 * C] * conv * jax.nn.silu(proj_scr[:, 3 * C:4 * C])
    y_scr[:, 0:C] = ya.astype(BF16)
    uc_scr[0:CONV_HALO, :] = uc_scr[T:T + CONV_HALO, :]

    up_scr[POOL_HALO:POOL_HALO + T, :] = proj_scr[:, 4 * C:4 * C + POOL_WIDTH]
    t1 = (lax.broadcasted_iota(jnp.int32, (T, POOL_GROUP), 0) + (i * T + 1)).astype(F32)
    for g, w in enumerate(POOL_WINDOWS):
        lo, hi = g * POOL_GROUP, (g + 1) * POOL_GROUP
        s = up_scr[:, lo:hi]
        k = 1
        while k < w:
            s = s + pltpu.roll(s, k, 0)
            k *= 2
        u = up_scr[POOL_HALO:POOL_HALO + T, lo:hi]
        d = s[POOL_HALO:, :] / jnp.minimum(t1, float(w)) - u
        yb = jnp.dot(d.astype(BF16), pw_ref[g], preferred_element_type=F32)
        gate = proj_scr[:, 4 * C + POOL_WIDTH + lo:4 * C + POOL_WIDTH + hi]
        y_scr[:, C + lo:C + hi] = (yb * ps_ref[:, lo:hi] * jax.nn.silu(gate)).astype(BF16)
    up_scr[0:POOL_HALO, :] = up_scr[T:T + POOL_HALO, :]

    o_ref[...] = x_ref[...] + jnp.dot(y_scr[...], wout_ref[...], preferred_element_type=F32)


def _even_layer(x, norm_g, w_in, conv_w, pool_w, pool_scale, w_out):
    S = x.shape[0]
    T = min(TILE, S)
    assert S % T == 0
    const = lambda *shape: pl.BlockSpec(shape, lambda i: (0,) * len(shape))
    return pl.pallas_call(
        _even_kernel,
        grid=(S // T,),
        in_specs=[
            pl.BlockSpec((T, D_MODEL), lambda i: (i, 0)),
            const(1, D_MODEL),
            const(D_MODEL, EVEN_IN),
            const(CONV_K, CONV_WIDTH),
            const(len(POOL_WINDOWS), POOL_GROUP, POOL_GROUP),
            const(1, POOL_WIDTH),
            const(EVEN_OUT, D_MODEL),
        ],
        out_specs=pl.BlockSpec((T, D_MODEL), lambda i: (i, 0)),
        out_shape=jax.ShapeDtypeStruct((S, D_MODEL), F32),
        scratch_shapes=[
            pltpu.VMEM((T, EVEN_IN), F32),
            pltpu.VMEM((T + CONV_HALO, CONV_WIDTH), F32),
            pltpu.VMEM((T + POOL_HALO, POOL_WIDTH), F32),
            pltpu.VMEM((T, EVEN_OUT), BF16),
        ],
        compiler_params=pltpu.CompilerParams(
            dimension_semantics=("arbitrary",), vmem_limit_bytes=VMEM_LIMIT),
        name="even_layer",
    )(x, norm_g.reshape(1, D_MODEL), w_in.astype(BF16), conv_w, pool_w.astype(BF16),
      pool_scale.reshape(1, POOL_WIDTH), w_out.astype(BF16))


def _lane_scan(x, op, identity):
    n = x.shape[1]
    lane = lax.broadcasted_iota(jnp.int32, x.shape, 1)
    k = 1
    while k < n:
        x = op(x, jnp.where(lane >= k, pltpu.roll(x, k, 1), identity))
        k *= 2
    return x


def _odd_kernel(sink_ref, x_ref, g_ref, wa_ref, wb_ref, wg_ref, bias_ref, wout_ref, fg_ref, o_ref,
                proj_scr, kvbuf, abias, c_scr, m_scr, y_scr):
    i = pl.program_id(0)
    T = x_ref.shape[0]
    NB = T // BLOCK
    H = MLSTM_HEADS
    L = min(MCHUNK, T)
    NEG = -jnp.inf

    @pl.when(i == 0)
    def _():
        for b in range(2 * ATT_KV_HEADS):
            kvbuf[b, 0:BLOCK, :] = jnp.zeros((BLOCK, LANES), BF16)
        c_scr[...] = jnp.zeros(c_scr.shape, F32)
        m_scr[...] = jnp.zeros(m_scr.shape, F32)
        qi = lax.broadcasted_iota(jnp.int32, (BLOCK, 2 * BLOCK), 0)
        kj = lax.broadcasted_iota(jnp.int32, (BLOCK, 2 * BLOCK), 1)
        dist = qi + BLOCK - kj
        ok = (dist >= 0) & (dist < WINDOW)
        distf = dist.astype(F32)
        for hd in range(ATT_HEADS):
            slope = 2.0 ** (-8.0 * (hd + 1) / ATT_HEADS)
            abias[hd * BLOCK:(hd + 1) * BLOCK, :] = jnp.where(ok, -slope * distf, NEG)

    h = _rmsnorm(x_ref[...], g_ref[...]).astype(BF16)
    proj_scr[:, 0:O_DO] = jnp.dot(h, wa_ref[...], preferred_element_type=F32)
    proj_scr[:, O_DO:O_IF] = jnp.dot(h, wb_ref[...], preferred_element_type=F32)
    proj_scr[:, O_IF:ODD_IN_PAD] = jnp.dot(h, wg_ref[...], preferred_element_type=F32)

    lane_q = lax.broadcasted_iota(jnp.int32, (BLOCK, LANES), 1)
    low = lane_q < ATT_HEAD_DIM
    low_t = lax.broadcasted_iota(jnp.int32, (T, LANES), 1) < ATT_HEAD_DIM
    for b, off in enumerate((O_CK, O_CV)):
        kv = proj_scr[:, off:off + ATT_KV_WIDTH]
        sw = pltpu.roll(kv, ATT_HEAD_DIM, 1)
        kvbuf[2 * b, BLOCK:BLOCK + T, :] = jnp.where(low_t, kv, sw).astype(BF16)
        kvbuf[2 * b + 1, BLOCK:BLOCK + T, :] = jnp.where(low_t, sw, kv).astype(BF16)
    key_lane = lax.broadcasted_iota(jnp.int32, (BLOCK, 2 * BLOCK), 1)
    no_prev = jnp.where(key_lane < BLOCK, jnp.where(i == 0, NEG, 0.0), 0.0)
    scale = ATT_HEAD_DIM ** -0.5
    for j in range(NB):
        r0 = j * BLOCK
        for g in range(ATT_KV_HEADS):
            kk = kvbuf[g, r0:r0 + 2 * BLOCK, :]
            vv = kvbuf[ATT_KV_HEADS + g, r0:r0 + 2 * BLOCK, :]
            qs = []
            for p in range(g * ATT_GROUP // 2, (g + 1) * ATT_GROUP // 2):
                qp = proj_scr[r0:r0 + BLOCK, O_CQ + p * LANES:O_CQ + (p + 1) * LANES] * scale
                qs.append(jnp.where(low, qp, 0.0).astype(BF16))
                qs.append(jnp.where(low, 0.0, qp).astype(BF16))
            s_all = lax.dot_general(jnp.concatenate(qs, axis=0), kk, (((1,), (1,)), ((), ())),
                                    preferred_element_type=F32)
            ps, ls = [], []
            for r in range(ATT_GROUP):
                hd = g * ATT_GROUP + r
                s = s_all[r * BLOCK:(r + 1) * BLOCK, :] + abias[hd * BLOCK:(hd + 1) * BLOCK, :]
                if j == 0:
                    s = s + no_prev
                sink = sink_ref[hd]
                m = jnp.maximum(jnp.max(s, axis=1, keepdims=True), sink)
                e = jnp.exp(s - m)
                ls.append(jnp.sum(e, axis=1, keepdims=True) + jnp.exp(sink - m))
                ps.append(e.astype(BF16))
            o_all = jnp.dot(jnp.concatenate(ps, axis=0), vv, preferred_element_type=F32)
            for pp in range(ATT_GROUP // 2):
                p = g * ATT_GROUP // 2 + pp
                o_lo = o_all[2 * pp * BLOCK:(2 * pp + 1) * BLOCK, :] / ls[2 * pp]
                o_hi = o_all[(2 * pp + 1) * BLOCK:(2 * pp + 2) * BLOCK, :] / ls[2 * pp + 1]
                att = jnp.where(low, o_lo, o_hi)
                gate = proj_scr[r0:r0 + BLOCK, O_GC + p * LANES:O_GC + (p + 1) * LANES]
                y_scr[r0:r0 + BLOCK, p * LANES:(p + 1) * LANES] = (att * jax.nn.silu(gate)).astype(BF16)
    for b in range(2 * ATT_KV_HEADS):
        kvbuf[b, 0:BLOCK, :] = kvbuf[b, T:T + BLOCK, :]

    gt = proj_scr[:, O_IF:O_IF + GATE_PAD].T[0:2 * H, :]
    pre = gt + jnp.concatenate([bias_ref[...]] * (T // LANES), axis=1)
    row = lax.broadcasted_iota(jnp.int32, (2 * H, T), 0)
    top = row < H
    bcum = _lane_scan(jax.nn.log_sigmoid(pre), jnp.add, 0.0)
    bsw = pltpu.roll(bcum, H, 0)
    a_top = pre - bsw
    ag = jnp.where(top, a_top, pltpu.roll(a_top, H, 0))
    bf = jnp.where(top, bsw, bcum)
    m_in = jnp.concatenate([m_scr[...]] * (T // LANES), axis=1)
    mg = jnp.maximum(m_in, _lane_scan(ag, jnp.maximum, NEG))
    m_t = bf + mg
    nrm = jnp.exp(-m_t)
    inter_parts, wk_parts, decay_parts = [], [], []
    for c in range(T // L):
        m_start = m_in[:, 0:1] if c == 0 else mg[:, c * L - 1:c * L]
        m_last = mg[:, (c + 1) * L - 1:(c + 1) * L]
        inter_parts.append(jnp.exp(m_start - mg[:, c * L:(c + 1) * L]))
        wk_parts.append(jnp.exp(ag[:, c * L:(c + 1) * L] - m_last))
        decay_parts.append(jnp.broadcast_to(jnp.exp(m_start - m_last), (2 * H, 2 * MLSTM_V_DIM)))
    inter = jnp.concatenate(inter_parts, axis=1)
    m_scr[...] = jnp.broadcast_to(m_t[:, T - 1:T], m_scr.shape)
    cols = jnp.concatenate([mg, inter, nrm, jnp.zeros((LANES - 6 * H, T), F32)], axis=0).T

    kT = proj_scr[:, O_DK:O_DK + MLSTM_QK_WIDTH].T
    lane_p = lax.broadcasted_iota(jnp.int32, (L, LANES), 1)
    low_p = lane_p < MLSTM_QK_DIM
    tri = (lax.broadcasted_iota(jnp.int32, (L, L), 1) <= lax.broadcasted_iota(jnp.int32, (L, L), 0))
    ones = jnp.ones((L, MLSTM_V_DIM), BF16)
    qscale = MLSTM_QK_DIM ** -0.5
    for c in range(T // L):
        r0 = c * L
        for pair in range(H // 2):
            qp = proj_scr[r0:r0 + L, O_DQ + pair * LANES:O_DQ + (pair + 1) * LANES] * qscale
            kT_pair = kT[pair * LANES:(pair + 1) * LANES, r0:r0 + L].astype(BF16)
            c_pair = c_scr[pair * LANES:(pair + 1) * LANES, :].astype(BF16)
            for half in range(2):
                hd = 2 * pair + half
                q_h = (jnp.where(low_p, qp, 0.0) if half == 0 else jnp.where(low_p, 0.0, qp)).astype(BF16)
                s = jnp.dot(q_h, kT_pair, preferred_element_type=F32)
                arg = ag[hd:hd + 1, r0:r0 + L] - cols[r0:r0 + L, hd:hd + 1]
                e = jnp.exp(jnp.where(tri, arg, NEG))
                p_bf = (s * e).astype(BF16)
                v_aug = jnp.concatenate(
                    [proj_scr[r0:r0 + L, O_DV + hd * MLSTM_V_DIM:O_DV + (hd + 1) * MLSTM_V_DIM].astype(BF16), ones],
                    axis=1)
                num = (jnp.dot(p_bf, v_aug, preferred_element_type=F32)
                       + cols[r0:r0 + L, 2 * H + hd:2 * H + hd + 1]
                       * jnp.dot(q_h, c_pair, preferred_element_type=F32))
                den = jnp.maximum(jnp.abs(num[:, MLSTM_V_DIM:]), cols[r0:r0 + L, 4 * H + hd:4 * H + hd + 1])
                hm = num[:, :MLSTM_V_DIM] / den
                og = proj_scr[r0:r0 + L, O_DO + hd * MLSTM_V_DIM:O_DO + (hd + 1) * MLSTM_V_DIM]
                gd = proj_scr[r0:r0 + L, O_GD + hd * MLSTM_V_DIM:O_GD + (hd + 1) * MLSTM_V_DIM]
                y_scr[r0:r0 + L, ATT_WIDTH + hd * MLSTM_V_DIM:ATT_WIDTH + (hd + 1) * MLSTM_V_DIM] = (
                    hm * jax.nn.sigmoid(og) * jax.nn.silu(gd)).astype(BF16)
                kw = (kT[hd * MLSTM_QK_DIM:(hd + 1) * MLSTM_QK_DIM, r0:r0 + L]
                      * wk_parts[c][hd:hd + 1, :]).astype(BF16)
                upd = jnp.dot(kw, v_aug, preferred_element_type=F32)
                rows = slice(hd * MLSTM_QK_DIM, (hd + 1) * MLSTM_QK_DIM)
                c_scr[rows, :] = decay_parts[c][hd:hd + 1, :] * c_scr[rows, :] + upd

    x2 = x_ref[...] + jnp.dot(y_scr[...], wout_ref[...], preferred_element_type=F32)
    o_ref[...] = _rmsnorm(x2, fg_ref[...])


def _odd_layer(x, norm_g, w_in, i_bias, f_bias, sinks, w_out, final_g):
    S = x.shape[0]
    T = min(TILE, S)
    assert S % T == 0 and T % BLOCK == 0 and T % min(MCHUNK, T) == 0
    o_if = O_DO
    o_do = o_if + 2 * MLSTM_HEADS
    wa = w_in[:, :o_if].astype(BF16)
    wb = w_in[:, o_do:].astype(BF16)
    wg = jnp.pad(w_in[:, o_if:o_do], ((0, 0), (0, GATE_PAD - 2 * MLSTM_HEADS))).astype(BF16)
    wo = w_out.astype(BF16)
    bias = jnp.broadcast_to(jnp.concatenate([i_bias, f_bias]).astype(F32)[:, None], (2 * MLSTM_HEADS, LANES))
    const = lambda *shape: pl.BlockSpec(shape, lambda i: (0,) * len(shape))
    return pl.pallas_call(
        _odd_kernel,
        grid=(S // T,),
        in_specs=[
            pl.BlockSpec(memory_space=pltpu.SMEM),
            pl.BlockSpec((T, D_MODEL), lambda i: (i, 0)),
            const(1, D_MODEL),
            const(D_MODEL, O_DO),
            const(D_MODEL, O_IF - O_DO),
            const(D_MODEL, GATE_PAD),
            const(2 * MLSTM_HEADS, LANES),
            const(ODD_OUT, D_MODEL),
            const(1, D_MODEL),
        ],
        out_specs=pl.BlockSpec((T, D_MODEL), lambda i: (i, 0)),
        out_shape=jax.ShapeDtypeStruct((S, D_MODEL), F32),
        scratch_shapes=[
            pltpu.VMEM((T, ODD_IN_PAD), F32),
            pltpu.VMEM((2 * ATT_KV_HEADS, T + BLOCK, LANES), BF16),
            pltpu.VMEM((ATT_HEADS * BLOCK, 2 * BLOCK), F32),
            pltpu.VMEM((MLSTM_HEADS * MLSTM_QK_DIM, 2 * MLSTM_V_DIM), F32),
            pltpu.VMEM((2 * MLSTM_HEADS, LANES), F32),
            pltpu.VMEM((T, ODD_OUT), BF16),
        ],
        compiler_params=pltpu.CompilerParams(
            dimension_semantics=("arbitrary",), vmem_limit_bytes=VMEM_LIMIT),
        name="odd_layer",
    )(sinks.astype(F32), x, norm_g.reshape(1, D_MODEL), wa, wb, wg, bias, wo, final_g.reshape(1, D_MODEL))


def kernel(x, even_norm, even_w_in, even_conv_w, even_pool_w, even_pool_scale, even_w_out,
           odd_norm, odd_w_in, odd_i_bias, odd_f_bias, odd_sinks, odd_w_out, final_norm):
    B, S, D = x.shape
    assert B == 1 and D == D_MODEL
    assert even_norm.shape[0] == 1 and odd_norm.shape[0] == 1
    h = x.reshape(S, D)
    h = _even_layer(h, even_norm[0], even_w_in[0], even_conv_w[0], even_pool_w[0],
                    even_pool_scale[0], even_w_out[0])
    h = _odd_layer(h, odd_norm[0], odd_w_in[0], odd_i_bias[0], odd_f_bias[0], odd_sinks[0],
                   odd_w_out[0], final_norm)
    return h.reshape(B, S, D)
```

```python
import functools

import jax
import jax.numpy as jnp
import numpy as np
from jax import lax
from jax.experimental import pallas as pl
from jax.experimental.pallas import tpu as pltpu

F32 = jnp.float32
BF16 = jnp.bfloat16

D_MODEL = 1024
RMS_EPS = 1e-6

CONV_WIDTH = 512
CONV_K = 3
POOL_WINDOWS = (2, 4, 8, 16)
POOL_WIDTH = 512
POOL_GROUP = POOL_WIDTH // len(POOL_WINDOWS)
EVEN_IN = 4 * CONV_WIDTH + 2 * POOL_WIDTH
EVEN_OUT = CONV_WIDTH + POOL_WIDTH
CONV_HALO = 8
POOL_HALO = 16

ATT_HEADS = 8
ATT_KV_HEADS = 2
ATT_GROUP = ATT_HEADS // ATT_KV_HEADS
ATT_HEAD_DIM = 64
ATT_WIDTH = ATT_HEADS * ATT_HEAD_DIM
ATT_KV_WIDTH = ATT_KV_HEADS * ATT_HEAD_DIM
WINDOW = 128
BLOCK = 128
MLSTM_HEADS = 4
MLSTM_QK_DIM = 64
MLSTM_V_DIM = 128
MLSTM_QK_WIDTH = MLSTM_HEADS * MLSTM_QK_DIM
MLSTM_WIDTH = MLSTM_HEADS * MLSTM_V_DIM
ODD_OUT = ATT_WIDTH + MLSTM_WIDTH
LANES = 128
GATE_PAD = LANES

O_CQ = 0
O_CK = O_CQ + ATT_WIDTH
O_CV = O_CK + ATT_KV_WIDTH
O_GC = O_CV + ATT_KV_WIDTH
O_DQ = O_GC + ATT_WIDTH
O_DK = O_DQ + MLSTM_QK_WIDTH
O_DV = O_DK + MLSTM_QK_WIDTH
O_DO = O_DV + MLSTM_WIDTH
O_GD = O_DO + MLSTM_WIDTH
O_IF = O_GD + MLSTM_WIDTH
ODD_IN_PAD = O_IF + GATE_PAD

TILE_EVEN = 1024
TILE_ODD = 512
MCHUNK = 256
VMEM_LIMIT = 56 * 1024 * 1024


def _const_spec(*shape):
    return pl.BlockSpec(shape, lambda i: (0,) * len(shape), pipeline_mode=pl.Buffered(1))


def _rmsnorm(x, g):
    ms = jnp.mean(x * x, axis=-1, keepdims=True)
    return x * lax.rsqrt(ms + RMS_EPS) * g


def _even_kernel(x_ref, g_ref, win_ref, cw_ref, pw_ref, ps_ref, wout_ref, o_ref,
                 proj_scr, uc_scr, up_scr, y_scr):
    i = pl.program_id(0)
    T = x_ref.shape[0]
    C = CONV_WIDTH

    @pl.when(i == 0)
    def _():
        uc_scr[0:CONV_HALO, :] = jnp.zeros((CONV_HALO, C), F32)
        up_scr[0:POOL_HALO, :] = jnp.zeros((POOL_HALO, POOL_WIDTH), F32)

    h = _rmsnorm(x_ref[...], g_ref[...]).astype(BF16)
    proj_scr[...] = jnp.dot(h, win_ref[...], preferred_element_type=F32)

    uc = proj_scr[:, 2 * C:3 * C] * proj_scr[:, 0:C]
    uc_scr[CONV_HALO:CONV_HALO + T, :] = uc
    conv = (cw_ref[2:3, :] * uc
            + cw_ref[1:2, :] * uc_scr[CONV_HALO - 1:CONV_HALO - 1 + T, :]
            + cw_ref[0:1, :] * uc_scr[CONV_HALO - 2:CONV_HALO - 2 + T, :])
    ya = proj_scr[:, C:2 * C] * conv * jax.nn.silu(proj_scr[:, 3 * C:4 * C])
    y_scr[:, 0:C] = ya.astype(BF16)
    uc_scr[0:CONV_HALO, :] = uc_scr[T:T + CONV_HALO, :]

    up_scr[POOL_HALO:POOL_HALO + T, :] = proj_scr[:, 4 * C:4 * C + POOL_WIDTH]
    t1 = (lax.broadcasted_iota(jnp.int32, (T, POOL_GROUP), 0) + (i * T + 1)).astype(F32)
    for g, w in enumerate(POOL_WINDOWS):
        lo, hi = g * POOL_GROUP, (g + 1) * POOL_GROUP
        s = up_scr[:, lo:hi]
        k = 1
        while k < w:
            s = s + pltpu.roll(s, k, 0)
            k *= 2
        u = up_scr[POOL_HALO:POOL_HALO + T, lo:hi]
        d = s[POOL_HALO:, :] / jnp.minimum(t1, float(w)) - u
        yb = jnp.dot(d.astype(BF16), pw_ref[g], preferred_element_type=F32)
        gate = proj_scr[:, 4 * C + POOL_WIDTH + lo:4 * C + POOL_WIDTH + hi]
        y_scr[:, C + lo:C + hi] = (yb * ps_ref[:, lo:hi] * jax.nn.silu(gate)).astype(BF16)
    up_scr[0:POOL_HALO, :] = up_scr[T:T + POOL_HALO, :]

    o_ref[...] = x_ref[...] + jnp.dot(y_scr[...], wout_ref[...], preferred_element_type=F32)


def _even_layer(x, norm_g, w_in, conv_w, pool_w, pool_scale, w_out):
    S = x.shape[0]
    T = min(TILE_EVEN, S)
    assert S % T == 0
    const = _const_spec
    return pl.pallas_call(
        _even_kernel,
        grid=(S // T,),
        in_specs=[
            pl.BlockSpec((T, D_MODEL), lambda i: (i, 0)),
            const(1, D_MODEL),
            const(D_MODEL, EVEN_IN),
            const(CONV_K, CONV_WIDTH),
            const(len(POOL_WINDOWS), POOL_GROUP, POOL_GROUP),
            const(1, POOL_WIDTH),
            const(EVEN_OUT, D_MODEL),
        ],
        out_specs=pl.BlockSpec((T, D_MODEL), lambda i: (i, 0)),
        out_shape=jax.ShapeDtypeStruct((S, D_MODEL), F32),
        scratch_shapes=[
            pltpu.VMEM((T, EVEN_IN), F32),
            pltpu.VMEM((T + CONV_HALO, CONV_WIDTH), F32),
            pltpu.VMEM((T + POOL_HALO, POOL_WIDTH), F32),
            pltpu.VMEM((T, EVEN_OUT), BF16),
        ],
        compiler_params=pltpu.CompilerParams(
            dimension_semantics=("arbitrary",), vmem_limit_bytes=VMEM_LIMIT),
        name="even_layer",
    )(x, norm_g.reshape(1, D_MODEL), w_in.astype(BF16), conv_w, pool_w.astype(BF16),
      pool_scale.reshape(1, POOL_WIDTH), w_out.astype(BF16))


def _lane_scan(x, op, identity):
    n = x.shape[1]
    lane = lax.broadcasted_iota(jnp.int32, x.shape, 1)
    k = 1
    while k < n:
        x = op(x, jnp.where(lane >= k, pltpu.roll(x, k, 1), identity))
        k *= 2
    return x


def _odd_kernel(sink_ref, x_ref, g_ref, wa_ref, wb_ref, wg_ref, bias_ref, wout_ref, fg_ref, o_ref,
                proj_scr, kvbuf, abias, c_scr, m_scr, y_scr):
    i = pl.program_id(0)
    T = x_ref.shape[0]
    NB = T // BLOCK
    H = MLSTM_HEADS
    L = min(MCHUNK, T)
    NEG = -jnp.inf

    @pl.when(i == 0)
    def _():
        for b in range(2 * ATT_KV_HEADS):
            kvbuf[b, 0:BLOCK, :] = jnp.zeros((BLOCK, LANES), BF16)
        c_scr[...] = jnp.zeros(c_scr.shape, F32)
        m_scr[...] = jnp.zeros(m_scr.shape, F32)
        qi = lax.broadcasted_iota(jnp.int32, (BLOCK, 2 * BLOCK), 0)
        kj = lax.broadcasted_iota(jnp.int32, (BLOCK, 2 * BLOCK), 1)
        dist = qi + BLOCK - kj
        ok = (dist >= 0) & (dist < WINDOW)
        distf = dist.astype(F32)
        for hd in range(ATT_HEADS):
            slope = 2.0 ** (-8.0 * (hd + 1) / ATT_HEADS)
            abias[hd * BLOCK:(hd + 1) * BLOCK, :] = jnp.where(ok, -slope * distf, NEG)

    h = _rmsnorm(x_ref[...], g_ref[...]).astype(BF16)
    proj_scr[:, 0:O_DO] = jnp.dot(h, wa_ref[...], preferred_element_type=F32)
    proj_scr[:, O_DO:O_IF] = jnp.dot(h, wb_ref[...], preferred_element_type=F32)
    proj_scr[:, O_IF:ODD_IN_PAD] = jnp.dot(h, wg_ref[...], preferred_element_type=F32)

    lane_q = lax.broadcasted_iota(jnp.int32, (BLOCK, LANES), 1)
    low = lane_q < ATT_HEAD_DIM
    low_t = lax.broadcasted_iota(jnp.int32, (T, LANES), 1) < ATT_HEAD_DIM
    for b, off in enumerate((O_CK, O_CV)):
        kv = proj_scr[:, off:off + ATT_KV_WIDTH]
        sw = pltpu.roll(kv, ATT_HEAD_DIM, 1)
        kvbuf[2 * b, BLOCK:BLOCK + T, :] = jnp.where(low_t, kv, sw).astype(BF16)
        kvbuf[2 * b + 1, BLOCK:BLOCK + T, :] = jnp.where(low_t, sw, kv).astype(BF16)
    key_lane = lax.broadcasted_iota(jnp.int32, (BLOCK, 2 * BLOCK), 1)
    no_prev = jnp.where(key_lane < BLOCK, jnp.where(i == 0, NEG, 0.0), 0.0)
    scale = ATT_HEAD_DIM ** -0.5
    for j in range(NB):
        r0 = j * BLOCK
        for g in range(ATT_KV_HEADS):
            kk = kvbuf[g, r0:r0 + 2 * BLOCK, :]
            vv = kvbuf[ATT_KV_HEADS + g, r0:r0 + 2 * BLOCK, :]
            qs = []
            for p in range(g * ATT_GROUP // 2, (g + 1) * ATT_GROUP // 2):
                qp = proj_scr[r0:r0 + BLOCK, O_CQ + p * LANES:O_CQ + (p + 1) * LANES] * scale
                qs.append(jnp.where(low, qp, 0.0).astype(BF16))
                qs.append(jnp.where(low, 0.0, qp).astype(BF16))
            s_all = lax.dot_general(jnp.concatenate(qs, axis=0), kk, (((1,), (1,)), ((), ())),
                                    preferred_element_type=F32)
            ps, ls = [], []
            for r in range(ATT_GROUP):
                hd = g * ATT_GROUP + r
                s = s_all[r * BLOCK:(r + 1) * BLOCK, :] + abias[hd * BLOCK:(hd + 1) * BLOCK, :]
                if j == 0:
                    s = s + no_prev
                sink = sink_ref[hd]
                m = jnp.maximum(jnp.max(s, axis=1, keepdims=True), sink)
                e = jnp.exp(s - m)
                ls.append(jnp.sum(e, axis=1, keepdims=True) + jnp.exp(sink - m))
                ps.append(e.astype(BF16))
            o_all = jnp.dot(jnp.concatenate(ps, axis=0), vv, preferred_element_type=F32)
            for pp in range(ATT_GROUP // 2):
                p = g * ATT_GROUP // 2 + pp
                o_lo = o_all[2 * pp * BLOCK:(2 * pp + 1) * BLOCK, :] / ls[2 * pp]
                o_hi = o_all[(2 * pp + 1) * BLOCK:(2 * pp + 2) * BLOCK, :] / ls[2 * pp + 1]
                att = jnp.where(low, o_lo, o_hi)
                gate = proj_scr[r0:r0 + BLOCK, O_GC + p * LANES:O_GC + (p + 1) * LANES]
                y_scr[r0:r0 + BLOCK, p * LANES:(p + 1) * LANES] = (att * jax.nn.silu(gate)).astype(BF16)
    for b in range(2 * ATT_KV_HEADS):
        kvbuf[b, 0:BLOCK, :] = kvbuf[b, T:T + BLOCK, :]

    gt = proj_scr[:, O_IF:O_IF + GATE_PAD].T[0:2 * H, :]
    pre = gt + jnp.concatenate([bias_ref[...]] * (T // LANES), axis=1)
    row = lax.broadcasted_iota(jnp.int32, (2 * H, T), 0)
    top = row < H
    bcum = _lane_scan(jax.nn.log_sigmoid(pre), jnp.add, 0.0)
    bsw = pltpu.roll(bcum, H, 0)
    a_top = pre - bsw
    ag = jnp.where(top, a_top, pltpu.roll(a_top, H, 0))
    bf = jnp.where(top, bsw, bcum)
    m_in = jnp.concatenate([m_scr[...]] * (T // LANES), axis=1)
    mg = jnp.maximum(m_in, _lane_scan(ag, jnp.maximum, NEG))
    m_t = bf + mg
    nrm = jnp.exp(-m_t)
    inter_parts, wk_parts, decay_parts = [], [], []
    for c in range(T // L):
        m_start = m_in[:, 0:1] if c == 0 else mg[:, c * L - 1:c * L]
        m_last = mg[:, (c + 1) * L - 1:(c + 1) * L]
        inter_parts.append(jnp.exp(m_start - mg[:, c * L:(c + 1) * L]))
        wk_parts.append(jnp.exp(ag[:, c * L:(c + 1) * L] - m_last))
        decay_parts.append(jnp.broadcast_to(jnp.exp(m_start - m_last), (2 * H, 2 * MLSTM_V_DIM)))
    inter = jnp.concatenate(inter_parts, axis=1)
    m_scr[...] = jnp.broadcast_to(m_t[:, T - 1:T], m_scr.shape)
    cols = jnp.concatenate([mg, inter, nrm, jnp.zeros((LANES - 6 * H, T), F32)], axis=0).T

    kT = proj_scr[:, O_DK:O_DK + MLSTM_QK_WIDTH].T
    lane_p = lax.broadcasted_iota(jnp.int32, (L, LANES), 1)
    low_p = lane_p < MLSTM_QK_DIM
    tri = (lax.broadcasted_iota(jnp.int32, (L, L), 1) <= lax.broadcasted_iota(jnp.int32, (L, L), 0))
    ones = jnp.ones((L, MLSTM_V_DIM), BF16)
    qscale = MLSTM_QK_DIM ** -0.5
    for c in range(T // L):
        r0 = c * L
        for pair in range(H // 2):
            qp = proj_scr[r0:r0 + L, O_DQ + pair * LANES:O_DQ + (pair + 1) * LANES] * qscale
            kT_pair = kT[pair * LANES:(pair + 1) * LANES, r0:r0 + L].astype(BF16)
            c_pair = c_scr[pair * LANES:(pair + 1) * LANES, :].astype(BF16)
            for half in range(2):
                hd = 2 * pair + half
                q_h = (jnp.where(low_p, qp, 0.0) if half == 0 else jnp.where(low_p, 0.0, qp)).astype(BF16)
                s = jnp.dot(q_h, kT_pair, preferred_element_type=F32)
                arg = ag[hd:hd + 1, r0:r0 + L] - cols[r0:r0 + L, hd:hd + 1]
                e = jnp.exp(jnp.where(tri, arg, NEG))
                p_bf = (s * e).astype(BF16)
                v_aug = jnp.concatenate(
                    [proj_scr[r0:r0 + L, O_DV + hd * MLSTM_V_DIM:O_DV + (hd + 1) * MLSTM_V_DIM].astype(BF16), ones],
                    axis=1)
                num = (jnp.dot(p_bf, v_aug, preferred_element_type=F32)
                       + cols[r0:r0 + L, 2 * H + hd:2 * H + hd + 1]
                       * jnp.dot(q_h, c_pair, preferred_element_type=F32))
                den = jnp.maximum(jnp.abs(num[:, MLSTM_V_DIM:]), cols[r0:r0 + L, 4 * H + hd:4 * H + hd + 1])
                hm = num[:, :MLSTM_V_DIM] / den
                og = proj_scr[r0:r0 + L, O_DO + hd * MLSTM_V_DIM:O_DO + (hd + 1) * MLSTM_V_DIM]
                gd = proj_scr[r0:r0 + L, O_GD + hd * MLSTM_V_DIM:O_GD + (hd + 1) * MLSTM_V_DIM]
                y_scr[r0:r0 + L, ATT_WIDTH + hd * MLSTM_V_DIM:ATT_WIDTH + (hd + 1) * MLSTM_V_DIM] = (
                    hm * jax.nn.sigmoid(og) * jax.nn.silu(gd)).astype(BF16)
                kw = (kT[hd * MLSTM_QK_DIM:(hd + 1) * MLSTM_QK_DIM, r0:r0 + L]
                      * wk_parts[c][hd:hd + 1, :]).astype(BF16)
                upd = jnp.dot(kw, v_aug, preferred_element_type=F32)
                rows = slice(hd * MLSTM_QK_DIM, (hd + 1) * MLSTM_QK_DIM)
                c_scr[rows, :] = decay_parts[c][hd:hd + 1, :] * c_scr[rows, :] + upd

    x2 = x_ref[...] + jnp.dot(y_scr[...], wout_ref[...], preferred_element_type=F32)
    o_ref[...] = _rmsnorm(x2, fg_ref[...])


def _odd_layer(x, norm_g, w_in, i_bias, f_bias, sinks, w_out, final_g):
    S = x.shape[0]
    T = min(TILE_ODD, S)
    assert S % T == 0 and T % BLOCK == 0 and T % min(MCHUNK, T) == 0
    o_if = O_DO
    o_do = o_if + 2 * MLSTM_HEADS
    wa = w_in[:, :o_if].astype(BF16)
    wb = w_in[:, o_do:].astype(BF16)
    wg = jnp.pad(w_in[:, o_if:o_do], ((0, 0), (0, GATE_PAD - 2 * MLSTM_HEADS))).astype(BF16)
    wo = w_out.astype(BF16)
    bias = jnp.broadcast_to(jnp.concatenate([i_bias, f_bias]).astype(F32)[:, None], (2 * MLSTM_HEADS, LANES))
    const = _const_spec
    return pl.pallas_call(
        _odd_kernel,
        grid=(S // T,),
        in_specs=[
            pl.BlockSpec(memory_space=pltpu.SMEM),
            pl.BlockSpec((T, D_MODEL), lambda i: (i, 0)),
            const(1, D_MODEL),
            const(D_MODEL, O_DO),
            const(D_MODEL, O_IF - O_DO),
            const(D_MODEL, GATE_PAD),
            const(2 * MLSTM_HEADS, LANES),
            const(ODD_OUT, D_MODEL),
            const(1, D_MODEL),
        ],
        out_specs=pl.BlockSpec((T, D_MODEL), lambda i: (i, 0)),
        out_shape=jax.ShapeDtypeStruct((S, D_MODEL), F32),
        scratch_shapes=[
            pltpu.VMEM((T, ODD_IN_PAD), F32),
            pltpu.VMEM((2 * ATT_KV_HEADS, T + BLOCK, LANES), BF16),
            pltpu.VMEM((ATT_HEADS * BLOCK, 2 * BLOCK), F32),
            pltpu.VMEM((MLSTM_HEADS * MLSTM_QK_DIM, 2 * MLSTM_V_DIM), F32),
            pltpu.VMEM((2 * MLSTM_HEADS, LANES), F32),
            pltpu.VMEM((T, ODD_OUT), BF16),
        ],
        compiler_params=pltpu.CompilerParams(
            dimension_semantics=("arbitrary",), vmem_limit_bytes=VMEM_LIMIT),
        name="odd_layer",
    )(sinks.astype(F32), x, norm_g.reshape(1, D_MODEL), wa, wb, wg, bias, wo, final_g.reshape(1, D_MODEL))


def kernel(x, even_norm, even_w_in, even_conv_w, even_pool_w, even_pool_scale, even_w_out,
           odd_norm, odd_w_in, odd_i_bias, odd_f_bias, odd_sinks, odd_w_out, final_norm):
    B, S, D = x.shape
    assert B == 1 and D == D_MODEL
    assert even_norm.shape[0] == 1 and odd_norm.shape[0] == 1
    h = x.reshape(S, D)
    h = _even_layer(h, even_norm[0], even_w_in[0], even_conv_w[0], even_pool_w[0],
                    even_pool_scale[0], even_w_out[0])
    h = _odd_layer(h, odd_norm[0], odd_w_in[0], odd_i_bias[0], odd_f_bias[0], odd_sinks[0],
                   odd_w_out[0], final_norm)
    return h.reshape(B, S, D)
```

```python
import jax
import jax.numpy as jnp
from jax import lax
from jax.experimental import pallas as pl
from jax.experimental.pallas import tpu as pltpu

F32 = jnp.float32
BF16 = jnp.bfloat16

D_MODEL = 1024
RMS_EPS = 1e-6

CONV_WIDTH = 512
CONV_K = 3
POOL_WINDOWS = (2, 4, 8, 16)
POOL_WIDTH = 512
POOL_GROUP = POOL_WIDTH // len(POOL_WINDOWS)
EVEN_IN = 4 * CONV_WIDTH + 2 * POOL_WIDTH
EVEN_OUT = CONV_WIDTH + POOL_WIDTH
CONV_HALO = 8
POOL_HALO = 16

ATT_HEADS = 8
ATT_KV_HEADS = 2
ATT_GROUP = ATT_HEADS // ATT_KV_HEADS
ATT_HEAD_DIM = 64
ATT_WIDTH = ATT_HEADS * ATT_HEAD_DIM
ATT_KV_WIDTH = ATT_KV_HEADS * ATT_HEAD_DIM
WINDOW = 128
BLOCK = 128
MLSTM_HEADS = 4
MLSTM_QK_DIM = 64
MLSTM_V_DIM = 128
MLSTM_QK_WIDTH = MLSTM_HEADS * MLSTM_QK_DIM
MLSTM_WIDTH = MLSTM_HEADS * MLSTM_V_DIM
ODD_OUT = ATT_WIDTH + MLSTM_WIDTH
LANES = 128
MXU_WIDTH = 256
GATE_PAD = LANES

O_CQ = 0
O_CK = O_CQ + ATT_WIDTH
O_CV = O_CK + ATT_KV_WIDTH
O_GC = O_CV + ATT_KV_WIDTH
O_DQ = O_GC + ATT_WIDTH
O_DK = O_DQ + MLSTM_QK_WIDTH
O_DV = O_DK + MLSTM_QK_WIDTH
O_DO = O_DV + MLSTM_WIDTH
O_GD = O_DO + MLSTM_WIDTH
O_IF = O_GD + MLSTM_WIDTH
ODD_IN_PAD = O_IF + GATE_PAD

TILE_EVEN = 1024
TILE_ODD = 512
MCHUNK = 256
INPROJ_ROWS = 512
OUT_ROWS = 256
PRE_CHUNKS = 3
MLSTM_CHUNKS = 5
LOG2E = 1.4426950408889634
VMEM_LIMIT = 60 * 1024 * 1024


def _const_spec(*shape):
    return pl.BlockSpec(shape, lambda i: (0,) * len(shape), pipeline_mode=pl.Buffered(1))


def _rmsnorm(x, g):
    ms = jnp.mean(x * x, axis=-1, keepdims=True)
    return x * lax.rsqrt(ms + RMS_EPS) * g


def _even_kernel(x_ref, g_ref, win_ref, cw_ref, pw_ref, ps_ref, wout_ref, o_ref,
                 proj_scr, uc_scr, up_scr, y_scr):
    i = pl.program_id(0)
    T = x_ref.shape[0]
    C = CONV_WIDTH

    @pl.when(i == 0)
    def _():
        uc_scr[0:CONV_HALO, :] = jnp.zeros((CONV_HALO, C), F32)
        up_scr[0:POOL_HALO, :] = jnp.zeros((POOL_HALO, POOL_WIDTH), F32)

    h = _rmsnorm(x_ref[...], g_ref[...]).astype(BF16)
    proj_scr[...] = jnp.dot(h, win_ref[...], preferred_element_type=F32)

    uc = proj_scr[:, 2 * C:3 * C] * proj_scr[:, 0:C]
    uc_scr[CONV_HALO:CONV_HALO + T, :] = uc
    conv = (cw_ref[2:3, :] * uc
            + cw_ref[1:2, :] * uc_scr[CONV_HALO - 1:CONV_HALO - 1 + T, :]
            + cw_ref[0:1, :] * uc_scr[CONV_HALO - 2:CONV_HALO - 2 + T, :])
    ya = proj_scr[:, C:2 * C] * conv * jax.nn.silu(proj_scr[:, 3 * C:4 * C])
    y_scr[:, 0:C] = ya.astype(BF16)
    uc_scr[0:CONV_HALO, :] = uc_scr[T:T + CONV_HALO, :]

    up_scr[POOL_HALO:POOL_HALO + T, :] = proj_scr[:, 4 * C:4 * C + POOL_WIDTH]
    t1 = (lax.broadcasted_iota(jnp.int32, (T, POOL_GROUP), 0) + (i * T + 1)).astype(F32)
    for g, w in enumerate(POOL_WINDOWS):
        lo, hi = g * POOL_GROUP, (g + 1) * POOL_GROUP
        s = up_scr[:, lo:hi]
        k = 1
        while k < w:
            s = s + pltpu.roll(s, k, 0)
            k *= 2
        u = up_scr[POOL_HALO:POOL_HALO + T, lo:hi]
        d = s[POOL_HALO:, :] / jnp.minimum(t1, float(w)) - u
        yb = jnp.dot(d.astype(BF16), pw_ref[g], preferred_element_type=F32)
        gate = proj_scr[:, 4 * C + POOL_WIDTH + lo:4 * C + POOL_WIDTH + hi]
        y_scr[:, C + lo:C + hi] = (yb * ps_ref[:, lo:hi] * jax.nn.silu(gate)).astype(BF16)
    up_scr[0:POOL_HALO, :] = up_scr[T:T + POOL_HALO, :]

    o_ref[...] = x_ref[...] + jnp.dot(y_scr[...], wout_ref[...], preferred_element_type=F32)


def _even_layer(x, norm_g, w_in, conv_w, pool_w, pool_scale, w_out):
    S = x.shape[0]
    T = min(TILE_EVEN, S)
    assert S % T == 0
    const = _const_spec
    return pl.pallas_call(
        _even_kernel,
        grid=(S // T,),
        in_specs=[
            pl.BlockSpec((T, D_MODEL), lambda i: (i, 0)),
            const(1, D_MODEL),
            const(D_MODEL, EVEN_IN),
            const(CONV_K, CONV_WIDTH),
            const(len(POOL_WINDOWS), POOL_GROUP, POOL_GROUP),
            const(1, POOL_WIDTH),
            const(EVEN_OUT, D_MODEL),
        ],
        out_specs=pl.BlockSpec((T, D_MODEL), lambda i: (i, 0)),
        out_shape=jax.ShapeDtypeStruct((S, D_MODEL), F32),
        scratch_shapes=[
            pltpu.VMEM((T, EVEN_IN), F32),
            pltpu.VMEM((T + CONV_HALO, CONV_WIDTH), F32),
            pltpu.VMEM((T + POOL_HALO, POOL_WIDTH), F32),
            pltpu.VMEM((T, EVEN_OUT), BF16),
        ],
        compiler_params=pltpu.CompilerParams(
            dimension_semantics=("arbitrary",), vmem_limit_bytes=VMEM_LIMIT),
        name="even_layer",
    )(x, norm_g.reshape(1, D_MODEL), w_in.astype(BF16), conv_w, pool_w.astype(BF16),
      pool_scale.reshape(1, POOL_WIDTH), w_out.astype(BF16))


def _lane_scan(x, op, identity):
    n = x.shape[1]
    lane = lax.broadcasted_iota(jnp.int32, x.shape, 1)
    k = 1
    while k < n:
        x = op(x, jnp.where(lane >= k, pltpu.roll(x, k, 1), identity))
        k *= 2
    return x


def _odd_init(kvbuf, abias, c_scr, m_scr):
    for b in range(2 * ATT_KV_HEADS):
        kvbuf[b, 0:BLOCK, :] = jnp.zeros((BLOCK, LANES), BF16)
    c_scr[...] = jnp.zeros(c_scr.shape, F32)
    m_scr[...] = jnp.zeros(m_scr.shape, F32)
    qi = lax.broadcasted_iota(jnp.int32, (BLOCK, 2 * BLOCK), 0)
    kj = lax.broadcasted_iota(jnp.int32, (BLOCK, 2 * BLOCK), 1)
    dist = qi + BLOCK - kj
    ok = (dist >= 0) & (dist < WINDOW)
    distf = dist.astype(F32)
    for hd in range(ATT_HEADS):
        slope = 2.0 ** (-8.0 * (hd + 1) / ATT_HEADS)
        abias[hd * BLOCK:(hd + 1) * BLOCK, :] = jnp.where(ok, (-slope * LOG2E) * distf, -jnp.inf)


class _InProj:
    def __init__(self, x_ref, g_ref, wa_ref, wb_ref, wg_ref, h_scr, proj):
        self.x_ref, self.g_ref, self.h_scr, self.proj = x_ref, g_ref, h_scr, proj
        self.todo = []
        rows = x_ref.shape[0]
        for w_ref, base in ((wa_ref, 0), (wb_ref, O_DO), (wg_ref, O_IF)):
            n = w_ref.shape[1]
            for c0 in range(0, n, MXU_WIDTH):
                for r0 in range(0, rows, INPROJ_ROWS):
                    self.todo.append((w_ref, base, c0, min(c0 + MXU_WIDTH, n), r0, min(r0 + INPROJ_ROWS, rows)))

    def normalise(self):
        self.h_scr[...] = _rmsnorm(self.x_ref[...], self.g_ref[...]).astype(BF16)

    def step(self):
        if self.todo:
            w_ref, base, c0, c1, r0, r1 = self.todo.pop(0)
            self.proj[r0:r1, base + c0:base + c1] = jnp.dot(
                self.h_scr[r0:r1, :], w_ref[:, c0:c1], preferred_element_type=F32)

    def finish(self):
        while self.todo:
            self.step()


def _odd_mixers(first, kv0, proj, x_ref, o_ref, sink_ref, bias_ref, wout_ref, fg_ref,
                kvbuf, abias, c_scr, m_scr, y_scr, nxt, after=None):
    T = x_ref.shape[0]
    NB = T // BLOCK
    H = MLSTM_HEADS
    L = min(MCHUNK, T)
    NEG = -jnp.inf

    lane_q = lax.broadcasted_iota(jnp.int32, (BLOCK, LANES), 1)
    low = lane_q < ATT_HEAD_DIM
    low_t = lax.broadcasted_iota(jnp.int32, (T, LANES), 1) < ATT_HEAD_DIM
    for b, off in enumerate((O_CK, O_CV)):
        kv = proj[:, off:off + ATT_KV_WIDTH]
        sw = pltpu.roll(kv, ATT_HEAD_DIM, 1)
        kvbuf[2 * b, kv0 + BLOCK:kv0 + BLOCK + T, :] = jnp.where(low_t, kv, sw).astype(BF16)
        kvbuf[2 * b + 1, kv0 + BLOCK:kv0 + BLOCK + T, :] = jnp.where(low_t, sw, kv).astype(BF16)
    if first is not False:
        key_lane = lax.broadcasted_iota(jnp.int32, (BLOCK, 2 * BLOCK), 1)
        no_prev = jnp.where(key_lane < BLOCK, jnp.where(first, NEG, 0.0), 0.0)
    scale = ATT_HEAD_DIM ** -0.5 * LOG2E
    ones_v = jnp.ones((2 * BLOCK, LANES), BF16)

    def scores(j, g):
        r0 = j * BLOCK
        kk = kvbuf[g, kv0 + r0:kv0 + r0 + 2 * BLOCK, :]
        qs = []
        for p in range(g * ATT_GROUP // 2, (g + 1) * ATT_GROUP // 2):
            qp = proj[r0:r0 + BLOCK, O_CQ + p * LANES:O_CQ + (p + 1) * LANES] * scale
            qs.append(jnp.where(low, qp, 0.0).astype(BF16))
            qs.append(jnp.where(low, 0.0, qp).astype(BF16))
        return lax.dot_general(jnp.concatenate(qs, axis=0), kk, (((1,), (1,)), ((), ())),
                               preferred_element_type=F32)

    def attention():
        slots = [(j, g) for j in range(NB) for g in range(ATT_KV_HEADS)]
        s_next = scores(*slots[0])
        for n, (j, g) in enumerate(slots):
            r0 = j * BLOCK
            s_all = s_next
            if n + 1 < len(slots):
                s_next = scores(*slots[n + 1])
            if n % 4 != 3:
                nxt.step()
            ps, sk = [], []
            for r in range(ATT_GROUP):
                hd = g * ATT_GROUP + r
                s = s_all[r * BLOCK:(r + 1) * BLOCK, :] + abias[hd * BLOCK:(hd + 1) * BLOCK, :]
                if j == 0 and first is not False:
                    s = s + no_prev
                sink = sink_ref[hd] * LOG2E
                m = jnp.maximum(jnp.max(s, axis=1, keepdims=True), sink)
                ps.append(jnp.exp2(s - m).astype(BF16))
                sk.append(jnp.exp2(sink - m))
            vv = jnp.concatenate([kvbuf[ATT_KV_HEADS + g, kv0 + r0:kv0 + r0 + 2 * BLOCK, :], ones_v], axis=1)
            o_all = jnp.dot(jnp.concatenate(ps, axis=0), vv, preferred_element_type=F32)
            for pp in range(ATT_GROUP // 2):
                p = g * ATT_GROUP // 2 + pp
                lo = o_all[2 * pp * BLOCK:(2 * pp + 1) * BLOCK, :]
                hi = o_all[(2 * pp + 1) * BLOCK:(2 * pp + 2) * BLOCK, :]
                att = jnp.where(low, lo[:, :LANES] / (lo[:, LANES:] + sk[2 * pp]),
                                hi[:, :LANES] / (hi[:, LANES:] + sk[2 * pp + 1]))
                gate = proj[r0:r0 + BLOCK, O_GC + p * LANES:O_GC + (p + 1) * LANES]
                y_scr[r0:r0 + BLOCK, p * LANES:(p + 1) * LANES] = (att * jax.nn.silu(gate)).astype(BF16)

    gt = proj[:, O_IF:O_IF + GATE_PAD].T[0:2 * H, :]
    pre = gt + jnp.concatenate([bias_ref[...]] * (T // LANES), axis=1)
    row = lax.broadcasted_iota(jnp.int32, (2 * H, T), 0)
    top = row < H
    bcum = _lane_scan(jax.nn.log_sigmoid(pre), jnp.add, 0.0)
    bsw = pltpu.roll(bcum, H, 0)
    a_top = pre - bsw
    ag = jnp.where(top, a_top, pltpu.roll(a_top, H, 0))
    bf = jnp.where(top, bsw, bcum)
    m_in = jnp.concatenate([m_scr[...]] * (T // LANES), axis=1)
    mg = jnp.maximum(m_in, _lane_scan(ag, jnp.maximum, NEG))
    m_t = bf + mg
    nrm = jnp.exp(-m_t)
    inter_parts, wk_parts, decay_parts = [], [], []
    for c in range(T // L):
        m_start = m_in[:, 0:1] if c == 0 else mg[:, c * L - 1:c * L]
        m_last = mg[:, (c + 1) * L - 1:(c + 1) * L]
        inter_parts.append(jnp.exp(m_start - mg[:, c * L:(c + 1) * L]))
        wk_parts.append(jnp.exp(ag[:, c * L:(c + 1) * L] - m_last))
        decay_parts.append(jnp.broadcast_to(jnp.exp(m_start - m_last), (2 * H, 2 * MLSTM_V_DIM)))
    inter = jnp.concatenate(inter_parts, axis=1)
    m_scr[...] = jnp.broadcast_to(m_t[:, T - 1:T], m_scr.shape)
    ag2 = ag * LOG2E
    cols = jnp.concatenate([mg * LOG2E, inter, nrm, jnp.zeros((LANES - 6 * H, T), F32)], axis=0).T

    kT = proj[:, O_DK:O_DK + MLSTM_QK_WIDTH].T
    lane_p = lax.broadcasted_iota(jnp.int32, (L, LANES), 1)
    low_p = lane_p < MLSTM_QK_DIM
    tri = (lax.broadcasted_iota(jnp.int32, (BLOCK, BLOCK), 1) <= lax.broadcasted_iota(jnp.int32, (BLOCK, BLOCK), 0))
    ones = jnp.ones((L, MLSTM_V_DIM), BF16)
    zero_blk = jnp.zeros((BLOCK, BLOCK), BF16)
    qscale = MLSTM_QK_DIM ** -0.5
    for _ in range(PRE_CHUNKS):
        nxt.step()
    for c in range(T // L):
        r0 = c * L
        for pair in range(H // 2):
            qp = proj[r0:r0 + L, O_DQ + pair * LANES:O_DQ + (pair + 1) * LANES] * qscale
            kT_pair = kT[pair * LANES:(pair + 1) * LANES, r0:r0 + L].astype(BF16)
            c_pair = c_scr[pair * LANES:(pair + 1) * LANES, :].astype(BF16)
            for half in range(2):
                hd = 2 * pair + half
                q_h = jnp.where(low_p, qp, 0.0) if half == 0 else jnp.where(low_p, 0.0, qp)
                s = jnp.dot(q_h.astype(BF16), kT_pair, preferred_element_type=F32)
                if H * c + hd < MLSTM_CHUNKS:
                    nxt.step()
                p_rows = []
                for bi in range(L // BLOCK):
                    m_col = cols[r0 + bi * BLOCK:r0 + (bi + 1) * BLOCK, hd:hd + 1]
                    blks = []
                    for bj in range(L // BLOCK):
                        if bj > bi:
                            blks.append(zero_blk)
                            continue
                        arg = ag2[hd:hd + 1, r0 + bj * BLOCK:r0 + (bj + 1) * BLOCK] - m_col
                        if bj == bi:
                            arg = jnp.where(tri, arg, NEG)
                        sb = s[bi * BLOCK:(bi + 1) * BLOCK, bj * BLOCK:(bj + 1) * BLOCK]
                        blks.append((sb * jnp.exp2(arg)).astype(BF16))
                    p_rows.append(jnp.concatenate(blks, axis=1))
                p_bf = jnp.concatenate(p_rows, axis=0)
                v_aug = jnp.concatenate(
                    [proj[r0:r0 + L, O_DV + hd * MLSTM_V_DIM:O_DV + (hd + 1) * MLSTM_V_DIM].astype(BF16), ones],
                    axis=1)
                q_int = (q_h * cols[r0:r0 + L, 2 * H + hd:2 * H + hd + 1]).astype(BF16)
                num = jnp.dot(jnp.concatenate([p_bf, q_int], axis=1),
                              jnp.concatenate([v_aug, c_pair], axis=0), preferred_element_type=F32)
                den = jnp.maximum(jnp.abs(num[:, MLSTM_V_DIM:]), cols[r0:r0 + L, 4 * H + hd:4 * H + hd + 1])
                hm = num[:, :MLSTM_V_DIM] / den
                og = proj[r0:r0 + L, O_DO + hd * MLSTM_V_DIM:O_DO + (hd + 1) * MLSTM_V_DIM]
                gd = proj[r0:r0 + L, O_GD + hd * MLSTM_V_DIM:O_GD + (hd + 1) * MLSTM_V_DIM]
                y_scr[r0:r0 + L, ATT_WIDTH + hd * MLSTM_V_DIM:ATT_WIDTH + (hd + 1) * MLSTM_V_DIM] = (
                    hm * jax.nn.sigmoid(og) * jax.nn.silu(gd)).astype(BF16)
                kw = (kT[hd * MLSTM_QK_DIM:(hd + 1) * MLSTM_QK_DIM, r0:r0 + L]
                      * wk_parts[c][hd:hd + 1, :]).astype(BF16)
                upd = jnp.dot(kw, v_aug, preferred_element_type=F32)
                rows = slice(hd * MLSTM_QK_DIM, (hd + 1) * MLSTM_QK_DIM)
                c_scr[rows, :] = decay_parts[c][hd:hd + 1, :] * c_scr[rows, :] + upd

    attention()
    nxt.finish()
    if after is not None:
        after.normalise()
    for r0 in range(0, T, OUT_ROWS):
        x2 = x_ref[r0:r0 + OUT_ROWS, :] + jnp.dot(y_scr[r0:r0 + OUT_ROWS, :], wout_ref[...],
                                                   preferred_element_type=F32)
        o_ref[r0:r0 + OUT_ROWS, :] = _rmsnorm(x2, fg_ref[...])


def _odd_kernel(sink_ref, xa_ref, xb_ref, xn_ref, g_ref, wa_ref, wb_ref, wg_ref, bias_ref, wout_ref,
                fg_ref, o_ref, proj_a, proj_b, h_a, h_b, kvbuf, abias, c_scr, m_scr, y_a, y_b):
    s = pl.program_id(0)
    T = xa_ref.shape[0]
    w = (g_ref, wa_ref, wb_ref, wg_ref)
    rest = (sink_ref, bias_ref, wout_ref, fg_ref, kvbuf, abias, c_scr, m_scr)

    @pl.when(s == 0)
    def _():
        _odd_init(kvbuf, abias, c_scr, m_scr)
        first = _InProj(xa_ref, *w, h_a, proj_a)
        first.normalise()
        first.finish()

    nxt = _InProj(xb_ref, *w, h_a, proj_b)
    nxt.normalise()
    after = _InProj(xn_ref, *w, h_b, proj_a)
    _odd_mixers(s == 0, 0, proj_a, xa_ref, o_ref.at[0:T], *rest, y_a, nxt, after)
    _odd_mixers(False, T, proj_b, xb_ref, o_ref.at[T:2 * T], *rest, y_b, after)
    for b in range(2 * ATT_KV_HEADS):
        kvbuf[b, 0:BLOCK, :] = kvbuf[b, 2 * T:2 * T + BLOCK, :]


def _odd_layer(x, norm_g, w_in, i_bias, f_bias, sinks, w_out, final_g):
    S = x.shape[0]
    T = min(TILE_ODD, S // 2)
    NT = S // T
    assert S % (2 * T) == 0 and T % BLOCK == 0 and T % min(MCHUNK, T) == 0
    o_if = O_DO
    o_do = o_if + 2 * MLSTM_HEADS
    wa = w_in[:, :o_if].astype(BF16)
    wb = w_in[:, o_do:].astype(BF16)
    wg = jnp.pad(w_in[:, o_if:o_do], ((0, 0), (0, GATE_PAD - 2 * MLSTM_HEADS))).astype(BF16)
    wo = w_out.astype(BF16)
    bias = jnp.broadcast_to(jnp.concatenate([i_bias, f_bias]).astype(F32)[:, None], (2 * MLSTM_HEADS, LANES))
    const = _const_spec
    return pl.pallas_call(
        _odd_kernel,
        grid=(NT // 2,),
        in_specs=[
            pl.BlockSpec(memory_space=pltpu.SMEM),
            pl.BlockSpec((T, D_MODEL), lambda s: (2 * s, 0)),
            pl.BlockSpec((T, D_MODEL), lambda s: (2 * s + 1, 0)),
            pl.BlockSpec((T, D_MODEL), lambda s: (jnp.minimum(2 * s + 2, NT - 1), 0)),
            const(1, D_MODEL),
            const(D_MODEL, O_DO),
            const(D_MODEL, O_IF - O_DO),
            const(D_MODEL, GATE_PAD),
            const(2 * MLSTM_HEADS, LANES),
            const(ODD_OUT, D_MODEL),
            const(1, D_MODEL),
        ],
        out_specs=pl.BlockSpec((2 * T, D_MODEL), lambda s: (s, 0)),
        out_shape=jax.ShapeDtypeStruct((S, D_MODEL), F32),
        scratch_shapes=[
            pltpu.VMEM((T, ODD_IN_PAD), F32),
            pltpu.VMEM((T, ODD_IN_PAD), F32),
            pltpu.VMEM((T, D_MODEL), BF16),
            pltpu.VMEM((T, D_MODEL), BF16),
            pltpu.VMEM((2 * ATT_KV_HEADS, 2 * T + BLOCK, LANES), BF16),
            pltpu.VMEM((ATT_HEADS * BLOCK, 2 * BLOCK), F32),
            pltpu.VMEM((MLSTM_HEADS * MLSTM_QK_DIM, 2 * MLSTM_V_DIM), F32),
            pltpu.VMEM((2 * MLSTM_HEADS, LANES), F32),
            pltpu.VMEM((T, ODD_OUT), BF16),
            pltpu.VMEM((T, ODD_OUT), BF16),
        ],
        compiler_params=pltpu.CompilerParams(
            dimension_semantics=("arbitrary",), vmem_limit_bytes=VMEM_LIMIT),
        name="odd_layer",
    )(sinks.astype(F32), x, x, x, norm_g.reshape(1, D_MODEL), wa, wb, wg, bias, wo,
      final_g.reshape(1, D_MODEL))


def kernel(x, even_norm, even_w_in, even_conv_w, even_pool_w, even_pool_scale, even_w_out,
           odd_norm, odd_w_in, odd_i_bias, odd_f_bias, odd_sinks, odd_w_out, final_norm):
    B, S, D = x.shape
    assert B == 1 and D == D_MODEL
    assert even_norm.shape[0] == 1 and odd_norm.shape[0] == 1
    h = x.reshape(S, D)
    h = _even_layer(h, even_norm[0], even_w_in[0], even_conv_w[0], even_pool_w[0],
                    even_pool_scale[0], even_w_out[0])
    h = _odd_layer(h, odd_norm[0], odd_w_in[0], odd_i_bias[0], odd_f_bias[0], odd_sinks[0],
                   odd_w_out[0], final_norm)
    return h.reshape(B, S, D)
```

```python
import jax
import jax.numpy as jnp
from jax import lax
from jax.experimental import pallas as pl
from jax.experimental.pallas import tpu as pltpu

F32 = jnp.float32
BF16 = jnp.bfloat16

D_MODEL = 1024
RMS_EPS = 1e-6

CONV_WIDTH = 512
CONV_K = 3
POOL_WINDOWS = (2, 4, 8, 16)
POOL_WIDTH = 512
POOL_GROUP = POOL_WIDTH // len(POOL_WINDOWS)
EVEN_IN = 4 * CONV_WIDTH + 2 * POOL_WIDTH
EVEN_OUT = CONV_WIDTH + POOL_WIDTH
CONV_HALO = 8
POOL_HALO = 16

ATT_HEADS = 8
ATT_KV_HEADS = 2
ATT_GROUP = ATT_HEADS // ATT_KV_HEADS
ATT_HEAD_DIM = 64
ATT_WIDTH = ATT_HEADS * ATT_HEAD_DIM
ATT_KV_WIDTH = ATT_KV_HEADS * ATT_HEAD_DIM
WINDOW = 128
BLOCK = 128
MLSTM_HEADS = 4
MLSTM_QK_DIM = 64
MLSTM_V_DIM = 128
MLSTM_QK_WIDTH = MLSTM_HEADS * MLSTM_QK_DIM
MLSTM_WIDTH = MLSTM_HEADS * MLSTM_V_DIM
ODD_OUT = ATT_WIDTH + MLSTM_WIDTH
LANES = 128
MXU_WIDTH = 256
GATE_PAD = LANES

O_CQ = 0
O_CK = O_CQ + ATT_WIDTH
O_CV = O_CK + ATT_KV_WIDTH
O_GC = O_CV + ATT_KV_WIDTH
O_DQ = O_GC + ATT_WIDTH
O_DK = O_DQ + MLSTM_QK_WIDTH
O_DV = O_DK + MLSTM_QK_WIDTH
O_DO = O_DV + MLSTM_WIDTH
O_GD = O_DO + MLSTM_WIDTH
O_IF = O_GD + MLSTM_WIDTH
ODD_IN_PAD = O_IF + GATE_PAD

TILE_EVEN = 1024
TILE_ODD = 512
MCHUNK = 256
INPROJ_ROWS = 512
OUT_ROWS = 256
PRE_CHUNKS = 3
MLSTM_CHUNKS = 5
LOG2E = 1.4426950408889634
VMEM_LIMIT = 60 * 1024 * 1024


def _const_spec(*shape):
    return pl.BlockSpec(shape, lambda i: (0,) * len(shape), pipeline_mode=pl.Buffered(1))


def _rmsnorm(x, g):
    ms = jnp.mean(x * x, axis=-1, keepdims=True)
    return x * lax.rsqrt(ms + RMS_EPS) * g


def _even_kernel(x_ref, g_ref, win_ref, cw_ref, pw_ref, ps_ref, wout_ref,
                 nwa32_ref, nwb32_ref, nwg32_ref, nwo32_ref,
                 o_ref, nwa_ref, nwb_ref, nwg_ref, nwo_ref, proj_scr, uc_scr, up_scr, y_scr):
    i = pl.program_id(0)
    T = x_ref.shape[0]
    C = CONV_WIDTH

    nwa_ref[...] = nwa32_ref[...].astype(BF16)
    nwb_ref[...] = nwb32_ref[...].astype(BF16)
    nwo_ref[...] = nwo32_ref[...].astype(BF16)
    nwg_ref[...] = jnp.concatenate(
        [nwg32_ref[...], jnp.zeros((GATE_PAD - 2 * MLSTM_HEADS, D_MODEL), F32)], axis=0).astype(BF16)

    @pl.when(i == 0)
    def _():
        uc_scr[0:CONV_HALO, :] = jnp.zeros((CONV_HALO, C), F32)
        up_scr[0:POOL_HALO, :] = jnp.zeros((POOL_HALO, POOL_WIDTH), F32)

    h = _rmsnorm(x_ref[...], g_ref[...]).astype(BF16)
    proj_scr[...] = jnp.dot(h, win_ref[...], preferred_element_type=F32)

    uc = proj_scr[:, 2 * C:3 * C] * proj_scr[:, 0:C]
    uc_scr[CONV_HALO:CONV_HALO + T, :] = uc
    conv = (cw_ref[2:3, :] * uc
            + cw_ref[1:2, :] * uc_scr[CONV_HALO - 1:CONV_HALO - 1 + T, :]
            + cw_ref[0:1, :] * uc_scr[CONV_HALO - 2:CONV_HALO - 2 + T, :])
    ya = proj_scr[:, C:2 * C] * conv * jax.nn.silu(proj_scr[:, 3 * C:4 * C])
    y_scr[:, 0:C] = ya.astype(BF16)
    uc_scr[0:CONV_HALO, :] = uc_scr[T:T + CONV_HALO, :]

    up_scr[POOL_HALO:POOL_HALO + T, :] = proj_scr[:, 4 * C:4 * C + POOL_WIDTH]
    t1 = (lax.broadcasted_iota(jnp.int32, (T, POOL_GROUP), 0) + (i * T + 1)).astype(F32)
    for g, w in enumerate(POOL_WINDOWS):
        lo, hi = g * POOL_GROUP, (g + 1) * POOL_GROUP
        s = up_scr[:, lo:hi]
        k = 1
        while k < w:
            s = s + pltpu.roll(s, k, 0)
            k *= 2
        u = up_scr[POOL_HALO:POOL_HALO + T, lo:hi]
        d = s[POOL_HALO:, :] / jnp.minimum(t1, float(w)) - u
        yb = jnp.dot(d.astype(BF16), pw_ref[g], preferred_element_type=F32)
        gate = proj_scr[:, 4 * C + POOL_WIDTH + lo:4 * C + POOL_WIDTH + hi]
        y_scr[:, C + lo:C + hi] = (yb * ps_ref[:, lo:hi] * jax.nn.silu(gate)).astype(BF16)
    up_scr[0:POOL_HALO, :] = up_scr[T:T + POOL_HALO, :]

    o_ref[...] = x_ref[...] + jnp.dot(y_scr[...], wout_ref[...], preferred_element_type=F32)


def _even_layer(x, norm_g, w_in, conv_w, pool_w, pool_scale, w_out, next_w_in_t, next_w_out):
    S = x.shape[0]
    T = min(TILE_EVEN, S)
    steps = S // T
    n_gates = 2 * MLSTM_HEADS
    assert S % T == 0
    assert next_w_in_t.shape == (O_IF + n_gates, D_MODEL) and next_w_out.shape == (ODD_OUT, D_MODEL)
    ra, rb, ro = O_DO // steps, (O_IF - O_DO) // steps, ODD_OUT // steps
    assert ra * steps == O_DO and rb * steps == O_IF - O_DO and ro * steps == ODD_OUT
    const = _const_spec
    slab = lambda r: pl.BlockSpec((r, D_MODEL), lambda i: (i, 0))
    return pl.pallas_call(
        _even_kernel,
        grid=(steps,),
        in_specs=[
            pl.BlockSpec((T, D_MODEL), lambda i: (i, 0)),
            const(1, D_MODEL),
            const(D_MODEL, EVEN_IN),
            const(CONV_K, CONV_WIDTH),
            const(len(POOL_WINDOWS), POOL_GROUP, POOL_GROUP),
            const(1, POOL_WIDTH),
            const(EVEN_OUT, D_MODEL),
            slab(ra),
            pl.BlockSpec((pl.Element(rb), pl.Element(D_MODEL)), lambda i: (pl.multiple_of(O_DO + n_gates + i * rb, n_gates), 0)),
            pl.BlockSpec((n_gates, D_MODEL), lambda i: (O_DO // n_gates, 0)),
            slab(ro),
        ],
        out_specs=[pl.BlockSpec((T, D_MODEL), lambda i: (i, 0)),
                   slab(ra), slab(rb), pl.BlockSpec((GATE_PAD, D_MODEL), lambda i: (0, 0)), slab(ro)],
        out_shape=[jax.ShapeDtypeStruct((S, D_MODEL), F32),
                   jax.ShapeDtypeStruct((O_DO, D_MODEL), BF16),
                   jax.ShapeDtypeStruct((O_IF - O_DO, D_MODEL), BF16),
                   jax.ShapeDtypeStruct((GATE_PAD, D_MODEL), BF16),
                   jax.ShapeDtypeStruct((ODD_OUT, D_MODEL), BF16)],
        scratch_shapes=[
            pltpu.VMEM((T, EVEN_IN), F32),
            pltpu.VMEM((T + CONV_HALO, CONV_WIDTH), F32),
            pltpu.VMEM((T + POOL_HALO, POOL_WIDTH), F32),
            pltpu.VMEM((T, EVEN_OUT), BF16),
        ],
        compiler_params=pltpu.CompilerParams(
            dimension_semantics=("arbitrary",), vmem_limit_bytes=VMEM_LIMIT),
        name="even_layer",
    )(x, norm_g.reshape(1, D_MODEL), w_in.astype(BF16), conv_w, pool_w.astype(BF16),
      pool_scale.reshape(1, POOL_WIDTH), w_out.astype(BF16),
      next_w_in_t, next_w_in_t, next_w_in_t, next_w_out)


def _lane_scan(x, op, identity):
    n = x.shape[1]
    lane = lax.broadcasted_iota(jnp.int32, x.shape, 1)
    k = 1
    while k < n:
        x = op(x, jnp.where(lane >= k, pltpu.roll(x, k, 1), identity))
        k *= 2
    return x


def _odd_init(kvbuf, abias, c_scr, m_scr):
    for b in range(2 * ATT_KV_HEADS):
        kvbuf[b, 0:BLOCK, :] = jnp.zeros((BLOCK, LANES), BF16)
    c_scr[...] = jnp.zeros(c_scr.shape, F32)
    m_scr[...] = jnp.zeros(m_scr.shape, F32)
    qi = lax.broadcasted_iota(jnp.int32, (BLOCK, 2 * BLOCK), 0)
    kj = lax.broadcasted_iota(jnp.int32, (BLOCK, 2 * BLOCK), 1)
    dist = qi + BLOCK - kj
    ok = (dist >= 0) & (dist < WINDOW)
    distf = dist.astype(F32)
    for hd in range(ATT_HEADS):
        slope = 2.0 ** (-8.0 * (hd + 1) / ATT_HEADS)
        abias[hd * BLOCK:(hd + 1) * BLOCK, :] = jnp.where(ok, (-slope * LOG2E) * distf, -jnp.inf)


class _InProj:
    def __init__(self, x_ref, g_ref, wa_ref, wb_ref, wg_ref, h_scr, proj):
        self.x_ref, self.g_ref, self.h_scr, self.proj = x_ref, g_ref, h_scr, proj
        self.todo = []
        rows = x_ref.shape[0]
        for w_ref, base in ((wa_ref, 0), (wb_ref, O_DO), (wg_ref, O_IF)):
            n = w_ref.shape[0]
            for c0 in range(0, n, MXU_WIDTH):
                for r0 in range(0, rows, INPROJ_ROWS):
                    self.todo.append((w_ref, base, c0, min(c0 + MXU_WIDTH, n), r0, min(r0 + INPROJ_ROWS, rows)))

    def normalise(self):
        self.h_scr[...] = _rmsnorm(self.x_ref[...], self.g_ref[...]).astype(BF16)

    def step(self):
        if self.todo:
            w_ref, base, c0, c1, r0, r1 = self.todo.pop(0)
            self.proj[r0:r1, base + c0:base + c1] = lax.dot_general(
                self.h_scr[r0:r1, :], w_ref[c0:c1, :], (((1,), (1,)), ((), ())), preferred_element_type=F32)

    def finish(self):
        while self.todo:
            self.step()


def _odd_mixers(first, kv0, proj, x_ref, o_ref, sink_ref, bias_ref, wout_ref, fg_ref,
                kvbuf, abias, c_scr, m_scr, y_scr, nxt, after=None):
    T = x_ref.shape[0]
    NB = T // BLOCK
    H = MLSTM_HEADS
    L = min(MCHUNK, T)
    NEG = -jnp.inf

    lane_q = lax.broadcasted_iota(jnp.int32, (BLOCK, LANES), 1)
    low = lane_q < ATT_HEAD_DIM
    low_t = lax.broadcasted_iota(jnp.int32, (T, LANES), 1) < ATT_HEAD_DIM
    for b, off in enumerate((O_CK, O_CV)):
        kv = proj[:, off:off + ATT_KV_WIDTH]
        sw = pltpu.roll(kv, ATT_HEAD_DIM, 1)
        kvbuf[2 * b, kv0 + BLOCK:kv0 + BLOCK + T, :] = jnp.where(low_t, kv, sw).astype(BF16)
        kvbuf[2 * b + 1, kv0 + BLOCK:kv0 + BLOCK + T, :] = jnp.where(low_t, sw, kv).astype(BF16)
    if first is not False:
        key_lane = lax.broadcasted_iota(jnp.int32, (BLOCK, 2 * BLOCK), 1)
        no_prev = jnp.where(key_lane < BLOCK, jnp.where(first, NEG, 0.0), 0.0)
    scale = ATT_HEAD_DIM ** -0.5 * LOG2E
    ones_v = jnp.ones((2 * BLOCK, LANES), BF16)

    def scores(j, g):
        r0 = j * BLOCK
        kk = kvbuf[g, kv0 + r0:kv0 + r0 + 2 * BLOCK, :]
        qs = []
        for p in range(g * ATT_GROUP // 2, (g + 1) * ATT_GROUP // 2):
            qp = proj[r0:r0 + BLOCK, O_CQ + p * LANES:O_CQ + (p + 1) * LANES] * scale
            qs.append(jnp.where(low, qp, 0.0).astype(BF16))
            qs.append(jnp.where(low, 0.0, qp).astype(BF16))
        return lax.dot_general(jnp.concatenate(qs, axis=0), kk, (((1,), (1,)), ((), ())),
                               preferred_element_type=F32)

    def attention():
        slots = [(j, g) for j in range(NB) for g in range(ATT_KV_HEADS)]
        s_next = scores(*slots[0])
        for n, (j, g) in enumerate(slots):
            r0 = j * BLOCK
            s_all = s_next
            if n + 1 < len(slots):
                s_next = scores(*slots[n + 1])
            if n % 4 != 3:
                nxt.step()
            ps, sk = [], []
            for r in range(ATT_GROUP):
                hd = g * ATT_GROUP + r
                s = s_all[r * BLOCK:(r + 1) * BLOCK, :] + abias[hd * BLOCK:(hd + 1) * BLOCK, :]
                if j == 0 and first is not False:
                    s = s + no_prev
                sink = sink_ref[hd] * LOG2E
                m = jnp.maximum(jnp.max(s, axis=1, keepdims=True), sink)
                ps.append(jnp.exp2(s - m).astype(BF16))
                sk.append(jnp.exp2(sink - m))
            vv = jnp.concatenate([kvbuf[ATT_KV_HEADS + g, kv0 + r0:kv0 + r0 + 2 * BLOCK, :], ones_v], axis=1)
            o_all = jnp.dot(jnp.concatenate(ps, axis=0), vv, preferred_element_type=F32)
            for pp in range(ATT_GROUP // 2):
                p = g * ATT_GROUP // 2 + pp
                lo = o_all[2 * pp * BLOCK:(2 * pp + 1) * BLOCK, :]
                hi = o_all[(2 * pp + 1) * BLOCK:(2 * pp + 2) * BLOCK, :]
                att = jnp.where(low, lo[:, :LANES] / (lo[:, LANES:] + sk[2 * pp]),
                                hi[:, :LANES] / (hi[:, LANES:] + sk[2 * pp + 1]))
                gate = proj[r0:r0 + BLOCK, O_GC + p * LANES:O_GC + (p + 1) * LANES]
                y_scr[r0:r0 + BLOCK, p * LANES:(p + 1) * LANES] = (att * jax.nn.silu(gate)).astype(BF16)

    gt = proj[:, O_IF:O_IF + GATE_PAD].T[0:2 * H, :]
    pre = gt + jnp.concatenate([bias_ref[...]] * (T // LANES), axis=1)
    row = lax.broadcasted_iota(jnp.int32, (2 * H, T), 0)
    top = row < H
    bcum = _lane_scan(jax.nn.log_sigmoid(pre), jnp.add, 0.0)
    bsw = pltpu.roll(bcum, H, 0)
    a_top = pre - bsw
    ag = jnp.where(top, a_top, pltpu.roll(a_top, H, 0))
    bf = jnp.where(top, bsw, bcum)
    m_in = jnp.concatenate([m_scr[...]] * (T // LANES), axis=1)
    mg = jnp.maximum(m_in, _lane_scan(ag, jnp.maximum, NEG))
    m_t = bf + mg
    nrm = jnp.exp(-m_t)
    inter_parts, wk_parts, decay_parts = [], [], []
    for c in range(T // L):
        m_start = m_in[:, 0:1] if c == 0 else mg[:, c * L - 1:c * L]
        m_last = mg[:, (c + 1) * L - 1:(c + 1) * L]
        inter_parts.append(jnp.exp(m_start - mg[:, c * L:(c + 1) * L]))
        wk_parts.append(jnp.exp(ag[:, c * L:(c + 1) * L] - m_last))
        decay_parts.append(jnp.broadcast_to(jnp.exp(m_start - m_last), (2 * H, 2 * MLSTM_V_DIM)))
    inter = jnp.concatenate(inter_parts, axis=1)
    m_scr[...] = jnp.broadcast_to(m_t[:, T - 1:T], m_scr.shape)
    ag2 = ag * LOG2E
    cols = jnp.concatenate([mg * LOG2E, inter, nrm, jnp.zeros((LANES - 6 * H, T), F32)], axis=0).T

    kT = proj[:, O_DK:O_DK + MLSTM_QK_WIDTH].T
    lane_p = lax.broadcasted_iota(jnp.int32, (L, LANES), 1)
    low_p = lane_p < MLSTM_QK_DIM
    tri = (lax.broadcasted_iota(jnp.int32, (BLOCK, BLOCK), 1) <= lax.broadcasted_iota(jnp.int32, (BLOCK, BLOCK), 0))
    ones = jnp.ones((L, MLSTM_V_DIM), BF16)
    zero_blk = jnp.zeros((BLOCK, BLOCK), BF16)
    qscale = MLSTM_QK_DIM ** -0.5
    for _ in range(PRE_CHUNKS):
        nxt.step()
    for c in range(T // L):
        r0 = c * L
        for pair in range(H // 2):
            qp = proj[r0:r0 + L, O_DQ + pair * LANES:O_DQ + (pair + 1) * LANES] * qscale
            kT_pair = kT[pair * LANES:(pair + 1) * LANES, r0:r0 + L].astype(BF16)
            c_pair = c_scr[pair * LANES:(pair + 1) * LANES, :].astype(BF16)
            for half in range(2):
                hd = 2 * pair + half
                q_h = jnp.where(low_p, qp, 0.0) if half == 0 else jnp.where(low_p, 0.0, qp)
                s = jnp.dot(q_h.astype(BF16), kT_pair, preferred_element_type=F32)
                if H * c + hd < MLSTM_CHUNKS:
                    nxt.step()
                p_rows = []
                for bi in range(L // BLOCK):
                    m_col = cols[r0 + bi * BLOCK:r0 + (bi + 1) * BLOCK, hd:hd + 1]
                    blks = []
                    for bj in range(L // BLOCK):
                        if bj > bi:
                            blks.append(zero_blk)
                            continue
                        arg = ag2[hd:hd + 1, r0 + bj * BLOCK:r0 + (bj + 1) * BLOCK] - m_col
                        if bj == bi:
                            arg = jnp.where(tri, arg, NEG)
                        sb = s[bi * BLOCK:(bi + 1) * BLOCK, bj * BLOCK:(bj + 1) * BLOCK]
                        blks.append((sb * jnp.exp2(arg)).astype(BF16))
                    p_rows.append(jnp.concatenate(blks, axis=1))
                p_bf = jnp.concatenate(p_rows, axis=0)
                v_aug = jnp.concatenate(
                    [proj[r0:r0 + L, O_DV + hd * MLSTM_V_DIM:O_DV + (hd + 1) * MLSTM_V_DIM].astype(BF16), ones],
                    axis=1)
                q_int = (q_h * cols[r0:r0 + L, 2 * H + hd:2 * H + hd + 1]).astype(BF16)
                num = jnp.dot(jnp.concatenate([p_bf, q_int], axis=1),
                              jnp.concatenate([v_aug, c_pair], axis=0), preferred_element_type=F32)
                den = jnp.maximum(jnp.abs(num[:, MLSTM_V_DIM:]), cols[r0:r0 + L, 4 * H + hd:4 * H + hd + 1])
                hm = num[:, :MLSTM_V_DIM] / den
                og = proj[r0:r0 + L, O_DO + hd * MLSTM_V_DIM:O_DO + (hd + 1) * MLSTM_V_DIM]
                gd = proj[r0:r0 + L, O_GD + hd * MLSTM_V_DIM:O_GD + (hd + 1) * MLSTM_V_DIM]
                y_scr[r0:r0 + L, ATT_WIDTH + hd * MLSTM_V_DIM:ATT_WIDTH + (hd + 1) * MLSTM_V_DIM] = (
                    hm * jax.nn.sigmoid(og) * jax.nn.silu(gd)).astype(BF16)
                kw = (kT[hd * MLSTM_QK_DIM:(hd + 1) * MLSTM_QK_DIM, r0:r0 + L]
                      * wk_parts[c][hd:hd + 1, :]).astype(BF16)
                upd = jnp.dot(kw, v_aug, preferred_element_type=F32)
                rows = slice(hd * MLSTM_QK_DIM, (hd + 1) * MLSTM_QK_DIM)
                c_scr[rows, :] = decay_parts[c][hd:hd + 1, :] * c_scr[rows, :] + upd

    attention()
    nxt.finish()
    if after is not None:
        after.normalise()
    for r0 in range(0, T, OUT_ROWS):
        x2 = x_ref[r0:r0 + OUT_ROWS, :] + jnp.dot(y_scr[r0:r0 + OUT_ROWS, :], wout_ref[...],
                                                   preferred_element_type=F32)
        o_ref[r0:r0 + OUT_ROWS, :] = _rmsnorm(x2, fg_ref[...])


def _odd_kernel(sink_ref, xa_ref, xb_ref, xn_ref, g_ref, wa_ref, wb_ref, wg_ref, bias_ref, wout_ref,
                fg_ref, o_ref, proj_a, proj_b, h_a, h_b, kvbuf, abias, c_scr, m_scr, y_a, y_b):
    s = pl.program_id(0)
    T = xa_ref.shape[0]
    w = (g_ref, wa_ref, wb_ref, wg_ref)
    rest = (sink_ref, bias_ref, wout_ref, fg_ref, kvbuf, abias, c_scr, m_scr)

    @pl.when(s == 0)
    def _():
        _odd_init(kvbuf, abias, c_scr, m_scr)
        first = _InProj(xa_ref, *w, h_a, proj_a)
        first.normalise()
        first.finish()

    nxt = _InProj(xb_ref, *w, h_a, proj_b)
    nxt.normalise()
    after = _InProj(xn_ref, *w, h_b, proj_a)
    _odd_mixers(s == 0, 0, proj_a, xa_ref, o_ref.at[0:T], *rest, y_a, nxt, after)
    _odd_mixers(False, T, proj_b, xb_ref, o_ref.at[T:2 * T], *rest, y_b, after)
    for b in range(2 * ATT_KV_HEADS):
        kvbuf[b, 0:BLOCK, :] = kvbuf[b, 2 * T:2 * T + BLOCK, :]


def _odd_layer(x, norm_g, wa, wb, wg, i_bias, f_bias, sinks, wo, final_g):
    S = x.shape[0]
    T = min(TILE_ODD, S // 2)
    NT = S // T
    assert S % (2 * T) == 0 and T % BLOCK == 0 and T % min(MCHUNK, T) == 0
    bias = jnp.broadcast_to(jnp.concatenate([i_bias, f_bias]).astype(F32)[:, None], (2 * MLSTM_HEADS, LANES))
    const = _const_spec
    return pl.pallas_call(
        _odd_kernel,
        grid=(NT // 2,),
        in_specs=[
            pl.BlockSpec(memory_space=pltpu.SMEM),
            pl.BlockSpec((T, D_MODEL), lambda s: (2 * s, 0)),
            pl.BlockSpec((T, D_MODEL), lambda s: (2 * s + 1, 0)),
            pl.BlockSpec((T, D_MODEL), lambda s: (jnp.minimum(2 * s + 2, NT - 1), 0)),
            const(1, D_MODEL),
            const(O_DO, D_MODEL),
            const(O_IF - O_DO, D_MODEL),
            const(GATE_PAD, D_MODEL),
            const(2 * MLSTM_HEADS, LANES),
            const(ODD_OUT, D_MODEL),
            const(1, D_MODEL),
        ],
        out_specs=pl.BlockSpec((2 * T, D_MODEL), lambda s: (s, 0)),
        out_shape=jax.ShapeDtypeStruct((S, D_MODEL), F32),
        scratch_shapes=[
            pltpu.VMEM((T, ODD_IN_PAD), F32),
            pltpu.VMEM((T, ODD_IN_PAD), F32),
            pltpu.VMEM((T, D_MODEL), BF16),
            pltpu.VMEM((T, D_MODEL), BF16),
            pltpu.VMEM((2 * ATT_KV_HEADS, 2 * T + BLOCK, LANES), BF16),
            pltpu.VMEM((ATT_HEADS * BLOCK, 2 * BLOCK), F32),
            pltpu.VMEM((MLSTM_HEADS * MLSTM_QK_DIM, 2 * MLSTM_V_DIM), F32),
            pltpu.VMEM((2 * MLSTM_HEADS, LANES), F32),
            pltpu.VMEM((T, ODD_OUT), BF16),
            pltpu.VMEM((T, ODD_OUT), BF16),
        ],
        compiler_params=pltpu.CompilerParams(
            dimension_semantics=("arbitrary",), vmem_limit_bytes=VMEM_LIMIT),
        name="odd_layer",
    )(sinks.astype(F32), x, x, x, norm_g.reshape(1, D_MODEL), wa, wb, wg, bias, wo,
      final_g.reshape(1, D_MODEL))


def kernel(x, even_norm, even_w_in, even_conv_w, even_pool_w, even_pool_scale, even_w_out,
           odd_norm, odd_w_in, odd_i_bias, odd_f_bias, odd_sinks, odd_w_out, final_norm):
    B, S, D = x.shape
    assert B == 1 and D == D_MODEL
    assert even_norm.shape[0] == 1 and odd_norm.shape[0] == 1
    h = x.reshape(S, D)
    h, wa, wb, wg, wo = _even_layer(h, even_norm[0], even_w_in[0], even_conv_w[0], even_pool_w[0],
                                    even_pool_scale[0], even_w_out[0], odd_w_in[0].T, odd_w_out[0])
    h = _odd_layer(h, odd_norm[0], wa, wb, wg, odd_i_bias[0], odd_f_bias[0], odd_sinks[0], wo, final_norm)
    return h.reshape(B, S, D)
```

```python
import jax
import jax.numpy as jnp
from jax import lax
from jax.experimental import pallas as pl
from jax.experimental.pallas import tpu as pltpu

F32 = jnp.float32
BF16 = jnp.bfloat16

D_MODEL = 1024
RMS_EPS = 1e-6

CONV_WIDTH = 512
CONV_K = 3
POOL_WINDOWS = (2, 4, 8, 16)
POOL_WIDTH = 512
POOL_GROUP = POOL_WIDTH // len(POOL_WINDOWS)
EVEN_IN = 4 * CONV_WIDTH + 2 * POOL_WIDTH
EVEN_OUT = CONV_WIDTH + POOL_WIDTH
CONV_HALO = 8
POOL_HALO = 16

ATT_HEADS = 8
ATT_KV_HEADS = 2
ATT_GROUP = ATT_HEADS // ATT_KV_HEADS
ATT_HEAD_DIM = 64
ATT_WIDTH = ATT_HEADS * ATT_HEAD_DIM
ATT_KV_WIDTH = ATT_KV_HEADS * ATT_HEAD_DIM
WINDOW = 128
BLOCK = 128
MLSTM_HEADS = 4
MLSTM_QK_DIM = 64
MLSTM_V_DIM = 128
MLSTM_QK_WIDTH = MLSTM_HEADS * MLSTM_QK_DIM
MLSTM_WIDTH = MLSTM_HEADS * MLSTM_V_DIM
ODD_OUT = ATT_WIDTH + MLSTM_WIDTH
LANES = 128
BF16_ROWS = 16
SINK_KEY = 0
MXU_WIDTH = 256
GATE_PAD = LANES

O_CQ = 0
O_CK = O_CQ + ATT_WIDTH
O_CV = O_CK + ATT_KV_WIDTH
O_GC = O_CV + ATT_KV_WIDTH
O_DQ = O_GC + ATT_WIDTH
O_DK = O_DQ + MLSTM_QK_WIDTH
O_DV = O_DK + MLSTM_QK_WIDTH
O_DO = O_DV + MLSTM_WIDTH
O_GD = O_DO + MLSTM_WIDTH
O_IF = O_GD + MLSTM_WIDTH
ODD_IN_PAD = O_IF + GATE_PAD

TILE_EVEN = 1024
EVEN_CHUNK = 512
TILE_ODD = 512
MCHUNK = 256
INPROJ_ROWS = 512
OUT_ROWS = 256
PRE_CHUNKS = 3
MLSTM_CHUNKS = 5
ATTN_SKIP_EVERY = 4
LOG2E = 1.4426950408889634
VMEM_LIMIT = 60 * 1024 * 1024


def _const_spec(*shape):
    return pl.BlockSpec(shape, lambda i: (0,) * len(shape), pipeline_mode=pl.Buffered(1))


def _rmsnorm(x, g):
    ms = jnp.mean(x * x, axis=-1, keepdims=True)
    return x * lax.rsqrt(ms + RMS_EPS) * g


def _even_kernel(x_ref, g_ref, win_ref, cw_ref, pw_ref, ps_ref, wout_ref,
                 nwa32_ref, nwb32_ref, nwg32_ref, nwo32_ref,
                 o_ref, nwa_ref, nwb_ref, nwg_ref, nwo_ref, proj_scr, uc_scr, up_scr, y_scr):
    i = pl.program_id(0)
    T = x_ref.shape[0]
    C = CONV_WIDTH

    nwa_ref[...] = nwa32_ref[...].astype(BF16)
    nwb_ref[...] = nwb32_ref[...].astype(BF16)
    nwo_ref[...] = nwo32_ref[...].astype(BF16)
    nwg_ref[...] = jnp.concatenate(
        [nwg32_ref[...], jnp.zeros((GATE_PAD - 2 * MLSTM_HEADS, D_MODEL), F32)], axis=0).astype(BF16)

    @pl.when(i == 0)
    def _():
        uc_scr[0:CONV_HALO, :] = jnp.zeros((CONV_HALO, C), F32)
        up_scr[0:POOL_HALO, :] = jnp.zeros((POOL_HALO, POOL_WIDTH), F32)

    h = _rmsnorm(x_ref[...], g_ref[...]).astype(BF16)
    for c0 in range(0, EVEN_IN, EVEN_CHUNK):
        proj_scr[:, c0:c0 + EVEN_CHUNK] = jnp.dot(h, win_ref[:, c0:c0 + EVEN_CHUNK].astype(BF16),
                                                  preferred_element_type=F32)

    uc = proj_scr[:, 2 * C:3 * C] * proj_scr[:, 0:C]
    uc_scr[CONV_HALO:CONV_HALO + T, :] = uc
    conv = (cw_ref[2:3, :] * uc
            + cw_ref[1:2, :] * uc_scr[CONV_HALO - 1:CONV_HALO - 1 + T, :]
            + cw_ref[0:1, :] * uc_scr[CONV_HALO - 2:CONV_HALO - 2 + T, :])
    ya = proj_scr[:, C:2 * C] * conv * jax.nn.silu(proj_scr[:, 3 * C:4 * C])
    y_scr[:, 0:C] = ya.astype(BF16)
    uc_scr[0:CONV_HALO, :] = uc_scr[T:T + CONV_HALO, :]

    up_scr[POOL_HALO:POOL_HALO + T, :] = proj_scr[:, 4 * C:4 * C + POOL_WIDTH]
    t1 = (lax.broadcasted_iota(jnp.int32, (T, POOL_GROUP), 0) + (i * T + 1)).astype(F32)
    for g, w in enumerate(POOL_WINDOWS):
        lo, hi = g * POOL_GROUP, (g + 1) * POOL_GROUP
        s = up_scr[:, lo:hi]
        k = 1
        while k < w:
            s = s + pltpu.roll(s, k, 0)
            k *= 2
        u = up_scr[POOL_HALO:POOL_HALO + T, lo:hi]
        d = s[POOL_HALO:, :] / jnp.minimum(t1, float(w)) - u
        yb = jnp.dot(d.astype(BF16), pw_ref[g].astype(BF16), preferred_element_type=F32)
        gate = proj_scr[:, 4 * C + POOL_WIDTH + lo:4 * C + POOL_WIDTH + hi]
        y_scr[:, C + lo:C + hi] = (yb * ps_ref[:, lo:hi] * jax.nn.silu(gate)).astype(BF16)
    up_scr[0:POOL_HALO, :] = up_scr[T:T + POOL_HALO, :]

    o_ref[...] = x_ref[...] + jnp.dot(y_scr[...], wout_ref[...].astype(BF16), preferred_element_type=F32)


def _even_layer(x, norm_g, w_in, conv_w, pool_w, pool_scale, w_out, next_w_in_t, next_w_out):
    S = x.shape[0]
    T = min(TILE_EVEN, S)
    steps = S // T
    n_gates = 2 * MLSTM_HEADS
    assert S % T == 0
    assert next_w_in_t.shape == (O_IF + n_gates, D_MODEL) and next_w_out.shape == (ODD_OUT, D_MODEL)
    ra, rb, ro = O_DO // steps, (O_IF - O_DO) // steps, ODD_OUT // steps
    assert ra * steps == O_DO and rb * steps == O_IF - O_DO and ro * steps == ODD_OUT
    const = _const_spec
    slab = lambda r: pl.BlockSpec((r, D_MODEL), lambda i: (i, 0))
    return pl.pallas_call(
        _even_kernel,
        grid=(steps,),
        in_specs=[
            pl.BlockSpec((T, D_MODEL), lambda i: (i, 0)),
            const(1, D_MODEL),
            const(D_MODEL, EVEN_IN),
            const(CONV_K, CONV_WIDTH),
            const(len(POOL_WINDOWS), POOL_GROUP, POOL_GROUP),
            const(1, POOL_WIDTH),
            const(EVEN_OUT, D_MODEL),
            slab(ra),
            pl.BlockSpec((pl.Element(rb), pl.Element(D_MODEL)), lambda i: (pl.multiple_of(O_DO + n_gates + i * rb, n_gates), 0)),
            pl.BlockSpec((n_gates, D_MODEL), lambda i: (O_DO // n_gates, 0)),
            slab(ro),
        ],
        out_specs=[pl.BlockSpec((T, D_MODEL), lambda i: (i, 0)),
                   slab(ra), slab(rb), pl.BlockSpec((GATE_PAD, D_MODEL), lambda i: (0, 0)), slab(ro)],
        out_shape=[jax.ShapeDtypeStruct((S, D_MODEL), F32),
                   jax.ShapeDtypeStruct((O_DO, D_MODEL), BF16),
                   jax.ShapeDtypeStruct((O_IF - O_DO, D_MODEL), BF16),
                   jax.ShapeDtypeStruct((GATE_PAD, D_MODEL), BF16),
                   jax.ShapeDtypeStruct((ODD_OUT, D_MODEL), BF16)],
        scratch_shapes=[
            pltpu.VMEM((T, EVEN_IN), F32),
            pltpu.VMEM((T + CONV_HALO, CONV_WIDTH), F32),
            pltpu.VMEM((T + POOL_HALO, POOL_WIDTH), F32),
            pltpu.VMEM((T, EVEN_OUT), BF16),
        ],
        compiler_params=pltpu.CompilerParams(
            dimension_semantics=("arbitrary",), vmem_limit_bytes=VMEM_LIMIT),
        name="even_layer",
    )(x, norm_g.reshape(1, D_MODEL), w_in, conv_w, pool_w, pool_scale.reshape(1, POOL_WIDTH), w_out,
      next_w_in_t, next_w_in_t, next_w_in_t, next_w_out)


def _lane_scan(x, op, identity):
    n = x.shape[1]
    lane = lax.broadcasted_iota(jnp.int32, x.shape, 1)
    k = 1
    while k < n:
        x = op(x, jnp.where(lane >= k, pltpu.roll(x, k, 1), identity))
        k *= 2
    return x


def _odd_init(sink_ref, kvbuf, abias, c_scr, m_scr):
    for b in range(2 * ATT_KV_HEADS):
        kvbuf[b, 0:BLOCK, :] = jnp.zeros((BLOCK, LANES), BF16)
    c_scr[...] = jnp.zeros(c_scr.shape, F32)
    m_scr[...] = jnp.zeros(m_scr.shape, F32)
    qi = lax.broadcasted_iota(jnp.int32, (BLOCK, 2 * BLOCK), 0)
    kj = lax.broadcasted_iota(jnp.int32, (BLOCK, 2 * BLOCK), 1)
    dist = qi + BLOCK - kj
    ok = (dist >= 0) & (dist < WINDOW)
    distf = dist.astype(F32)
    for hd in range(ATT_HEADS):
        slope = 2.0 ** (-8.0 * (hd + 1) / ATT_HEADS)
        band = jnp.where(ok, (-slope * LOG2E) * distf, -jnp.inf)
        abias[hd * BLOCK:(hd + 1) * BLOCK, :] = jnp.where(kj == SINK_KEY, sink_ref[hd] * LOG2E, band)


class _InProj:
    def __init__(self, x_ref, g_ref, wa_ref, wb_ref, wg_ref, h_scr, proj):
        self.x_ref, self.g_ref, self.h_scr, self.proj = x_ref, g_ref, h_scr, proj
        self.todo = []
        rows = x_ref.shape[0]
        for w_ref, base in ((wa_ref, 0), (wb_ref, O_DO), (wg_ref, O_IF)):
            n = w_ref.shape[0]
            for c0 in range(0, n, MXU_WIDTH):
                for r0 in range(0, rows, INPROJ_ROWS):
                    self.todo.append((w_ref, base, c0, min(c0 + MXU_WIDTH, n), r0, min(r0 + INPROJ_ROWS, rows)))

    def normalise(self):
        self.h_scr[...] = _rmsnorm(self.x_ref[...], self.g_ref[...]).astype(BF16)

    def step(self):
        if self.todo:
            w_ref, base, c0, c1, r0, r1 = self.todo.pop(0)
            self.proj[r0:r1, base + c0:base + c1] = lax.dot_general(
                self.h_scr[r0:r1, :], w_ref[c0:c1, :], (((1,), (1,)), ((), ())), preferred_element_type=F32)

    def finish(self):
        while self.todo:
            self.step()


def _odd_mixers(first, kv0, proj, x_ref, o_ref, bias_ref, wout_ref, fg_ref,
                kvbuf, abias, c_scr, m_scr, y_scr, nxt, after=None):
    T = x_ref.shape[0]
    NB = T // BLOCK
    H = MLSTM_HEADS
    L = min(MCHUNK, T)
    NEG = -jnp.inf

    lane_q = lax.broadcasted_iota(jnp.int32, (BLOCK, LANES), 1)
    low = lane_q < ATT_HEAD_DIM
    low_t = lax.broadcasted_iota(jnp.int32, (T, LANES), 1) < ATT_HEAD_DIM
    for b, off in enumerate((O_CK, O_CV)):
        kv = proj[:, off:off + ATT_KV_WIDTH]
        sw = pltpu.roll(kv, ATT_HEAD_DIM, 1)
        kvbuf[2 * b, kv0 + BLOCK:kv0 + BLOCK + T, :] = jnp.where(low_t, kv, sw).astype(BF16)
        kvbuf[2 * b + 1, kv0 + BLOCK:kv0 + BLOCK + T, :] = jnp.where(low_t, sw, kv).astype(BF16)
    if first is not False:
        key_lane = lax.broadcasted_iota(jnp.int32, (BLOCK, 2 * BLOCK), 1)
        no_prev = jnp.where((key_lane < BLOCK) & (key_lane != SINK_KEY), jnp.where(first, NEG, 0.0), 0.0)
    sink_row = lax.broadcasted_iota(jnp.int32, (BF16_ROWS, LANES), 0) == SINK_KEY

    def drop_sink_row(a):
        head = a[0:BF16_ROWS, :]
        return jnp.concatenate([jnp.where(sink_row, jnp.zeros_like(head), head), a[BF16_ROWS:, :]], axis=0)

    scale = ATT_HEAD_DIM ** -0.5 * LOG2E
    ones_v = jnp.ones((2 * BLOCK, LANES), BF16)

    def scores(j, g):
        r0 = j * BLOCK
        kk = drop_sink_row(kvbuf[g, kv0 + r0:kv0 + r0 + 2 * BLOCK, :])
        qs = []
        for p in range(g * ATT_GROUP // 2, (g + 1) * ATT_GROUP // 2):
            qp = proj[r0:r0 + BLOCK, O_CQ + p * LANES:O_CQ + (p + 1) * LANES] * scale
            qs.append(jnp.where(low, qp, 0.0).astype(BF16))
            qs.append(jnp.where(low, 0.0, qp).astype(BF16))
        return lax.dot_general(jnp.concatenate(qs, axis=0), kk, (((1,), (1,)), ((), ())),
                               preferred_element_type=F32)

    def attention():
        slots = [(j, g) for j in range(NB) for g in range(ATT_KV_HEADS)]
        s_next = scores(*slots[0])
        for n, (j, g) in enumerate(slots):
            r0 = j * BLOCK
            s_all = s_next
            if n + 1 < len(slots):
                s_next = scores(*slots[n + 1])
            if n % ATTN_SKIP_EVERY != ATTN_SKIP_EVERY - 1:
                nxt.step()
            ps = []
            for r in range(ATT_GROUP):
                hd = g * ATT_GROUP + r
                s = s_all[r * BLOCK:(r + 1) * BLOCK, :] + abias[hd * BLOCK:(hd + 1) * BLOCK, :]
                if j == 0 and first is not False:
                    s = s + no_prev
                ps.append(jnp.exp2(s - jnp.max(s, axis=1, keepdims=True)).astype(BF16))
            vv = jnp.concatenate(
                [drop_sink_row(kvbuf[ATT_KV_HEADS + g, kv0 + r0:kv0 + r0 + 2 * BLOCK, :]), ones_v], axis=1)
            o_all = jnp.dot(jnp.concatenate(ps, axis=0), vv, preferred_element_type=F32)
            for pp in range(ATT_GROUP // 2):
                p = g * ATT_GROUP // 2 + pp
                lo = o_all[2 * pp * BLOCK:(2 * pp + 1) * BLOCK, :]
                hi = o_all[(2 * pp + 1) * BLOCK:(2 * pp + 2) * BLOCK, :]
                att = jnp.where(low, lo[:, :LANES] / lo[:, LANES:], hi[:, :LANES] / hi[:, LANES:])
                gate = proj[r0:r0 + BLOCK, O_GC + p * LANES:O_GC + (p + 1) * LANES]
                y_scr[r0:r0 + BLOCK, p * LANES:(p + 1) * LANES] = (att * jax.nn.silu(gate)).astype(BF16)

    gt = proj[:, O_IF:O_IF + GATE_PAD].T[0:2 * H, :]
    pre = gt + jnp.concatenate([bias_ref[...]] * (T // LANES), axis=1)
    row = lax.broadcasted_iota(jnp.int32, (2 * H, T), 0)
    top = row < H
    bcum = _lane_scan(jax.nn.log_sigmoid(pre), jnp.add, 0.0)
    bsw = pltpu.roll(bcum, H, 0)
    a_top = pre - bsw
    ag = jnp.where(top, a_top, pltpu.roll(a_top, H, 0))
    bf = jnp.where(top, bsw, bcum)
    m_in = jnp.concatenate([m_scr[...]] * (T // LANES), axis=1)
    mg = jnp.maximum(m_in, _lane_scan(ag, jnp.maximum, NEG))
    m_t = bf + mg
    nrm = jnp.exp(-m_t)
    inter_parts, wk_parts, decay_parts = [], [], []
    for c in range(T // L):
        m_start = m_in[:, 0:1] if c == 0 else mg[:, c * L - 1:c * L]
        m_last = mg[:, (c + 1) * L - 1:(c + 1) * L]
        inter_parts.append(jnp.exp(m_start - mg[:, c * L:(c + 1) * L]))
        wk_parts.append(jnp.exp(ag[:, c * L:(c + 1) * L] - m_last))
        decay_parts.append(jnp.broadcast_to(jnp.exp(m_start - m_last), (2 * H, 2 * MLSTM_V_DIM)))
    inter = jnp.concatenate(inter_parts, axis=1)
    m_scr[...] = jnp.broadcast_to(m_t[:, T - 1:T], m_scr.shape)
    ag2 = ag * LOG2E
    cols = jnp.concatenate([mg * LOG2E, inter, nrm, jnp.zeros((LANES - 6 * H, T), F32)], axis=0).T

    kT = proj[:, O_DK:O_DK + MLSTM_QK_WIDTH].T
    lane_p = lax.broadcasted_iota(jnp.int32, (L, LANES), 1)
    low_p = lane_p < MLSTM_QK_DIM
    tri = (lax.broadcasted_iota(jnp.int32, (BLOCK, BLOCK), 1) <= lax.broadcasted_iota(jnp.int32, (BLOCK, BLOCK), 0))
    ones = jnp.ones((L, MLSTM_V_DIM), BF16)
    zero_blk = jnp.zeros((BLOCK, BLOCK), BF16)
    qscale = MLSTM_QK_DIM ** -0.5

    def mlstm_chunk(c):
        r0 = c * L
        for pair in range(H // 2):
            qp = proj[r0:r0 + L, O_DQ + pair * LANES:O_DQ + (pair + 1) * LANES] * qscale
            kT_pair = kT[pair * LANES:(pair + 1) * LANES, r0:r0 + L].astype(BF16)
            c_pair = c_scr[pair * LANES:(pair + 1) * LANES, :].astype(BF16)
            for half in range(2):
                hd = 2 * pair + half
                q_h = jnp.where(low_p, qp, 0.0) if half == 0 else jnp.where(low_p, 0.0, qp)
                s = jnp.dot(q_h.astype(BF16), kT_pair, preferred_element_type=F32)
                if H * c + hd < MLSTM_CHUNKS:
                    nxt.step()
                p_rows = []
                for bi in range(L // BLOCK):
                    m_col = cols[r0 + bi * BLOCK:r0 + (bi + 1) * BLOCK, hd:hd + 1]
                    blks = []
                    for bj in range(L // BLOCK):
                        if bj > bi:
                            blks.append(zero_blk)
                            continue
                        arg = ag2[hd:hd + 1, r0 + bj * BLOCK:r0 + (bj + 1) * BLOCK] - m_col
                        if bj == bi:
                            arg = jnp.where(tri, arg, NEG)
                        sb = s[bi * BLOCK:(bi + 1) * BLOCK, bj * BLOCK:(bj + 1) * BLOCK]
                        blks.append((sb * jnp.exp2(arg)).astype(BF16))
                    p_rows.append(jnp.concatenate(blks, axis=1))
                p_bf = jnp.concatenate(p_rows, axis=0)
                v_aug = jnp.concatenate(
                    [proj[r0:r0 + L, O_DV + hd * MLSTM_V_DIM:O_DV + (hd + 1) * MLSTM_V_DIM].astype(BF16), ones],
                    axis=1)
                q_int = (q_h * cols[r0:r0 + L, 2 * H + hd:2 * H + hd + 1]).astype(BF16)
                num = jnp.dot(jnp.concatenate([p_bf, q_int], axis=1),
                              jnp.concatenate([v_aug, c_pair], axis=0), preferred_element_type=F32)
                den = jnp.maximum(jnp.abs(num[:, MLSTM_V_DIM:]), cols[r0:r0 + L, 4 * H + hd:4 * H + hd + 1])
                hm = num[:, :MLSTM_V_DIM] / den
                og = proj[r0:r0 + L, O_DO + hd * MLSTM_V_DIM:O_DO + (hd + 1) * MLSTM_V_DIM]
                gd = proj[r0:r0 + L, O_GD + hd * MLSTM_V_DIM:O_GD + (hd + 1) * MLSTM_V_DIM]
                y_scr[r0:r0 + L, ATT_WIDTH + hd * MLSTM_V_DIM:ATT_WIDTH + (hd + 1) * MLSTM_V_DIM] = (
                    hm * jax.nn.sigmoid(og) * jax.nn.silu(gd)).astype(BF16)
                kw = (kT[hd * MLSTM_QK_DIM:(hd + 1) * MLSTM_QK_DIM, r0:r0 + L]
                      * wk_parts[c][hd:hd + 1, :]).astype(BF16)
                upd = jnp.dot(kw, v_aug, preferred_element_type=F32)
                rows = slice(hd * MLSTM_QK_DIM, (hd + 1) * MLSTM_QK_DIM)
                c_scr[rows, :] = decay_parts[c][hd:hd + 1, :] * c_scr[rows, :] + upd

    attention()
    for _ in range(PRE_CHUNKS):
        nxt.step()
    for c in range(T // L):
        mlstm_chunk(c)
    nxt.finish()
    if after is not None:
        after.normalise()
    for r0 in range(0, T, OUT_ROWS):
        x2 = x_ref[r0:r0 + OUT_ROWS, :] + jnp.dot(y_scr[r0:r0 + OUT_ROWS, :], wout_ref[...],
                                                   preferred_element_type=F32)
        o_ref[r0:r0 + OUT_ROWS, :] = _rmsnorm(x2, fg_ref[...])


def _odd_kernel(sink_ref, xa_ref, xb_ref, xn_ref, g_ref, wa_ref, wb_ref, wg_ref, bias_ref, wout_ref,
                fg_ref, o_ref, proj_a, proj_b, h_a, h_b, kvbuf, abias, c_scr, m_scr, y_a, y_b):
    s = pl.program_id(0)
    T = xa_ref.shape[0]
    w = (g_ref, wa_ref, wb_ref, wg_ref)
    rest = (bias_ref, wout_ref, fg_ref, kvbuf, abias, c_scr, m_scr)

    @pl.when(s == 0)
    def _():
        _odd_init(sink_ref, kvbuf, abias, c_scr, m_scr)
        first = _InProj(xa_ref, *w, h_a, proj_a)
        first.normalise()
        first.finish()

    nxt = _InProj(xb_ref, *w, h_a, proj_b)
    nxt.normalise()
    after = _InProj(xn_ref, *w, h_b, proj_a)
    _odd_mixers(s == 0, 0, proj_a, xa_ref, o_ref.at[0:T], *rest, y_a, nxt, after)
    _odd_mixers(False, T, proj_b, xb_ref, o_ref.at[T:2 * T], *rest, y_b, after)
    for b in range(2 * ATT_KV_HEADS):
        kvbuf[b, 0:BLOCK, :] = kvbuf[b, 2 * T:2 * T + BLOCK, :]


def _odd_layer(x, norm_g, wa, wb, wg, i_bias, f_bias, sinks, wo, final_g):
    S = x.shape[0]
    T = min(TILE_ODD, S // 2)
    NT = S // T
    assert S % (2 * T) == 0 and T % BLOCK == 0 and T % min(MCHUNK, T) == 0
    bias = jnp.broadcast_to(jnp.concatenate([i_bias, f_bias]).astype(F32)[:, None], (2 * MLSTM_HEADS, LANES))
    const = _const_spec
    return pl.pallas_call(
        _odd_kernel,
        grid=(NT // 2,),
        in_specs=[
            pl.BlockSpec(memory_space=pltpu.SMEM),
            pl.BlockSpec((T, D_MODEL), lambda s: (2 * s, 0)),
            pl.BlockSpec((T, D_MODEL), lambda s: (2 * s + 1, 0)),
            pl.BlockSpec((T, D_MODEL), lambda s: (jnp.minimum(2 * s + 2, NT - 1), 0)),
            const(1, D_MODEL),
            const(O_DO, D_MODEL),
            const(O_IF - O_DO, D_MODEL),
            const(GATE_PAD, D_MODEL),
            const(2 * MLSTM_HEADS, LANES),
            const(ODD_OUT, D_MODEL),
            const(1, D_MODEL),
        ],
        out_specs=pl.BlockSpec((2 * T, D_MODEL), lambda s: (s, 0)),
        out_shape=jax.ShapeDtypeStruct((S, D_MODEL), F32),
        scratch_shapes=[
            pltpu.VMEM((T, ODD_IN_PAD), F32),
            pltpu.VMEM((T, ODD_IN_PAD), F32),
            pltpu.VMEM((T, D_MODEL), BF16),
            pltpu.VMEM((T, D_MODEL), BF16),
            pltpu.VMEM((2 * ATT_KV_HEADS, 2 * T + BLOCK, LANES), BF16),
            pltpu.VMEM((ATT_HEADS * BLOCK, 2 * BLOCK), F32),
            pltpu.VMEM((MLSTM_HEADS * MLSTM_QK_DIM, 2 * MLSTM_V_DIM), F32),
            pltpu.VMEM((2 * MLSTM_HEADS, LANES), F32),
            pltpu.VMEM((T, ODD_OUT), BF16),
            pltpu.VMEM((T, ODD_OUT), BF16),
        ],
        compiler_params=pltpu.CompilerParams(
            dimension_semantics=("arbitrary",), vmem_limit_bytes=VMEM_LIMIT),
        name="odd_layer",
    )(sinks.astype(F32), x, x, x, norm_g.reshape(1, D_MODEL), wa, wb, wg, bias, wo,
      final_g.reshape(1, D_MODEL))


def kernel(x, even_norm, even_w_in, even_conv_w, even_pool_w, even_pool_scale, even_w_out,
           odd_norm, odd_w_in, odd_i_bias, odd_f_bias, odd_sinks, odd_w_out, final_norm):
    B, S, D = x.shape
    assert B == 1 and D == D_MODEL
    assert even_norm.shape[0] == 1 and odd_norm.shape[0] == 1
    h = x.reshape(S, D)
    h, wa, wb, wg, wo = _even_layer(h, even_norm[0], even_w_in[0], even_conv_w[0], even_pool_w[0],
                                    even_pool_scale[0], even_w_out[0], odd_w_in[0].T, odd_w_out[0])
    h = _odd_layer(h, odd_norm[0], wa, wb, wg, odd_i_bias[0], odd_f_bias[0], odd_sinks[0], wo, final_norm)
    return h.reshape(B, S, D)
```

```python
import jax
import jax.numpy as jnp
from jax import lax
from jax.experimental import pallas as pl
from jax.experimental.pallas import tpu as pltpu

F32 = jnp.float32
BF16 = jnp.bfloat16

D_MODEL = 1024
RMS_EPS = 1e-6

CONV_WIDTH = 512
CONV_K = 3
POOL_WINDOWS = (2, 4, 8, 16)
POOL_WIDTH = 512
POOL_GROUP = POOL_WIDTH // len(POOL_WINDOWS)
EVEN_IN = 4 * CONV_WIDTH + 2 * POOL_WIDTH
EVEN_OUT = CONV_WIDTH + POOL_WIDTH
CONV_HALO = 8
POOL_HALO = 16

ATT_HEADS = 8
ATT_KV_HEADS = 2
ATT_GROUP = ATT_HEADS // ATT_KV_HEADS
ATT_HEAD_DIM = 64
ATT_WIDTH = ATT_HEADS * ATT_HEAD_DIM
ATT_KV_WIDTH = ATT_KV_HEADS * ATT_HEAD_DIM
WINDOW = 128
BLOCK = 128
MLSTM_HEADS = 4
MLSTM_QK_DIM = 64
MLSTM_V_DIM = 128
MLSTM_QK_WIDTH = MLSTM_HEADS * MLSTM_QK_DIM
MLSTM_WIDTH = MLSTM_HEADS * MLSTM_V_DIM
ODD_OUT = ATT_WIDTH + MLSTM_WIDTH
LANES = 128
BF16_ROWS = 16
SINK_KEY = 0
MXU_WIDTH = 256
GATE_PAD = LANES

O_CQ = 0
O_CK = O_CQ + ATT_WIDTH
O_CV = O_CK + ATT_KV_WIDTH
O_GC = O_CV + ATT_KV_WIDTH
O_DQ = O_GC + ATT_WIDTH
O_DK = O_DQ + MLSTM_QK_WIDTH
O_DV = O_DK + MLSTM_QK_WIDTH
O_DO = O_DV + MLSTM_WIDTH
O_GD = O_DO + MLSTM_WIDTH
O_IF = O_GD + MLSTM_WIDTH
ODD_IN_PAD = O_IF + GATE_PAD

TILE_EVEN = 1024
EVEN_CHUNK = 512
TILE_ODD = 512
MCHUNK = 256
INPROJ_ROWS = 512
OUT_ROWS = 256
PRE_CHUNKS = 3
MLSTM_CHUNKS = 5
ATTN_SKIP_EVERY = 4
LOG2E = 1.4426950408889634
ATT_Q_SCALE = ATT_HEAD_DIM ** -0.5 * LOG2E
MLSTM_Q_SCALE = MLSTM_QK_DIM ** -0.5
VMEM_LIMIT = 60 * 1024 * 1024


def _const_spec(*shape):
    return pl.BlockSpec(shape, lambda i: (0,) * len(shape), pipeline_mode=pl.Buffered(1))


def _rms_scale(x):
    ms = jnp.mean(x * x, axis=-1, keepdims=True)
    return x * lax.rsqrt(ms + RMS_EPS)


def _rmsnorm(x, g):
    return _rms_scale(x) * g


def _even_kernel(x_ref, g_ref, win_ref, cw_ref, pw_ref, ps_ref, wout_ref,
                 ng_ref, nwa32_ref, nwb32_ref, nwg32_ref, nwo32_ref,
                 o_ref, nwa_ref, nwb_ref, nwg_ref, nwo_ref, proj_scr, uc_scr, up_scr, y_scr, wfold_scr):
    i = pl.program_id(0)
    T = x_ref.shape[0]
    C = CONV_WIDTH

    ng = ng_ref[...]
    ra = nwa32_ref.shape[0]
    feat = lax.broadcasted_iota(jnp.int32, (ra, D_MODEL), 0) + i * ra
    q_scale = jnp.where(feat < O_CK, ATT_Q_SCALE,
                        jnp.where(feat < O_DQ, 1.0, jnp.where(feat < O_DK, MLSTM_Q_SCALE, 1.0)))
    nwa_ref[...] = (nwa32_ref[...] * ng * q_scale).astype(BF16)
    nwb_ref[...] = (nwb32_ref[...] * ng).astype(BF16)
    nwo_ref[...] = nwo32_ref[...].astype(BF16)
    nwg_ref[...] = jnp.concatenate(
        [nwg32_ref[...] * ng, jnp.zeros((GATE_PAD - 2 * MLSTM_HEADS, D_MODEL), F32)], axis=0).astype(BF16)

    @pl.when(i == 0)
    def _():
        uc_scr[0:CONV_HALO, :] = jnp.zeros((CONV_HALO, C), F32)
        up_scr[0:POOL_HALO, :] = jnp.zeros((POOL_HALO, POOL_WIDTH), F32)
        for g in range(len(POOL_WINDOWS)):
            lo, hi = g * POOL_GROUP, (g + 1) * POOL_GROUP
            wfold_scr[:, lo:hi] = jnp.dot(win_ref[:, 4 * C + lo:4 * C + hi], pw_ref[g],
                                          precision=lax.Precision.HIGHEST, preferred_element_type=F32)

    h = _rmsnorm(x_ref[...], g_ref[...]).astype(BF16)
    for c0 in range(0, EVEN_IN, EVEN_CHUNK):
        w = wfold_scr[...] if c0 == 4 * C else win_ref[:, c0:c0 + EVEN_CHUNK]
        proj_scr[:, c0:c0 + EVEN_CHUNK] = jnp.dot(h, w.astype(BF16), preferred_element_type=F32)

    uc = proj_scr[:, 2 * C:3 * C] * proj_scr[:, 0:C]
    uc_scr[CONV_HALO:CONV_HALO + T, :] = uc
    conv = (cw_ref[2:3, :] * uc
            + cw_ref[1:2, :] * uc_scr[CONV_HALO - 1:CONV_HALO - 1 + T, :]
            + cw_ref[0:1, :] * uc_scr[CONV_HALO - 2:CONV_HALO - 2 + T, :])
    ya = proj_scr[:, C:2 * C] * conv * jax.nn.silu(proj_scr[:, 3 * C:4 * C])
    y_scr[:, 0:C] = ya.astype(BF16)
    uc_scr[0:CONV_HALO, :] = uc_scr[T:T + CONV_HALO, :]

    up_scr[POOL_HALO:POOL_HALO + T, :] = proj_scr[:, 4 * C:4 * C + POOL_WIDTH]
    t1 = (lax.broadcasted_iota(jnp.int32, (T, POOL_GROUP), 0) + (i * T + 1)).astype(F32)
    for g, w in enumerate(POOL_WINDOWS):
        lo, hi = g * POOL_GROUP, (g + 1) * POOL_GROUP
        s = up_scr[:, lo:hi]
        k = 1
        while k < w:
            s = s + pltpu.roll(s, k, 0)
            k *= 2
        u = up_scr[POOL_HALO:POOL_HALO + T, lo:hi]
        yb = s[POOL_HALO:, :] / jnp.minimum(t1, float(w)) - u
        gate = proj_scr[:, 4 * C + POOL_WIDTH + lo:4 * C + POOL_WIDTH + hi]
        y_scr[:, C + lo:C + hi] = (yb * ps_ref[:, lo:hi] * jax.nn.silu(gate)).astype(BF16)
    up_scr[0:POOL_HALO, :] = up_scr[T:T + POOL_HALO, :]

    o_ref[...] = x_ref[...] + jnp.dot(y_scr[...], wout_ref[...].astype(BF16), preferred_element_type=F32)


def _even_layer(x, norm_g, w_in, conv_w, pool_w, pool_scale, w_out, next_norm_g, next_w_in_t, next_w_out):
    S = x.shape[0]
    T = min(TILE_EVEN, S)
    steps = S // T
    n_gates = 2 * MLSTM_HEADS
    assert S % T == 0
    assert next_w_in_t.shape == (O_IF + n_gates, D_MODEL) and next_w_out.shape == (ODD_OUT, D_MODEL)
    ra, rb, ro = O_DO // steps, (O_IF - O_DO) // steps, ODD_OUT // steps
    assert ra * steps == O_DO and rb * steps == O_IF - O_DO and ro * steps == ODD_OUT
    const = _const_spec
    slab = lambda r: pl.BlockSpec((r, D_MODEL), lambda i: (i, 0))
    return pl.pallas_call(
        _even_kernel,
        grid=(steps,),
        in_specs=[
            pl.BlockSpec((T, D_MODEL), lambda i: (i, 0)),
            const(1, D_MODEL),
            const(D_MODEL, EVEN_IN),
            const(CONV_K, CONV_WIDTH),
            const(len(POOL_WINDOWS), POOL_GROUP, POOL_GROUP),
            const(1, POOL_WIDTH),
            const(EVEN_OUT, D_MODEL),
            const(1, D_MODEL),
            slab(ra),
            pl.BlockSpec((pl.Element(rb), pl.Element(D_MODEL)), lambda i: (pl.multiple_of(O_DO + n_gates + i * rb, n_gates), 0)),
            pl.BlockSpec((n_gates, D_MODEL), lambda i: (O_DO // n_gates, 0)),
            slab(ro),
        ],
        out_specs=[pl.BlockSpec((T, D_MODEL), lambda i: (i, 0)),
                   slab(ra), slab(rb), pl.BlockSpec((GATE_PAD, D_MODEL), lambda i: (0, 0)), slab(ro)],
        out_shape=[jax.ShapeDtypeStruct((S, D_MODEL), F32),
                   jax.ShapeDtypeStruct((O_DO, D_MODEL), BF16),
                   jax.ShapeDtypeStruct((O_IF - O_DO, D_MODEL), BF16),
                   jax.ShapeDtypeStruct((GATE_PAD, D_MODEL), BF16),
                   jax.ShapeDtypeStruct((ODD_OUT, D_MODEL), BF16)],
        scratch_shapes=[
            pltpu.VMEM((T, EVEN_IN), F32),
            pltpu.VMEM((T + CONV_HALO, CONV_WIDTH), F32),
            pltpu.VMEM((T + POOL_HALO, POOL_WIDTH), F32),
            pltpu.VMEM((T, EVEN_OUT), BF16),
            pltpu.VMEM((D_MODEL, POOL_WIDTH), F32),
        ],
        compiler_params=pltpu.CompilerParams(
            dimension_semantics=("arbitrary",), vmem_limit_bytes=VMEM_LIMIT),
        name="even_layer",
    )(x, norm_g.reshape(1, D_MODEL), w_in, conv_w, pool_w, pool_scale.reshape(1, POOL_WIDTH), w_out,
      next_norm_g.reshape(1, D_MODEL), next_w_in_t, next_w_in_t, next_w_in_t, next_w_out)


def _lane_scan(x, op, identity):
    n = x.shape[1]
    lane = lax.broadcasted_iota(jnp.int32, x.shape, 1)
    k = 1
    while k < n:
        x = op(x, jnp.where(lane >= k, pltpu.roll(x, k, 1), identity))
        k *= 2
    return x


def _odd_init(sink_ref, kvbuf, abias, c_scr, m_scr):
    for b in range(2 * ATT_KV_HEADS):
        kvbuf[b, 0:BLOCK, :] = jnp.zeros((BLOCK, LANES), BF16)
    c_scr[...] = jnp.zeros(c_scr.shape, F32)
    m_scr[...] = jnp.zeros(m_scr.shape, F32)
    qi = lax.broadcasted_iota(jnp.int32, (BLOCK, 2 * BLOCK), 0)
    kj = lax.broadcasted_iota(jnp.int32, (BLOCK, 2 * BLOCK), 1)
    dist = qi + BLOCK - kj
    ok = (dist >= 0) & (dist < WINDOW)
    distf = dist.astype(F32)
    for hd in range(ATT_HEADS):
        slope = 2.0 ** (-8.0 * (hd + 1) / ATT_HEADS)
        band = jnp.where(ok, (-slope * LOG2E) * distf, -jnp.inf)
        abias[hd * BLOCK:(hd + 1) * BLOCK, :] = jnp.where(kj == SINK_KEY, sink_ref[hd] * LOG2E, band)


class _InProj:
    def __init__(self, x_ref, wa_ref, wb_ref, wg_ref, h_scr, proj):
        self.x_ref, self.h_scr, self.proj = x_ref, h_scr, proj
        self.todo = []
        rows = x_ref.shape[0]
        for w_ref, base in ((wa_ref, 0), (wb_ref, O_DO), (wg_ref, O_IF)):
            n = w_ref.shape[0]
            for c0 in range(0, n, MXU_WIDTH):
                for r0 in range(0, rows, INPROJ_ROWS):
                    self.todo.append((w_ref, base, c0, min(c0 + MXU_WIDTH, n), r0, min(r0 + INPROJ_ROWS, rows)))

    def normalise(self):
        self.h_scr[...] = _rms_scale(self.x_ref[...]).astype(BF16)

    def step(self):
        if self.todo:
            w_ref, base, c0, c1, r0, r1 = self.todo.pop(0)
            self.proj[r0:r1, base + c0:base + c1] = lax.dot_general(
                self.h_scr[r0:r1, :], w_ref[c0:c1, :], (((1,), (1,)), ((), ())), preferred_element_type=F32)

    def finish(self):
        while self.todo:
            self.step()


def _odd_mixers(first, kv0, proj, x_ref, o_ref, bias_ref, wout_ref, fg_ref,
                kvbuf, abias, c_scr, m_scr, y_scr, nxt, after=None):
    T = x_ref.shape[0]
    NB = T // BLOCK
    H = MLSTM_HEADS
    L = min(MCHUNK, T)
    NEG = -jnp.inf

    lane_q = lax.broadcasted_iota(jnp.int32, (BLOCK, LANES), 1)
    low = lane_q < ATT_HEAD_DIM
    low_t = lax.broadcasted_iota(jnp.int32, (T, LANES), 1) < ATT_HEAD_DIM
    for b, off in enumerate((O_CK, O_CV)):
        kv = proj[:, off:off + ATT_KV_WIDTH]
        sw = pltpu.roll(kv, ATT_HEAD_DIM, 1)
        kvbuf[2 * b, kv0 + BLOCK:kv0 + BLOCK + T, :] = jnp.where(low_t, kv, sw).astype(BF16)
        kvbuf[2 * b + 1, kv0 + BLOCK:kv0 + BLOCK + T, :] = jnp.where(low_t, sw, kv).astype(BF16)
    if first is not False:
        key_lane = lax.broadcasted_iota(jnp.int32, (BLOCK, 2 * BLOCK), 1)
        no_prev = jnp.where((key_lane < BLOCK) & (key_lane != SINK_KEY), jnp.where(first, NEG, 0.0), 0.0)
    sink_row = lax.broadcasted_iota(jnp.int32, (BF16_ROWS, LANES), 0) == SINK_KEY

    def drop_sink_row(a):
        head = a[0:BF16_ROWS, :]
        return jnp.concatenate([jnp.where(sink_row, jnp.zeros_like(head), head), a[BF16_ROWS:, :]], axis=0)

    ones_v = jnp.ones((2 * BLOCK, LANES), BF16)

    def scores(j, g):
        r0 = j * BLOCK
        kk = drop_sink_row(kvbuf[g, kv0 + r0:kv0 + r0 + 2 * BLOCK, :])
        qs = []
        for p in range(g * ATT_GROUP // 2, (g + 1) * ATT_GROUP // 2):
            qp = proj[r0:r0 + BLOCK, O_CQ + p * LANES:O_CQ + (p + 1) * LANES]
            qs.append(jnp.where(low, qp, 0.0).astype(BF16))
            qs.append(jnp.where(low, 0.0, qp).astype(BF16))
        return lax.dot_general(jnp.concatenate(qs, axis=0), kk, (((1,), (1,)), ((), ())),
                               preferred_element_type=F32)

    def attention():
        slots = [(j, g) for j in range(NB) for g in range(ATT_KV_HEADS)]
        s_next = scores(*slots[0])
        for n, (j, g) in enumerate(slots):
            r0 = j * BLOCK
            s_all = s_next
            if n + 1 < len(slots):
                s_next = scores(*slots[n + 1])
            if n % ATTN_SKIP_EVERY != ATTN_SKIP_EVERY - 1:
                nxt.step()
            ps = []
            for r in range(ATT_GROUP):
                hd = g * ATT_GROUP + r
                s = s_all[r * BLOCK:(r + 1) * BLOCK, :] + abias[hd * BLOCK:(hd + 1) * BLOCK, :]
                if j == 0 and first is not False:
                    s = s + no_prev
                ps.append(jnp.exp2(s - jnp.max(s, axis=1, keepdims=True)).astype(BF16))
            vv = jnp.concatenate(
                [drop_sink_row(kvbuf[ATT_KV_HEADS + g, kv0 + r0:kv0 + r0 + 2 * BLOCK, :]), ones_v], axis=1)
            o_all = jnp.dot(jnp.concatenate(ps, axis=0), vv, preferred_element_type=F32)
            for pp in range(ATT_GROUP // 2):
                p = g * ATT_GROUP // 2 + pp
                lo = o_all[2 * pp * BLOCK:(2 * pp + 1) * BLOCK, :]
                hi = o_all[(2 * pp + 1) * BLOCK:(2 * pp + 2) * BLOCK, :]
                att = jnp.where(low, lo[:, :LANES] / lo[:, LANES:], hi[:, :LANES] / hi[:, LANES:])
                gate = proj[r0:r0 + BLOCK, O_GC + p * LANES:O_GC + (p + 1) * LANES]
                y_scr[r0:r0 + BLOCK, p * LANES:(p + 1) * LANES] = (att * jax.nn.silu(gate)).astype(BF16)

    gt = proj[:, O_IF:O_IF + GATE_PAD].T[0:2 * H, :]
    pre = gt + jnp.concatenate([bias_ref[...]] * (T // LANES), axis=1)
    row = lax.broadcasted_iota(jnp.int32, (2 * H, T), 0)
    top = row < H
    bcum = _lane_scan(jax.nn.log_sigmoid(pre), jnp.add, 0.0)
    bsw = pltpu.roll(bcum, H, 0)
    a_top = pre - bsw
    ag = jnp.where(top, a_top, pltpu.roll(a_top, H, 0))
    bf = jnp.where(top, bsw, bcum)
    m_in = jnp.concatenate([m_scr[...]] * (T // LANES), axis=1)
    mg = jnp.maximum(m_in, _lane_scan(ag, jnp.maximum, NEG))
    m_t = bf + mg
    nrm = jnp.exp(-m_t)
    inter_parts, wk_parts, decay_parts = [], [], []
    for c in range(T // L):
        m_start = m_in[:, 0:1] if c == 0 else mg[:, c * L - 1:c * L]
        m_last = mg[:, (c + 1) * L - 1:(c + 1) * L]
        inter_parts.append(jnp.exp(m_start - mg[:, c * L:(c + 1) * L]))
        wk_parts.append(jnp.exp(ag[:, c * L:(c + 1) * L] - m_last))
        decay_parts.append(jnp.broadcast_to(jnp.exp(m_start - m_last), (2 * H, 2 * MLSTM_V_DIM)))
    inter = jnp.concatenate(inter_parts, axis=1)
    m_scr[...] = jnp.broadcast_to(m_t[:, T - 1:T], m_scr.shape)
    ag2 = ag * LOG2E
    cols = jnp.concatenate([mg * LOG2E, inter, nrm, jnp.zeros((LANES - 6 * H, T), F32)], axis=0).T

    kT = proj[:, O_DK:O_DK + MLSTM_QK_WIDTH].T
    lane_p = lax.broadcasted_iota(jnp.int32, (L, LANES), 1)
    low_p = lane_p < MLSTM_QK_DIM
    tri = (lax.broadcasted_iota(jnp.int32, (BLOCK, BLOCK), 1) <= lax.broadcasted_iota(jnp.int32, (BLOCK, BLOCK), 0))
    ones = jnp.ones((L, MLSTM_V_DIM), BF16)
    zero_blk = jnp.zeros((BLOCK, BLOCK), BF16)

    def mlstm_chunk(c):
        r0 = c * L
        for pair in range(H // 2):
            qp = proj[r0:r0 + L, O_DQ + pair * LANES:O_DQ + (pair + 1) * LANES]
            kT_pair = kT[pair * LANES:(pair + 1) * LANES, r0:r0 + L].astype(BF16)
            c_pair = c_scr[pair * LANES:(pair + 1) * LANES, :].astype(BF16)
            for half in range(2):
                hd = 2 * pair + half
                q_h = jnp.where(low_p, qp, 0.0) if half == 0 else jnp.where(low_p, 0.0, qp)
                s = jnp.dot(q_h.astype(BF16), kT_pair, preferred_element_type=F32)
                if H * c + hd < MLSTM_CHUNKS:
                    nxt.step()
                p_rows = []
                for bi in range(L // BLOCK):
                    m_col = cols[r0 + bi * BLOCK:r0 + (bi + 1) * BLOCK, hd:hd + 1]
                    blks = []
                    for bj in range(L // BLOCK):
                        if bj > bi:
                            blks.append(zero_blk)
                            continue
                        arg = ag2[hd:hd + 1, r0 + bj * BLOCK:r0 + (bj + 1) * BLOCK] - m_col
                        if bj == bi:
                            arg = jnp.where(tri, arg, NEG)
                        sb = s[bi * BLOCK:(bi + 1) * BLOCK, bj * BLOCK:(bj + 1) * BLOCK]
                        blks.append((sb * jnp.exp2(arg)).astype(BF16))
                    p_rows.append(jnp.concatenate(blks, axis=1))
                p_bf = jnp.concatenate(p_rows, axis=0)
                v_aug = jnp.concatenate(
                    [proj[r0:r0 + L, O_DV + hd * MLSTM_V_DIM:O_DV + (hd + 1) * MLSTM_V_DIM].astype(BF16), ones],
                    axis=1)
                q_int = (q_h * cols[r0:r0 + L, 2 * H + hd:2 * H + hd + 1]).astype(BF16)
                num = jnp.dot(jnp.concatenate([p_bf, q_int], axis=1),
                              jnp.concatenate([v_aug, c_pair], axis=0), preferred_element_type=F32)
                den = jnp.maximum(jnp.abs(num[:, MLSTM_V_DIM:]), cols[r0:r0 + L, 4 * H + hd:4 * H + hd + 1])
                hm = num[:, :MLSTM_V_DIM] / den
                og = proj[r0:r0 + L, O_DO + hd * MLSTM_V_DIM:O_DO + (hd + 1) * MLSTM_V_DIM]
                gd = proj[r0:r0 + L, O_GD + hd * MLSTM_V_DIM:O_GD + (hd + 1) * MLSTM_V_DIM]
                y_scr[r0:r0 + L, ATT_WIDTH + hd * MLSTM_V_DIM:ATT_WIDTH + (hd + 1) * MLSTM_V_DIM] = (
                    hm * jax.nn.sigmoid(og) * jax.nn.silu(gd)).astype(BF16)
                kw = (kT[hd * MLSTM_QK_DIM:(hd + 1) * MLSTM_QK_DIM, r0:r0 + L]
                      * wk_parts[c][hd:hd + 1, :]).astype(BF16)
                upd = jnp.dot(kw, v_aug, preferred_element_type=F32)
                rows = slice(hd * MLSTM_QK_DIM, (hd + 1) * MLSTM_QK_DIM)
                c_scr[rows, :] = decay_parts[c][hd:hd + 1, :] * c_scr[rows, :] + upd

    attention()
    for _ in range(PRE_CHUNKS):
        nxt.step()
    for c in range(T // L):
        mlstm_chunk(c)
    nxt.finish()
    if after is not None:
        after.normalise()
    for r0 in range(0, T, OUT_ROWS):
        x2 = x_ref[r0:r0 + OUT_ROWS, :] + jnp.dot(y_scr[r0:r0 + OUT_ROWS, :], wout_ref[...],
                                                   preferred_element_type=F32)
        o_ref[r0:r0 + OUT_ROWS, :] = _rmsnorm(x2, fg_ref[...])


def _odd_kernel(sink_ref, xa_ref, xb_ref, xn_ref, wa_ref, wb_ref, wg_ref, bias_ref, wout_ref,
                fg_ref, o_ref, proj_a, proj_b, h_a, h_b, kvbuf, abias, c_scr, m_scr, y_a, y_b):
    s = pl.program_id(0)
    T = xa_ref.shape[0]
    w = (wa_ref, wb_ref, wg_ref)
    rest = (bias_ref, wout_ref, fg_ref, kvbuf, abias, c_scr, m_scr)

    @pl.when(s == 0)
    def _():
        _odd_init(sink_ref, kvbuf, abias, c_scr, m_scr)
        first = _InProj(xa_ref, *w, h_a, proj_a)
        first.normalise()
        first.finish()

    nxt = _InProj(xb_ref, *w, h_a, proj_b)
    nxt.normalise()
    after = _InProj(xn_ref, *w, h_b, proj_a)
    _odd_mixers(s == 0, 0, proj_a, xa_ref, o_ref.at[0:T], *rest, y_a, nxt, after)
    _odd_mixers(False, T, proj_b, xb_ref, o_ref.at[T:2 * T], *rest, y_b, after)
    for b in range(2 * ATT_KV_HEADS):
        kvbuf[b, 0:BLOCK, :] = kvbuf[b, 2 * T:2 * T + BLOCK, :]


def _odd_layer(x, wa, wb, wg, i_bias, f_bias, sinks, wo, final_g):
    S = x.shape[0]
    T = min(TILE_ODD, S // 2)
    NT = S // T
    assert S % (2 * T) == 0 and T % BLOCK == 0 and T % min(MCHUNK, T) == 0
    bias = jnp.broadcast_to(jnp.concatenate([i_bias, f_bias]).astype(F32)[:, None], (2 * MLSTM_HEADS, LANES))
    const = _const_spec
    return pl.pallas_call(
        _odd_kernel,
        grid=(NT // 2,),
        in_specs=[
            pl.BlockSpec(memory_space=pltpu.SMEM),
            pl.BlockSpec((T, D_MODEL), lambda s: (2 * s, 0)),
            pl.BlockSpec((T, D_MODEL), lambda s: (2 * s + 1, 0)),
            pl.BlockSpec((T, D_MODEL), lambda s: (jnp.minimum(2 * s + 2, NT - 1), 0)),
            const(O_DO, D_MODEL),
            const(O_IF - O_DO, D_MODEL),
            const(GATE_PAD, D_MODEL),
            const(2 * MLSTM_HEADS, LANES),
            const(ODD_OUT, D_MODEL),
            const(1, D_MODEL),
        ],
        out_specs=pl.BlockSpec((2 * T, D_MODEL), lambda s: (s, 0)),
        out_shape=jax.ShapeDtypeStruct((S, D_MODEL), F32),
        scratch_shapes=[
            pltpu.VMEM((T, ODD_IN_PAD), F32),
            pltpu.VMEM((T, ODD_IN_PAD), F32),
            pltpu.VMEM((T, D_MODEL), BF16),
            pltpu.VMEM((T, D_MODEL), BF16),
            pltpu.VMEM((2 * ATT_KV_HEADS, 2 * T + BLOCK, LANES), BF16),
            pltpu.VMEM((ATT_HEADS * BLOCK, 2 * BLOCK), F32),
            pltpu.VMEM((MLSTM_HEADS * MLSTM_QK_DIM, 2 * MLSTM_V_DIM), F32),
            pltpu.VMEM((2 * MLSTM_HEADS, LANES), F32),
            pltpu.VMEM((T, ODD_OUT), BF16),
            pltpu.VMEM((T, ODD_OUT), BF16),
        ],
        compiler_params=pltpu.CompilerParams(
            dimension_semantics=("arbitrary",), vmem_limit_bytes=VMEM_LIMIT),
        name="odd_layer",
    )(sinks.astype(F32), x, x, x, wa, wb, wg, bias, wo,
      final_g.reshape(1, D_MODEL))


def kernel(x, even_norm, even_w_in, even_conv_w, even_pool_w, even_pool_scale, even_w_out,
           odd_norm, odd_w_in, odd_i_bias, odd_f_bias, odd_sinks, odd_w_out, final_norm):
    B, S, D = x.shape
    assert B == 1 and D == D_MODEL
    assert even_norm.shape[0] == 1 and odd_norm.shape[0] == 1
    h = x.reshape(S, D)
    h, wa, wb, wg, wo = _even_layer(h, even_norm[0], even_w_in[0], even_conv_w[0], even_pool_w[0],
                                    even_pool_scale[0], even_w_out[0], odd_norm[0], odd_w_in[0].T, odd_w_out[0])
    h = _odd_layer(h, wa, wb, wg, odd_i_bias[0], odd_f_bias[0], odd_sinks[0], wo, final_norm)
    return h.reshape(B, S, D)
```

```python
import jax
import jax.numpy as jnp
from jax import lax
from jax.experimental import pallas as pl
from jax.experimental.pallas import tpu as pltpu

F32 = jnp.float32
BF16 = jnp.bfloat16

D_MODEL = 1024
RMS_EPS = 1e-6

CONV_WIDTH = 512
CONV_K = 3
POOL_WINDOWS = (2, 4, 8, 16)
POOL_WIDTH = 512
POOL_GROUP = POOL_WIDTH // len(POOL_WINDOWS)
EVEN_IN = 4 * CONV_WIDTH + 2 * POOL_WIDTH
EVEN_OUT = CONV_WIDTH + POOL_WIDTH
CONV_HALO = 8
POOL_HALO = 16

ATT_HEADS = 8
ATT_KV_HEADS = 2
ATT_GROUP = ATT_HEADS // ATT_KV_HEADS
ATT_HEAD_DIM = 64
ATT_WIDTH = ATT_HEADS * ATT_HEAD_DIM
ATT_KV_WIDTH = ATT_KV_HEADS * ATT_HEAD_DIM
WINDOW = 128
BLOCK = 128
MLSTM_HEADS = 4
MLSTM_QK_DIM = 64
MLSTM_V_DIM = 128
MLSTM_QK_WIDTH = MLSTM_HEADS * MLSTM_QK_DIM
MLSTM_WIDTH = MLSTM_HEADS * MLSTM_V_DIM
ODD_OUT = ATT_WIDTH + MLSTM_WIDTH
LANES = 128
BF16_ROWS = 16
SINK_KEY = 0
MXU_WIDTH = 256
GATE_PAD = LANES

O_CQ = 0
O_CK = O_CQ + ATT_WIDTH
O_CV = O_CK + ATT_KV_WIDTH
O_GC = O_CV + ATT_KV_WIDTH
O_DQ = O_GC + ATT_WIDTH
O_DK = O_DQ + MLSTM_QK_WIDTH
O_DV = O_DK + MLSTM_QK_WIDTH
O_DO = O_DV + MLSTM_WIDTH
O_GD = O_DO + MLSTM_WIDTH
O_IF = O_GD + MLSTM_WIDTH
ODD_IN_PAD = O_IF + GATE_PAD

TILE_EVEN = 1024
EVEN_CHUNK = 512
TILE_ODD = 512
MCHUNK = 256
INPROJ_ROWS = 512
OUT_ROWS = 256
PRE_CHUNKS = 3
MLSTM_CHUNKS = 5
ATTN_SKIP_EVERY = 4
LOG2E = 1.4426950408889634
ATT_Q_SCALE = ATT_HEAD_DIM ** -0.5 * LOG2E
MLSTM_Q_SCALE = MLSTM_QK_DIM ** -0.5
VMEM_LIMIT = 60 * 1024 * 1024


def _const_spec(*shape):
    return pl.BlockSpec(shape, lambda i: (0,) * len(shape), pipeline_mode=pl.Buffered(1))


def _rms_scale(x):
    ms = jnp.mean(x * x, axis=-1, keepdims=True)
    return x * lax.rsqrt(ms + RMS_EPS)


def _rmsnorm(x, g):
    return _rms_scale(x) * g


def _even_kernel(x_ref, g_ref, win_ref, cw_ref, pw_ref, ps_ref, wout_ref,
                 ng_ref, nwa32_ref, nwb32_ref, nwg32_ref, nwo32_ref,
                 o_ref, nwa_ref, nwb_ref, nwg_ref, nwo_ref, proj_scr, uc_scr, up_scr, y_scr, wfold_scr):
    i = pl.program_id(0)
    T = x_ref.shape[0]
    C = CONV_WIDTH

    ng = ng_ref[...]
    ra = nwa32_ref.shape[0]
    feat = lax.broadcasted_iota(jnp.int32, (ra, D_MODEL), 0) + i * ra
    q_scale = jnp.where(feat < O_CK, ATT_Q_SCALE,
                        jnp.where(feat < O_DQ, 1.0, jnp.where(feat < O_DK, MLSTM_Q_SCALE, 1.0)))
    nwa_ref[...] = (nwa32_ref[...] * ng * q_scale).astype(BF16)
    nwb_ref[...] = (nwb32_ref[...] * ng).astype(BF16)
    nwo_ref[...] = nwo32_ref[...].astype(BF16)
    nwg_ref[...] = jnp.concatenate(
        [nwg32_ref[...] * ng, jnp.zeros((GATE_PAD - 2 * MLSTM_HEADS, D_MODEL), F32)], axis=0).astype(BF16)

    @pl.when(i == 0)
    def _():
        uc_scr[0:CONV_HALO, :] = jnp.zeros((CONV_HALO, C), F32)
        up_scr[0:POOL_HALO, :] = jnp.zeros((POOL_HALO, POOL_WIDTH), F32)
        for g in range(len(POOL_WINDOWS)):
            lo, hi = g * POOL_GROUP, (g + 1) * POOL_GROUP
            wfold_scr[:, lo:hi] = jnp.dot(win_ref[:, 4 * C + lo:4 * C + hi], pw_ref[g],
                                          precision=lax.Precision.HIGHEST, preferred_element_type=F32)

    h = _rmsnorm(x_ref[...], g_ref[...]).astype(BF16)
    for c0 in range(0, EVEN_IN, EVEN_CHUNK):
        w = wfold_scr[...] if c0 == 4 * C else win_ref[:, c0:c0 + EVEN_CHUNK]
        proj_scr[:, c0:c0 + EVEN_CHUNK] = jnp.dot(h, w.astype(BF16), preferred_element_type=F32)

    uc = proj_scr[:, 2 * C:3 * C] * proj_scr[:, 0:C]
    uc_scr[CONV_HALO:CONV_HALO + T, :] = uc
    conv = (cw_ref[2:3, :] * uc
            + cw_ref[1:2, :] * uc_scr[CONV_HALO - 1:CONV_HALO - 1 + T, :]
            + cw_ref[0:1, :] * uc_scr[CONV_HALO - 2:CONV_HALO - 2 + T, :])
    ya = proj_scr[:, C:2 * C] * conv * jax.nn.silu(proj_scr[:, 3 * C:4 * C])
    y_scr[:, 0:C] = ya.astype(BF16)
    uc_scr[0:CONV_HALO, :] = uc_scr[T:T + CONV_HALO, :]

    up_scr[POOL_HALO:POOL_HALO + T, :] = proj_scr[:, 4 * C:4 * C + POOL_WIDTH]
    t1 = (lax.broadcasted_iota(jnp.int32, (T, POOL_GROUP), 0) + (i * T + 1)).astype(F32)
    for g, w in enumerate(POOL_WINDOWS):
        lo, hi = g * POOL_GROUP, (g + 1) * POOL_GROUP
        s = up_scr[:, lo:hi]
        k = 1
        while k < w:
            s = s + pltpu.roll(s, k, 0)
            k *= 2
        u = up_scr[POOL_HALO:POOL_HALO + T, lo:hi]
        yb = s[POOL_HALO:, :] / jnp.minimum(t1, float(w)) - u
        gate = proj_scr[:, 4 * C + POOL_WIDTH + lo:4 * C + POOL_WIDTH + hi]
        y_scr[:, C + lo:C + hi] = (yb * ps_ref[:, lo:hi] * jax.nn.silu(gate)).astype(BF16)
    up_scr[0:POOL_HALO, :] = up_scr[T:T + POOL_HALO, :]

    o_ref[...] = x_ref[...] + jnp.dot(y_scr[...], wout_ref[...].astype(BF16), preferred_element_type=F32)


def _even_layer(x, norm_g, w_in, conv_w, pool_w, pool_scale, w_out, next_norm_g, next_w_in_t, next_w_out):
    S = x.shape[0]
    T = min(TILE_EVEN, S)
    steps = S // T
    n_gates = 2 * MLSTM_HEADS
    assert S % T == 0
    assert next_w_in_t.shape == (O_IF + n_gates, D_MODEL) and next_w_out.shape == (ODD_OUT, D_MODEL)
    ra, rb, ro = O_DO // steps, (O_IF - O_DO) // steps, ODD_OUT // steps
    assert ra * steps == O_DO and rb * steps == O_IF - O_DO and ro * steps == ODD_OUT
    const = _const_spec
    slab = lambda r: pl.BlockSpec((r, D_MODEL), lambda i: (i, 0))
    return pl.pallas_call(
        _even_kernel,
        grid=(steps,),
        in_specs=[
            pl.BlockSpec((T, D_MODEL), lambda i: (i, 0)),
            const(1, D_MODEL),
            const(D_MODEL, EVEN_IN),
            const(CONV_K, CONV_WIDTH),
            const(len(POOL_WINDOWS), POOL_GROUP, POOL_GROUP),
            const(1, POOL_WIDTH),
            const(EVEN_OUT, D_MODEL),
            const(1, D_MODEL),
            slab(ra),
            pl.BlockSpec((pl.Element(rb), pl.Element(D_MODEL)), lambda i: (pl.multiple_of(O_DO + n_gates + i * rb, n_gates), 0)),
            pl.BlockSpec((n_gates, D_MODEL), lambda i: (O_DO // n_gates, 0)),
            slab(ro),
        ],
        out_specs=[pl.BlockSpec((T, D_MODEL), lambda i: (i, 0)),
                   slab(ra), slab(rb), pl.BlockSpec((GATE_PAD, D_MODEL), lambda i: (0, 0)), slab(ro)],
        out_shape=[jax.ShapeDtypeStruct((S, D_MODEL), F32),
                   jax.ShapeDtypeStruct((O_DO, D_MODEL), BF16),
                   jax.ShapeDtypeStruct((O_IF - O_DO, D_MODEL), BF16),
                   jax.ShapeDtypeStruct((GATE_PAD, D_MODEL), BF16),
                   jax.ShapeDtypeStruct((ODD_OUT, D_MODEL), BF16)],
        scratch_shapes=[
            pltpu.VMEM((T, EVEN_IN), F32),
            pltpu.VMEM((T + CONV_HALO, CONV_WIDTH), F32),
            pltpu.VMEM((T + POOL_HALO, POOL_WIDTH), F32),
            pltpu.VMEM((T, EVEN_OUT), BF16),
            pltpu.VMEM((D_MODEL, POOL_WIDTH), F32),
        ],
        compiler_params=pltpu.CompilerParams(
            dimension_semantics=("arbitrary",), vmem_limit_bytes=VMEM_LIMIT),
        name="even_layer",
    )(x, norm_g.reshape(1, D_MODEL), w_in, conv_w, pool_w, pool_scale.reshape(1, POOL_WIDTH), w_out,
      next_norm_g.reshape(1, D_MODEL), next_w_in_t, next_w_in_t, next_w_in_t, next_w_out)


def _lane_scan(x, op, identity):
    n = x.shape[1]
    lane = lax.broadcasted_iota(jnp.int32, x.shape, 1)
    k = 1
    while k < n:
        x = op(x, jnp.where(lane >= k, pltpu.roll(x, k, 1), identity))
        k *= 2
    return x


def _odd_init(sink_ref, kvbuf, abias, c_scr, m_scr):
    for b in range(2 * ATT_KV_HEADS):
        kvbuf[b, 0:BLOCK, :] = jnp.zeros((BLOCK, LANES), BF16)
    c_scr[...] = jnp.zeros(c_scr.shape, F32)
    m_scr[...] = jnp.zeros(m_scr.shape, F32)
    qi = lax.broadcasted_iota(jnp.int32, (BLOCK, 2 * BLOCK), 0)
    kj = lax.broadcasted_iota(jnp.int32, (BLOCK, 2 * BLOCK), 1)
    dist = qi + BLOCK - kj
    ok = (dist >= 0) & (dist < WINDOW)
    distf = dist.astype(F32)
    for hd in range(ATT_HEADS):
        slope = 2.0 ** (-8.0 * (hd + 1) / ATT_HEADS)
        band = jnp.where(ok, (-slope * LOG2E) * distf, -jnp.inf)
        abias[hd * BLOCK:(hd + 1) * BLOCK, :] = jnp.where(kj == SINK_KEY, sink_ref[hd] * LOG2E, band)


class _InProj:
    def __init__(self, x_ref, wa_ref, wb_ref, wg_ref, h_scr, proj):
        self.x_ref, self.h_scr, self.proj = x_ref, h_scr, proj
        self.todo = []
        rows = x_ref.shape[0]
        for w_ref, base in ((wa_ref, 0), (wb_ref, O_DO), (wg_ref, O_IF)):
            n = w_ref.shape[0]
            for c0 in range(0, n, MXU_WIDTH):
                for r0 in range(0, rows, INPROJ_ROWS):
                    self.todo.append((w_ref, base, c0, min(c0 + MXU_WIDTH, n), r0, min(r0 + INPROJ_ROWS, rows)))

    def normalise(self):
        self.h_scr[...] = _rms_scale(self.x_ref[...]).astype(BF16)

    def step(self):
        if self.todo:
            w_ref, base, c0, c1, r0, r1 = self.todo.pop(0)
            self.proj[r0:r1, base + c0:base + c1] = lax.dot_general(
                self.h_scr[r0:r1, :], w_ref[c0:c1, :], (((1,), (1,)), ((), ())), preferred_element_type=F32)

    def finish(self):
        while self.todo:
            self.step()


def _odd_mixers(first, kv0, proj, x_ref, o_ref, bias_ref, wout_ref, fg_ref,
                kvbuf, abias, c_scr, m_scr, y_scr, nxt, after=None):
    T = x_ref.shape[0]
    NB = T // BLOCK
    H = MLSTM_HEADS
    L = min(MCHUNK, T)
    NEG = -jnp.inf

    lane_q = lax.broadcasted_iota(jnp.int32, (BLOCK, LANES), 1)
    low = lane_q < ATT_HEAD_DIM
    low_t = lax.broadcasted_iota(jnp.int32, (T, LANES), 1) < ATT_HEAD_DIM
    for b, off in enumerate((O_CK, O_CV)):
        kv = proj[:, off:off + ATT_KV_WIDTH]
        sw = pltpu.roll(kv, ATT_HEAD_DIM, 1)
        kvbuf[2 * b, kv0 + BLOCK:kv0 + BLOCK + T, :] = jnp.where(low_t, kv, sw).astype(BF16)
        kvbuf[2 * b + 1, kv0 + BLOCK:kv0 + BLOCK + T, :] = jnp.where(low_t, sw, kv).astype(BF16)
    if first is not False:
        key_lane = lax.broadcasted_iota(jnp.int32, (BLOCK, 2 * BLOCK), 1)
        no_prev = jnp.where((key_lane < BLOCK) & (key_lane != SINK_KEY), jnp.where(first, NEG, 0.0), 0.0)
    sink_row = lax.broadcasted_iota(jnp.int32, (BF16_ROWS, LANES), 0) == SINK_KEY

    def drop_sink_row(a):
        head = a[0:BF16_ROWS, :]
        return jnp.concatenate([jnp.where(sink_row, jnp.zeros_like(head), head), a[BF16_ROWS:, :]], axis=0)

    ones_v = jnp.ones((2 * BLOCK, LANES), BF16)

    def scores(j, g):
        r0 = j * BLOCK
        kk = drop_sink_row(kvbuf[g, kv0 + r0:kv0 + r0 + 2 * BLOCK, :])
        qs = []
        for p in range(g * ATT_GROUP // 2, (g + 1) * ATT_GROUP // 2):
            qp = proj[r0:r0 + BLOCK, O_CQ + p * LANES:O_CQ + (p + 1) * LANES]
            qs.append(jnp.where(low, qp, 0.0).astype(BF16))
            qs.append(jnp.where(low, 0.0, qp).astype(BF16))
        return lax.dot_general(jnp.concatenate(qs, axis=0), kk, (((1,), (1,)), ((), ())),
                               preferred_element_type=F32)

    def attention():
        slots = [(j, g) for j in range(NB) for g in range(ATT_KV_HEADS)]
        s_next = scores(*slots[0])
        for n, (j, g) in enumerate(slots):
            r0 = j * BLOCK
            s_all = s_next
            if n + 1 < len(slots):
                s_next = scores(*slots[n + 1])
            if n % ATTN_SKIP_EVERY != ATTN_SKIP_EVERY - 1:
                nxt.step()
            ps = []
            for r in range(ATT_GROUP):
                hd = g * ATT_GROUP + r
                s = s_all[r * BLOCK:(r + 1) * BLOCK, :] + abias[hd * BLOCK:(hd + 1) * BLOCK, :]
                if j == 0 and first is not False:
                    s = s + no_prev
                ps.append(jnp.exp2(s - jnp.max(s, axis=1, keepdims=True)).astype(BF16))
            vv = jnp.concatenate(
                [drop_sink_row(kvbuf[ATT_KV_HEADS + g, kv0 + r0:kv0 + r0 + 2 * BLOCK, :]), ones_v], axis=1)
            o_all = jnp.dot(jnp.concatenate(ps, axis=0), vv, preferred_element_type=F32)
            for pp in range(ATT_GROUP // 2):
                p = g * ATT_GROUP // 2 + pp
                lo = o_all[2 * pp * BLOCK:(2 * pp + 1) * BLOCK, :]
                hi = o_all[(2 * pp + 1) * BLOCK:(2 * pp + 2) * BLOCK, :]
                att = jnp.where(low, lo[:, :LANES] / lo[:, LANES:], hi[:, :LANES] / hi[:, LANES:])
                gate = proj[r0:r0 + BLOCK, O_GC + p * LANES:O_GC + (p + 1) * LANES]
                y_scr[r0:r0 + BLOCK, p * LANES:(p + 1) * LANES] = (att * jax.nn.silu(gate)).astype(BF16)

    gt = proj[:, O_IF:O_IF + GATE_PAD].T[0:2 * H, :]
    pre = gt + jnp.concatenate([bias_ref[...]] * (T // LANES), axis=1)
    row = lax.broadcasted_iota(jnp.int32, (2 * H, T), 0)
    top = row < H
    bcum = _lane_scan(jax.nn.log_sigmoid(pre), jnp.add, 0.0)
    bsw = pltpu.roll(bcum, H, 0)
    a_top = pre - bsw
    ag = jnp.where(top, a_top, pltpu.roll(a_top, H, 0))
    bf = jnp.where(top, bsw, bcum)
    m_in = jnp.concatenate([m_scr[...]] * (T // LANES), axis=1)
    mg = jnp.maximum(m_in, _lane_scan(ag, jnp.maximum, NEG))
    m_t = bf + mg
    nrm = jnp.exp(-m_t)
    inter_parts, wk_parts, decay_parts = [], [], []
    for c in range(T // L):
        m_start = m_in[:, 0:1] if c == 0 else mg[:, c * L - 1:c * L]
        m_last = mg[:, (c + 1) * L - 1:(c + 1) * L]
        inter_parts.append(jnp.exp(m_start - mg[:, c * L:(c + 1) * L]))
        wk_parts.append(jnp.exp(ag[:, c * L:(c + 1) * L] - m_last))
        decay_parts.append(jnp.broadcast_to(jnp.exp(m_start - m_last), (2 * H, 2 * MLSTM_V_DIM)))
    inter = jnp.concatenate(inter_parts, axis=1)
    m_scr[...] = jnp.broadcast_to(m_t[:, T - 1:T], m_scr.shape)
    ag2 = ag * LOG2E
    cols = jnp.concatenate([mg * LOG2E, inter, nrm, jnp.zeros((LANES - 6 * H, T), F32)], axis=0).T

    kT = proj[:, O_DK:O_DK + MLSTM_QK_WIDTH].T
    lane_p = lax.broadcasted_iota(jnp.int32, (L, LANES), 1)
    low_p = lane_p < MLSTM_QK_DIM
    tri = (lax.broadcasted_iota(jnp.int32, (BLOCK, BLOCK), 1) <= lax.broadcasted_iota(jnp.int32, (BLOCK, BLOCK), 0))
    ones = jnp.ones((L, MLSTM_V_DIM), BF16)
    zero_blk = jnp.zeros((BLOCK, BLOCK), BF16)

    def mlstm_chunk(c):
        r0 = c * L
        for pair in range(H // 2):
            qp = proj[r0:r0 + L, O_DQ + pair * LANES:O_DQ + (pair + 1) * LANES]
            kT_pair = kT[pair * LANES:(pair + 1) * LANES, r0:r0 + L].astype(BF16)
            c_pair = c_scr[pair * LANES:(pair + 1) * LANES, :].astype(BF16)
            for half in range(2):
                hd = 2 * pair + half
                q_h = jnp.where(low_p, qp, 0.0) if half == 0 else jnp.where(low_p, 0.0, qp)
                s = jnp.dot(q_h.astype(BF16), kT_pair, preferred_element_type=F32)
                if H * c + hd < MLSTM_CHUNKS:
                    nxt.step()
                p_rows = []
                for bi in range(L // BLOCK):
                    m_col = cols[r0 + bi * BLOCK:r0 + (bi + 1) * BLOCK, hd:hd + 1]
                    blks = []
                    for bj in range(L // BLOCK):
                        if bj > bi:
                            blks.append(zero_blk)
                            continue
                        arg = ag2[hd:hd + 1, r0 + bj * BLOCK:r0 + (bj + 1) * BLOCK] - m_col
                        if bj == bi:
                            arg = jnp.where(tri, arg, NEG)
                        sb = s[bi * BLOCK:(bi + 1) * BLOCK, bj * BLOCK:(bj + 1) * BLOCK]
                        blks.append((sb * jnp.exp2(arg)).astype(BF16))
                    p_rows.append(jnp.concatenate(blks, axis=1))
                p_bf = jnp.concatenate(p_rows, axis=0)
                v_aug = jnp.concatenate(
                    [proj[r0:r0 + L, O_DV + hd * MLSTM_V_DIM:O_DV + (hd + 1) * MLSTM_V_DIM].astype(BF16), ones],
                    axis=1)
                q_int = (q_h * cols[r0:r0 + L, 2 * H + hd:2 * H + hd + 1]).astype(BF16)
                num = jnp.dot(jnp.concatenate([p_bf, q_int], axis=1),
                              jnp.concatenate([v_aug, c_pair], axis=0), preferred_element_type=F32)
                den = jnp.maximum(jnp.abs(num[:, MLSTM_V_DIM:]), cols[r0:r0 + L, 4 * H + hd:4 * H + hd + 1])
                hm = num[:, :MLSTM_V_DIM] / den
                og = proj[r0:r0 + L, O_DO + hd * MLSTM_V_DIM:O_DO + (hd + 1) * MLSTM_V_DIM]
                gd = proj[r0:r0 + L, O_GD + hd * MLSTM_V_DIM:O_GD + (hd + 1) * MLSTM_V_DIM]
                y_scr[r0:r0 + L, ATT_WIDTH + hd * MLSTM_V_DIM:ATT_WIDTH + (hd + 1) * MLSTM_V_DIM] = (
                    hm * jax.nn.sigmoid(og) * jax.nn.silu(gd)).astype(BF16)
                kw = (kT[hd * MLSTM_QK_DIM:(hd + 1) * MLSTM_QK_DIM, r0:r0 + L]
                      * wk_parts[c][hd:hd + 1, :]).astype(BF16)
                upd = jnp.dot(kw, v_aug, preferred_element_type=F32)
                rows = slice(hd * MLSTM_QK_DIM, (hd + 1) * MLSTM_QK_DIM)
                c_scr[rows, :] = decay_parts[c][hd:hd + 1, :] * c_scr[rows, :] + upd

    attention()
    for _ in range(PRE_CHUNKS):
        nxt.step()
    for c in range(T // L):
        mlstm_chunk(c)
    nxt.finish()
    if after is not None:
        after.normalise()
    for r0 in range(0, T, OUT_ROWS):
        x2 = x_ref[r0:r0 + OUT_ROWS, :] + jnp.dot(y_scr[r0:r0 + OUT_ROWS, :], wout_ref[...],
                                                   preferred_element_type=F32)
        o_ref[r0:r0 + OUT_ROWS, :] = _rmsnorm(x2, fg_ref[...])


def _odd_kernel(sink_ref, xa_ref, xb_ref, xn_ref, wa_ref, wb_ref, wg_ref, bias_ref, wout_ref,
                fg_ref, o_ref, proj_a, proj_b, h_a, h_b, kvbuf, abias, c_scr, y_a, y_b, m_scr):
    s = pl.program_id(0)
    T = xa_ref.shape[0]
    w = (wa_ref, wb_ref, wg_ref)
    rest = (bias_ref, wout_ref, fg_ref, kvbuf, abias, c_scr, m_scr)

    @pl.when(s == 0)
    def _():
        _odd_init(sink_ref, kvbuf, abias, c_scr, m_scr)
        first = _InProj(xa_ref, *w, h_a, proj_a)
        first.normalise()
        first.finish()

    nxt = _InProj(xb_ref, *w, h_a, proj_b)
    nxt.normalise()
    after = _InProj(xn_ref, *w, h_b, proj_a)
    _odd_mixers(s == 0, 0, proj_a, xa_ref, o_ref.at[0:T], *rest, y_a, nxt, after)
    _odd_mixers(False, T, proj_b, xb_ref, o_ref.at[T:2 * T], *rest, y_b, after)
    for b in range(2 * ATT_KV_HEADS):
        kvbuf[b, 0:BLOCK, :] = kvbuf[b, 2 * T:2 * T + BLOCK, :]


def _odd_layer(x, wa, wb, wg, i_bias, f_bias, sinks, wo, final_g):
    S = x.shape[0]
    T = min(TILE_ODD, S // 2)
    NT = S // T
    assert S % (2 * T) == 0 and T % BLOCK == 0 and T % min(MCHUNK, T) == 0
    bias = jnp.broadcast_to(jnp.concatenate([i_bias, f_bias]).astype(F32)[:, None], (2 * MLSTM_HEADS, LANES))
    const = _const_spec
    return pl.pallas_call(
        _odd_kernel,
        grid=(NT // 2,),
        in_specs=[
            pl.BlockSpec(memory_space=pltpu.SMEM),
            pl.BlockSpec((T, D_MODEL), lambda s: (2 * s, 0)),
            pl.BlockSpec((T, D_MODEL), lambda s: (2 * s + 1, 0)),
            pl.BlockSpec((T, D_MODEL), lambda s: (jnp.minimum(2 * s + 2, NT - 1), 0)),
            const(O_DO, D_MODEL),
            const(O_IF - O_DO, D_MODEL),
            const(GATE_PAD, D_MODEL),
            const(2 * MLSTM_HEADS, LANES),
            const(ODD_OUT, D_MODEL),
            const(1, D_MODEL),
        ],
        out_specs=pl.BlockSpec((2 * T, D_MODEL), lambda s: (s, 0)),
        out_shape=jax.ShapeDtypeStruct((S, D_MODEL), F32),
        scratch_shapes=[
            pltpu.VMEM((T, ODD_IN_PAD), F32),
            pltpu.VMEM((T, ODD_IN_PAD), F32),
            pltpu.VMEM((T, D_MODEL), BF16),
            pltpu.VMEM((T, D_MODEL), BF16),
            pltpu.VMEM((2 * ATT_KV_HEADS, 2 * T + BLOCK, LANES), BF16),
            pltpu.VMEM((ATT_HEADS * BLOCK, 2 * BLOCK), F32),
            pltpu.VMEM((MLSTM_HEADS * MLSTM_QK_DIM, 2 * MLSTM_V_DIM), F32),
            pltpu.VMEM((T, ODD_OUT), BF16),
            pltpu.VMEM((T, ODD_OUT), BF16),
            pltpu.VMEM((2 * MLSTM_HEADS, LANES), F32),
        ],
        compiler_params=pltpu.CompilerParams(
            dimension_semantics=("arbitrary",), vmem_limit_bytes=VMEM_LIMIT),
        name="odd_layer",
    )(sinks.astype(F32), x, x, x, wa, wb, wg, bias, wo,
      final_g.reshape(1, D_MODEL))


def kernel(x, even_norm, even_w_in, even_conv_w, even_pool_w, even_pool_scale, even_w_out,
           odd_norm, odd_w_in, odd_i_bias, odd_f_bias, odd_sinks, odd_w_out, final_norm):
    B, S, D = x.shape
    assert B == 1 and D == D_MODEL
    assert even_norm.shape[0] == 1 and odd_norm.shape[0] == 1
    h = x.reshape(S, D)
    h, wa, wb, wg, wo = _even_layer(h, even_norm[0], even_w_in[0], even_conv_w[0], even_pool_w[0],
                                    even_pool_scale[0], even_w_out[0], odd_norm[0], odd_w_in[0].T, odd_w_out[0])
    h = _odd_layer(h, wa, wb, wg, odd_i_bias[0], odd_f_bias[0], odd_sinks[0], wo, final_norm)
    return h.reshape(B, S, D)
```

```python
import functools

import jax
import jax.numpy as jnp
from jax import lax
from jax.experimental import pallas as pl
from jax.experimental.pallas import tpu as pltpu

F32 = jnp.float32
BF16 = jnp.bfloat16

D_MODEL = 1024
RMS_EPS = 1e-6

CONV_WIDTH = 512
CONV_K = 3
POOL_WINDOWS = (2, 4, 8, 16)
POOL_WIDTH = 512
POOL_GROUP = POOL_WIDTH // len(POOL_WINDOWS)
EVEN_IN = 4 * CONV_WIDTH + 2 * POOL_WIDTH
EVEN_OUT = CONV_WIDTH + POOL_WIDTH
CONV_HALO = 8
POOL_HALO = 16

ATT_HEADS = 8
ATT_KV_HEADS = 2
ATT_GROUP = ATT_HEADS // ATT_KV_HEADS
ATT_HEAD_DIM = 64
ATT_WIDTH = ATT_HEADS * ATT_HEAD_DIM
ATT_KV_WIDTH = ATT_KV_HEADS * ATT_HEAD_DIM
WINDOW = 128
BLOCK = 128
MLSTM_HEADS = 4
MLSTM_QK_DIM = 64
MLSTM_V_DIM = 128
MLSTM_QK_WIDTH = MLSTM_HEADS * MLSTM_QK_DIM
MLSTM_WIDTH = MLSTM_HEADS * MLSTM_V_DIM
ODD_OUT = ATT_WIDTH + MLSTM_WIDTH
LANES = 128
BF16_ROWS = 16
SINK_KEY = 0
MXU_WIDTH = 256
GATE_PAD = LANES

O_CQ = 0
O_CK = O_CQ + ATT_WIDTH
O_CV = O_CK + ATT_KV_WIDTH
O_GC = O_CV + ATT_KV_WIDTH
O_DQ = O_GC + ATT_WIDTH
O_DK = O_DQ + MLSTM_QK_WIDTH
O_DV = O_DK + MLSTM_QK_WIDTH
O_DO = O_DV + MLSTM_WIDTH
O_GD = O_DO + MLSTM_WIDTH
O_IF = O_GD + MLSTM_WIDTH
ODD_IN_PAD = O_IF + GATE_PAD

TILE_EVEN = 1024
EVEN_CHUNK = 512
TILE_ODD = 512
MCHUNK = 256
INPROJ_ROWS = 512
OUT_ROWS = 256
PRE_CHUNKS = 3
MLSTM_CHUNKS = 5
ATTN_SKIP_EVERY = 4
LOG2E = 1.4426950408889634
ATT_Q_SCALE = ATT_HEAD_DIM ** -0.5 * LOG2E
MLSTM_Q_SCALE = MLSTM_QK_DIM ** -0.5
VMEM_LIMIT = 60 * 1024 * 1024


def _const_spec(*shape):
    return pl.BlockSpec(shape, lambda i: (0,) * len(shape), pipeline_mode=pl.Buffered(1))


def _rms_scale(x):
    ms = jnp.mean(x * x, axis=-1, keepdims=True)
    return x * lax.rsqrt(ms + RMS_EPS)


def _rmsnorm(x, g):
    return _rms_scale(x) * g


def _even_kernel(x_ref, g_ref, win_ref, cw_ref, pw_ref, ps_ref, wout_ref,
                 ng_ref, nwa32_ref, nwb32_ref, nwg32_ref, nwo32_ref,
                 o_ref, nwa_ref, nwb_ref, nwg_ref, nwo_ref, proj_scr, uc_scr, up_scr, y_scr, wfold_scr):
    i = pl.program_id(0)
    T = x_ref.shape[0]
    C = CONV_WIDTH

    ng = ng_ref[...]
    ra = nwa32_ref.shape[0]
    feat = lax.broadcasted_iota(jnp.int32, (ra, D_MODEL), 0) + i * ra
    q_scale = jnp.where(feat < O_CK, ATT_Q_SCALE,
                        jnp.where(feat < O_DQ, 1.0, jnp.where(feat < O_DK, MLSTM_Q_SCALE, 1.0)))
    nwa_ref[...] = (nwa32_ref[...] * ng * q_scale).astype(BF16)
    nwb_ref[...] = (nwb32_ref[...] * ng).astype(BF16)
    nwo_ref[...] = nwo32_ref[...].astype(BF16)
    nwg_ref[...] = jnp.concatenate(
        [nwg32_ref[...] * ng, jnp.zeros((GATE_PAD - 2 * MLSTM_HEADS, D_MODEL), F32)], axis=0).astype(BF16)

    @pl.when(i == 0)
    def _():
        uc_scr[0:CONV_HALO, :] = jnp.zeros((CONV_HALO, C), F32)
        up_scr[0:POOL_HALO, :] = jnp.zeros((POOL_HALO, POOL_WIDTH), F32)
        for g in range(len(POOL_WINDOWS)):
            lo, hi = g * POOL_GROUP, (g + 1) * POOL_GROUP
            wfold_scr[:, lo:hi] = jnp.dot(win_ref[:, 4 * C + lo:4 * C + hi], pw_ref[g],
                                          precision=lax.Precision.HIGHEST, preferred_element_type=F32)

    h = _rmsnorm(x_ref[...], g_ref[...]).astype(BF16)
    for c0 in range(0, EVEN_IN, EVEN_CHUNK):
        w = wfold_scr[...] if c0 == 4 * C else win_ref[:, c0:c0 + EVEN_CHUNK]
        proj_scr[:, c0:c0 + EVEN_CHUNK] = jnp.dot(h, w.astype(BF16), preferred_element_type=F32)

    uc = proj_scr[:, 2 * C:3 * C] * proj_scr[:, 0:C]
    uc_scr[CONV_HALO:CONV_HALO + T, :] = uc
    conv = (cw_ref[2:3, :] * uc
            + cw_ref[1:2, :] * uc_scr[CONV_HALO - 1:CONV_HALO - 1 + T, :]
            + cw_ref[0:1, :] * uc_scr[CONV_HALO - 2:CONV_HALO - 2 + T, :])
    ya = proj_scr[:, C:2 * C] * conv * jax.nn.silu(proj_scr[:, 3 * C:4 * C])
    y_scr[:, 0:C] = ya.astype(BF16)
    uc_scr[0:CONV_HALO, :] = uc_scr[T:T + CONV_HALO, :]

    up_scr[POOL_HALO:POOL_HALO + T, :] = proj_scr[:, 4 * C:4 * C + POOL_WIDTH]
    t1 = (lax.broadcasted_iota(jnp.int32, (T, POOL_GROUP), 0) + (i * T + 1)).astype(F32)
    for g, w in enumerate(POOL_WINDOWS):
        lo, hi = g * POOL_GROUP, (g + 1) * POOL_GROUP
        s = up_scr[:, lo:hi]
        k = 1
        while k < w:
            s = s + pltpu.roll(s, k, 0)
            k *= 2
        u = up_scr[POOL_HALO:POOL_HALO + T, lo:hi]
        yb = s[POOL_HALO:, :] / jnp.minimum(t1, float(w)) - u
        gate = proj_scr[:, 4 * C + POOL_WIDTH + lo:4 * C + POOL_WIDTH + hi]
        y_scr[:, C + lo:C + hi] = (yb * ps_ref[:, lo:hi] * jax.nn.silu(gate)).astype(BF16)
    up_scr[0:POOL_HALO, :] = up_scr[T:T + POOL_HALO, :]

    o_ref[...] = x_ref[...] + jnp.dot(y_scr[...], wout_ref[...].astype(BF16), preferred_element_type=F32)


def _even_layer(x, norm_g, w_in, conv_w, pool_w, pool_scale, w_out, next_norm_g, next_w_in_t, next_w_out):
    S = x.shape[0]
    T = min(TILE_EVEN, S)
    steps = S // T
    n_gates = 2 * MLSTM_HEADS
    assert S % T == 0
    assert next_w_in_t.shape == (O_IF + n_gates, D_MODEL) and next_w_out.shape == (ODD_OUT, D_MODEL)
    ra, rb, ro = O_DO // steps, (O_IF - O_DO) // steps, ODD_OUT // steps
    assert ra * steps == O_DO and rb * steps == O_IF - O_DO and ro * steps == ODD_OUT
    const = _const_spec
    slab = lambda r: pl.BlockSpec((r, D_MODEL), lambda i: (i, 0))
    return pl.pallas_call(
        _even_kernel,
        grid=(steps,),
        in_specs=[
            pl.BlockSpec((T, D_MODEL), lambda i: (i, 0)),
            const(1, D_MODEL),
            const(D_MODEL, EVEN_IN),
            const(CONV_K, CONV_WIDTH),
            const(len(POOL_WINDOWS), POOL_GROUP, POOL_GROUP),
            const(1, POOL_WIDTH),
            const(EVEN_OUT, D_MODEL),
            const(1, D_MODEL),
            slab(ra),
            pl.BlockSpec((pl.Element(rb), pl.Element(D_MODEL)), lambda i: (pl.multiple_of(O_DO + n_gates + i * rb, n_gates), 0)),
            pl.BlockSpec((n_gates, D_MODEL), lambda i: (O_DO // n_gates, 0)),
            slab(ro),
        ],
        out_specs=[pl.BlockSpec((T, D_MODEL), lambda i: (i, 0)),
                   slab(ra), slab(rb), pl.BlockSpec((GATE_PAD, D_MODEL), lambda i: (0, 0)), slab(ro)],
        out_shape=[jax.ShapeDtypeStruct((S, D_MODEL), F32),
                   jax.ShapeDtypeStruct((O_DO, D_MODEL), BF16),
                   jax.ShapeDtypeStruct((O_IF - O_DO, D_MODEL), BF16),
                   jax.ShapeDtypeStruct((GATE_PAD, D_MODEL), BF16),
                   jax.ShapeDtypeStruct((ODD_OUT, D_MODEL), BF16)],
        scratch_shapes=[
            pltpu.VMEM((T, EVEN_IN), F32),
            pltpu.VMEM((T + CONV_HALO, CONV_WIDTH), F32),
            pltpu.VMEM((T + POOL_HALO, POOL_WIDTH), F32),
            pltpu.VMEM((T, EVEN_OUT), BF16),
            pltpu.VMEM((D_MODEL, POOL_WIDTH), F32),
        ],
        compiler_params=pltpu.CompilerParams(
            dimension_semantics=("arbitrary",), vmem_limit_bytes=VMEM_LIMIT),
        name="even_layer",
    )(x, norm_g.reshape(1, D_MODEL), w_in, conv_w, pool_w, pool_scale.reshape(1, POOL_WIDTH), w_out,
      next_norm_g.reshape(1, D_MODEL), next_w_in_t, next_w_in_t, next_w_in_t, next_w_out)


def _lane_scan(x, op, identity):
    n = x.shape[1]
    lane = lax.broadcasted_iota(jnp.int32, x.shape, 1)
    k = 1
    while k < n:
        x = op(x, jnp.where(lane >= k, pltpu.roll(x, k, 1), identity))
        k *= 2
    return x


def _odd_init(sink_ref, kvbuf, abias, c_scr, m_scr):
    for b in range(2 * ATT_KV_HEADS):
        kvbuf[b, 0:BLOCK, :] = jnp.zeros((BLOCK, LANES), BF16)
    c_scr[...] = jnp.zeros(c_scr.shape, F32)
    m_scr[...] = jnp.zeros(m_scr.shape, F32)
    qi = lax.broadcasted_iota(jnp.int32, (BLOCK, 2 * BLOCK), 0)
    kj = lax.broadcasted_iota(jnp.int32, (BLOCK, 2 * BLOCK), 1)
    dist = qi + BLOCK - kj
    ok = (dist >= 0) & (dist < WINDOW)
    distf = dist.astype(F32)
    for hd in range(ATT_HEADS):
        slope = 2.0 ** (-8.0 * (hd + 1) / ATT_HEADS)
        band = jnp.where(ok, (-slope * LOG2E) * distf, -jnp.inf)
        abias[hd * BLOCK:(hd + 1) * BLOCK, :] = jnp.where(kj == SINK_KEY, sink_ref[hd] * LOG2E, band)


class _InProj:
    def __init__(self, x_ref, wa_ref, wb_ref, wg_ref, h_scr, proj):
        self.x_ref, self.h_scr, self.proj = x_ref, h_scr, proj
        self.todo = []
        rows = x_ref.shape[0]
        for w_ref, base in ((wa_ref, 0), (wb_ref, O_DO), (wg_ref, O_IF)):
            n = w_ref.shape[0]
            for c0 in range(0, n, MXU_WIDTH):
                for r0 in range(0, rows, INPROJ_ROWS):
                    self.todo.append(functools.partial(
                        self.chunk, w_ref, base, c0, min(c0 + MXU_WIDTH, n), r0, min(r0 + INPROJ_ROWS, rows)))

    def normalise(self):
        self.h_scr[...] = _rms_scale(self.x_ref[...]).astype(BF16)

    def chunk(self, w_ref, base, c0, c1, r0, r1):
        self.proj[r0:r1, base + c0:base + c1] = lax.dot_general(
            self.h_scr[r0:r1, :], w_ref[c0:c1, :], (((1,), (1,)), ((), ())), preferred_element_type=F32)

    def step(self):
        if self.todo:
            self.todo.pop(0)()

    def finish(self):
        while self.todo:
            self.step()


def _odd_mixers(first, kv0, proj, x_ref, o_ref, bias_ref, wout_ref, fg_ref,
                kvbuf, abias, c_scr, m_scr, y_scr, nxt, after=None, defer_out=False):
    T = x_ref.shape[0]
    NB = T // BLOCK
    H = MLSTM_HEADS
    L = min(MCHUNK, T)
    NEG = -jnp.inf

    lane_q = lax.broadcasted_iota(jnp.int32, (BLOCK, LANES), 1)
    low = lane_q < ATT_HEAD_DIM
    low_t = lax.broadcasted_iota(jnp.int32, (T, LANES), 1) < ATT_HEAD_DIM
    for b, off in enumerate((O_CK, O_CV)):
        kv = proj[:, off:off + ATT_KV_WIDTH]
        sw = pltpu.roll(kv, ATT_HEAD_DIM, 1)
        kvbuf[2 * b, kv0 + BLOCK:kv0 + BLOCK + T, :] = jnp.where(low_t, kv, sw).astype(BF16)
        kvbuf[2 * b + 1, kv0 + BLOCK:kv0 + BLOCK + T, :] = jnp.where(low_t, sw, kv).astype(BF16)
    if first is not False:
        key_lane = lax.broadcasted_iota(jnp.int32, (BLOCK, 2 * BLOCK), 1)
        no_prev = jnp.where((key_lane < BLOCK) & (key_lane != SINK_KEY), jnp.where(first, NEG, 0.0), 0.0)
    sink_row = lax.broadcasted_iota(jnp.int32, (BF16_ROWS, LANES), 0) == SINK_KEY

    def drop_sink_row(a):
        head = a[0:BF16_ROWS, :]
        return jnp.concatenate([jnp.where(sink_row, jnp.zeros_like(head), head), a[BF16_ROWS:, :]], axis=0)

    ones_v = jnp.ones((2 * BLOCK, LANES), BF16)

    def scores(j, g):
        r0 = j * BLOCK
        kk = drop_sink_row(kvbuf[g, kv0 + r0:kv0 + r0 + 2 * BLOCK, :])
        qs = []
        for p in range(g * ATT_GROUP // 2, (g + 1) * ATT_GROUP // 2):
            qp = proj[r0:r0 + BLOCK, O_CQ + p * LANES:O_CQ + (p + 1) * LANES]
            qs.append(jnp.where(low, qp, 0.0).astype(BF16))
            qs.append(jnp.where(low, 0.0, qp).astype(BF16))
        return lax.dot_general(jnp.concatenate(qs, axis=0), kk, (((1,), (1,)), ((), ())),
                               preferred_element_type=F32)

    def attention():
        slots = [(j, g) for j in range(NB) for g in range(ATT_KV_HEADS)]
        s_next = scores(*slots[0])
        for n, (j, g) in enumerate(slots):
            r0 = j * BLOCK
            s_all = s_next
            if n + 1 < len(slots):
                s_next = scores(*slots[n + 1])
            if n % ATTN_SKIP_EVERY != ATTN_SKIP_EVERY - 1:
                nxt.step()
            ps = []
            for r in range(ATT_GROUP):
                hd = g * ATT_GROUP + r
                s = s_all[r * BLOCK:(r + 1) * BLOCK, :] + abias[hd * BLOCK:(hd + 1) * BLOCK, :]
                if j == 0 and first is not False:
                    s = s + no_prev
                ps.append(jnp.exp2(s - jnp.max(s, axis=1, keepdims=True)).astype(BF16))
            vv = jnp.concatenate(
                [drop_sink_row(kvbuf[ATT_KV_HEADS + g, kv0 + r0:kv0 + r0 + 2 * BLOCK, :]), ones_v], axis=1)
            o_all = jnp.dot(jnp.concatenate(ps, axis=0), vv, preferred_element_type=F32)
            for pp in range(ATT_GROUP // 2):
                p = g * ATT_GROUP // 2 + pp
                lo = o_all[2 * pp * BLOCK:(2 * pp + 1) * BLOCK, :]
                hi = o_all[(2 * pp + 1) * BLOCK:(2 * pp + 2) * BLOCK, :]
                att = jnp.where(low, lo[:, :LANES] / lo[:, LANES:], hi[:, :LANES] / hi[:, LANES:])
                gate = proj[r0:r0 + BLOCK, O_GC + p * LANES:O_GC + (p + 1) * LANES]
                y_scr[r0:r0 + BLOCK, p * LANES:(p + 1) * LANES] = (att * jax.nn.silu(gate)).astype(BF16)

    gt = proj[:, O_IF:O_IF + GATE_PAD].T[0:2 * H, :]
    pre = gt + jnp.concatenate([bias_ref[...]] * (T // LANES), axis=1)
    row = lax.broadcasted_iota(jnp.int32, (2 * H, T), 0)
    top = row < H
    bcum = _lane_scan(jax.nn.log_sigmoid(pre), jnp.add, 0.0)
    bsw = pltpu.roll(bcum, H, 0)
    a_top = pre - bsw
    ag = jnp.where(top, a_top, pltpu.roll(a_top, H, 0))
    bf = jnp.where(top, bsw, bcum)
    m_in = jnp.concatenate([m_scr[...]] * (T // LANES), axis=1)
    mg = jnp.maximum(m_in, _lane_scan(ag, jnp.maximum, NEG))
    m_t = bf + mg
    nrm = jnp.exp(-m_t)
    inter_parts, wk_parts, decay_parts = [], [], []
    for c in range(T // L):
        m_start = m_in[:, 0:1] if c == 0 else mg[:, c * L - 1:c * L]
        m_last = mg[:, (c + 1) * L - 1:(c + 1) * L]
        inter_parts.append(jnp.exp(m_start - mg[:, c * L:(c + 1) * L]))
        wk_parts.append(jnp.exp(ag[:, c * L:(c + 1) * L] - m_last))
        decay_parts.append(jnp.broadcast_to(jnp.exp(m_start - m_last), (2 * H, 2 * MLSTM_V_DIM)))
    inter = jnp.concatenate(inter_parts, axis=1)
    m_scr[...] = jnp.broadcast_to(m_t[:, T - 1:T], m_scr.shape)
    ag2 = ag * LOG2E
    cols = jnp.concatenate([mg * LOG2E, inter, nrm, jnp.zeros((LANES - 6 * H, T), F32)], axis=0).T

    kT = proj[:, O_DK:O_DK + MLSTM_QK_WIDTH].T
    lane_p = lax.broadcasted_iota(jnp.int32, (L, LANES), 1)
    low_p = lane_p < MLSTM_QK_DIM
    tri = (lax.broadcasted_iota(jnp.int32, (BLOCK, BLOCK), 1) <= lax.broadcasted_iota(jnp.int32, (BLOCK, BLOCK), 0))
    ones = jnp.ones((L, MLSTM_V_DIM), BF16)
    zero_blk = jnp.zeros((BLOCK, BLOCK), BF16)

    def mlstm_chunk(c):
        r0 = c * L
        for pair in range(H // 2):
            qp = proj[r0:r0 + L, O_DQ + pair * LANES:O_DQ + (pair + 1) * LANES]
            kT_pair = kT[pair * LANES:(pair + 1) * LANES, r0:r0 + L].astype(BF16)
            c_pair = c_scr[pair * LANES:(pair + 1) * LANES, :].astype(BF16)
            for half in range(2):
                hd = 2 * pair + half
                q_h = jnp.where(low_p, qp, 0.0) if half == 0 else jnp.where(low_p, 0.0, qp)
                s = jnp.dot(q_h.astype(BF16), kT_pair, preferred_element_type=F32)
                if H * c + hd < MLSTM_CHUNKS:
                    nxt.step()
                p_rows = []
                for bi in range(L // BLOCK):
                    m_col = cols[r0 + bi * BLOCK:r0 + (bi + 1) * BLOCK, hd:hd + 1]
                    blks = []
                    for bj in range(L // BLOCK):
                        if bj > bi:
                            blks.append(zero_blk)
                            continue
                        arg = ag2[hd:hd + 1, r0 + bj * BLOCK:r0 + (bj + 1) * BLOCK] - m_col
                        if bj == bi:
                            arg = jnp.where(tri, arg, NEG)
                        sb = s[bi * BLOCK:(bi + 1) * BLOCK, bj * BLOCK:(bj + 1) * BLOCK]
                        blks.append((sb * jnp.exp2(arg)).astype(BF16))
                    p_rows.append(jnp.concatenate(blks, axis=1))
                p_bf = jnp.concatenate(p_rows, axis=0)
                v_aug = jnp.concatenate(
                    [proj[r0:r0 + L, O_DV + hd * MLSTM_V_DIM:O_DV + (hd + 1) * MLSTM_V_DIM].astype(BF16), ones],
                    axis=1)
                q_int = (q_h * cols[r0:r0 + L, 2 * H + hd:2 * H + hd + 1]).astype(BF16)
                num = jnp.dot(jnp.concatenate([p_bf, q_int], axis=1),
                              jnp.concatenate([v_aug, c_pair], axis=0), preferred_element_type=F32)
                den = jnp.maximum(jnp.abs(num[:, MLSTM_V_DIM:]), cols[r0:r0 + L, 4 * H + hd:4 * H + hd + 1])
                hm = num[:, :MLSTM_V_DIM] / den
                og = proj[r0:r0 + L, O_DO + hd * MLSTM_V_DIM:O_DO + (hd + 1) * MLSTM_V_DIM]
                gd = proj[r0:r0 + L, O_GD + hd * MLSTM_V_DIM:O_GD + (hd + 1) * MLSTM_V_DIM]
                y_scr[r0:r0 + L, ATT_WIDTH + hd * MLSTM_V_DIM:ATT_WIDTH + (hd + 1) * MLSTM_V_DIM] = (
                    hm * jax.nn.sigmoid(og) * jax.nn.silu(gd)).astype(BF16)
                kw = (kT[hd * MLSTM_QK_DIM:(hd + 1) * MLSTM_QK_DIM, r0:r0 + L]
                      * wk_parts[c][hd:hd + 1, :]).astype(BF16)
                upd = jnp.dot(kw, v_aug, preferred_element_type=F32)
                rows = slice(hd * MLSTM_QK_DIM, (hd + 1) * MLSTM_QK_DIM)
                c_scr[rows, :] = decay_parts[c][hd:hd + 1, :] * c_scr[rows, :] + upd

    attention()
    for _ in range(PRE_CHUNKS):
        nxt.step()
    for c in range(T // L):
        mlstm_chunk(c)
    nxt.finish()
    if after is not None:
        after.normalise()
    def out_rows(r0):
        x2 = x_ref[r0:r0 + OUT_ROWS, :] + jnp.dot(y_scr[r0:r0 + OUT_ROWS, :], wout_ref[...],
                                                   preferred_element_type=F32)
        o_ref[r0:r0 + OUT_ROWS, :] = _rmsnorm(x2, fg_ref[...])

    out_chunks = [functools.partial(out_rows, r0) for r0 in range(0, T, OUT_ROWS)]
    if defer_out:
        return out_chunks
    for chunk in out_chunks:
        chunk()
    return []


def _odd_kernel(sink_ref, xa_ref, xb_ref, xn_ref, wa_ref, wb_ref, wg_ref, bias_ref, wout_ref,
                fg_ref, o_ref, proj_a, proj_b, h_a, h_b, kvbuf, abias, c_scr, m_scr, y_a, y_b):
    s = pl.program_id(0)
    T = xa_ref.shape[0]
    w = (wa_ref, wb_ref, wg_ref)
    rest = (bias_ref, wout_ref, fg_ref, kvbuf, abias, c_scr, m_scr)

    @pl.when(s == 0)
    def _():
        _odd_init(sink_ref, kvbuf, abias, c_scr, m_scr)
        first = _InProj(xa_ref, *w, h_a, proj_a)
        first.normalise()
        first.finish()

    nxt = _InProj(xb_ref, *w, h_a, proj_b)
    nxt.normalise()
    after = _InProj(xn_ref, *w, h_b, proj_a)
    out_a = _odd_mixers(s == 0, 0, proj_a, xa_ref, o_ref.at[0:T], *rest, y_a, nxt, after, defer_out=True)
    after.todo = out_a + after.todo
    _odd_mixers(False, T, proj_b, xb_ref, o_ref.at[T:2 * T], *rest, y_b, after)
    for b in range(2 * ATT_KV_HEADS):
        kvbuf[b, 0:BLOCK, :] = kvbuf[b, 2 * T:2 * T + BLOCK, :]


def _odd_layer(x, wa, wb, wg, i_bias, f_bias, sinks, wo, final_g):
    S = x.shape[0]
    T = min(TILE_ODD, S // 2)
    NT = S // T
    assert S % (2 * T) == 0 and T % BLOCK == 0 and T % min(MCHUNK, T) == 0
    bias = jnp.broadcast_to(jnp.concatenate([i_bias, f_bias]).astype(F32)[:, None], (2 * MLSTM_HEADS, LANES))
    const = _const_spec
    return pl.pallas_call(
        _odd_kernel,
        grid=(NT // 2,),
        in_specs=[
            pl.BlockSpec(memory_space=pltpu.SMEM),
            pl.BlockSpec((T, D_MODEL), lambda s: (2 * s, 0)),
            pl.BlockSpec((T, D_MODEL), lambda s: (2 * s + 1, 0)),
            pl.BlockSpec((T, D_MODEL), lambda s: (jnp.minimum(2 * s + 2, NT - 1), 0)),
            const(O_DO, D_MODEL),
            const(O_IF - O_DO, D_MODEL),
            const(GATE_PAD, D_MODEL),
            const(2 * MLSTM_HEADS, LANES),
            const(ODD_OUT, D_MODEL),
            const(1, D_MODEL),
        ],
        out_specs=pl.BlockSpec((2 * T, D_MODEL), lambda s: (s, 0)),
        out_shape=jax.ShapeDtypeStruct((S, D_MODEL), F32),
        scratch_shapes=[
            pltpu.VMEM((T, ODD_IN_PAD), F32),
            pltpu.VMEM((T, ODD_IN_PAD), F32),
            pltpu.VMEM((T, D_MODEL), BF16),
            pltpu.VMEM((T, D_MODEL), BF16),
            pltpu.VMEM((2 * ATT_KV_HEADS, 2 * T + BLOCK, LANES), BF16),
            pltpu.VMEM((ATT_HEADS * BLOCK, 2 * BLOCK), F32),
            pltpu.VMEM((MLSTM_HEADS * MLSTM_QK_DIM, 2 * MLSTM_V_DIM), F32),
            pltpu.VMEM((2 * MLSTM_HEADS, LANES), F32),
            pltpu.VMEM((T, ODD_OUT), BF16),
            pltpu.VMEM((T, ODD_OUT), BF16),
        ],
        compiler_params=pltpu.CompilerParams(
            dimension_semantics=("arbitrary",), vmem_limit_bytes=VMEM_LIMIT),
        name="odd_layer",
    )(sinks.astype(F32), x, x, x, wa, wb, wg, bias, wo,
      final_g.reshape(1, D_MODEL))


def kernel(x, even_norm, even_w_in, even_conv_w, even_pool_w, even_pool_scale, even_w_out,
           odd_norm, odd_w_in, odd_i_bias, odd_f_bias, odd_sinks, odd_w_out, final_norm):
    B, S, D = x.shape
    assert B == 1 and D == D_MODEL
    assert even_norm.shape[0] == 1 and odd_norm.shape[0] == 1
    h = x.reshape(S, D)
    h, wa, wb, wg, wo = _even_layer(h, even_norm[0], even_w_in[0], even_conv_w[0], even_pool_w[0],
                                    even_pool_scale[0], even_w_out[0], odd_norm[0], odd_w_in[0].T, odd_w_out[0])
    h = _odd_layer(h, wa, wb, wg, odd_i_bias[0], odd_f_bias[0], odd_sinks[0], wo, final_norm)
    return h.reshape(B, S, D)
```

```python
import jax
import jax.numpy as jnp
from jax import lax
from jax.experimental import pallas as pl
from jax.experimental.pallas import tpu as pltpu

F32 = jnp.float32
BF16 = jnp.bfloat16

D_MODEL = 1024
RMS_EPS = 1e-6

CONV_WIDTH = 512
CONV_K = 3
POOL_WINDOWS = (2, 4, 8, 16)
POOL_WIDTH = 512
POOL_GROUP = POOL_WIDTH // len(POOL_WINDOWS)
EVEN_IN = 4 * CONV_WIDTH + 2 * POOL_WIDTH
EVEN_OUT = CONV_WIDTH + POOL_WIDTH
CONV_HALO = 8
POOL_HALO = 16

ATT_HEADS = 8
ATT_KV_HEADS = 2
ATT_GROUP = ATT_HEADS // ATT_KV_HEADS
ATT_HEAD_DIM = 64
ATT_WIDTH = ATT_HEADS * ATT_HEAD_DIM
ATT_KV_WIDTH = ATT_KV_HEADS * ATT_HEAD_DIM
WINDOW = 128
BLOCK = 128
MLSTM_HEADS = 4
MLSTM_QK_DIM = 64
MLSTM_V_DIM = 128
MLSTM_QK_WIDTH = MLSTM_HEADS * MLSTM_QK_DIM
MLSTM_WIDTH = MLSTM_HEADS * MLSTM_V_DIM
ODD_OUT = ATT_WIDTH + MLSTM_WIDTH
LANES = 128
BF16_ROWS = 16
SINK_KEY = 0
MXU_WIDTH = 256
GATE_PAD = LANES

O_CQ = 0
O_CK = O_CQ + ATT_WIDTH
O_CV = O_CK + ATT_KV_WIDTH
O_GC = O_CV + ATT_KV_WIDTH
O_DQ = O_GC + ATT_WIDTH
O_DK = O_DQ + MLSTM_QK_WIDTH
O_DV = O_DK + MLSTM_QK_WIDTH
O_DO = O_DV + MLSTM_WIDTH
O_GD = O_DO + MLSTM_WIDTH
O_IF = O_GD + MLSTM_WIDTH
ODD_IN_PAD = O_IF + GATE_PAD

TILE_EVEN = 1024
EVEN_CHUNK = 512
TILE_ODD = 512
MCHUNK = 256
INPROJ_ROWS = 512
OUT_ROWS = 256
PRE_CHUNKS = 3
MLSTM_CHUNKS = 5
ATTN_SKIP_EVERY = 4
LOG2E = 1.4426950408889634
ATT_Q_SCALE = ATT_HEAD_DIM ** -0.5 * LOG2E
MLSTM_Q_SCALE = MLSTM_QK_DIM ** -0.5
VMEM_LIMIT = 60 * 1024 * 1024


def _const_spec(*shape):
    return pl.BlockSpec(shape, lambda i: (0,) * len(shape), pipeline_mode=pl.Buffered(1))


def _rms_scale(x):
    ms = jnp.mean(x * x, axis=-1, keepdims=True)
    return x * lax.rsqrt(ms + RMS_EPS)


def _rmsnorm(x, g):
    return _rms_scale(x) * g


def _even_kernel(x_ref, g_ref, win_ref, cw_ref, pw_ref, ps_ref, wout_ref,
                 ng_ref, nwa32_ref, nwb32_ref, nwg32_ref, nwo32_ref,
                 o_ref, nwa_ref, nwb_ref, nwg_ref, nwo_ref, proj_scr, uc_scr, up_scr, y_scr, wfold_scr):
    i = pl.program_id(0)
    T = x_ref.shape[0]
    C = CONV_WIDTH

    ng = ng_ref[...]
    ra = nwa32_ref.shape[0]
    feat = lax.broadcasted_iota(jnp.int32, (ra, D_MODEL), 0) + i * ra
    q_scale = jnp.where(feat < O_CK, ATT_Q_SCALE,
                        jnp.where(feat < O_DQ, 1.0, jnp.where(feat < O_DK, MLSTM_Q_SCALE, 1.0)))
    nwa_ref[...] = (nwa32_ref[...] * ng * q_scale).astype(BF16)
    nwb_ref[...] = (nwb32_ref[...] * ng).astype(BF16)
    nwo_ref[...] = nwo32_ref[...].astype(BF16)
    nwg_ref[...] = jnp.concatenate(
        [nwg32_ref[...] * ng, jnp.zeros((GATE_PAD - 2 * MLSTM_HEADS, D_MODEL), F32)], axis=0).astype(BF16)

    @pl.when(i == 0)
    def _():
        uc_scr[0:CONV_HALO, :] = jnp.zeros((CONV_HALO, C), F32)
        up_scr[0:POOL_HALO, :] = jnp.zeros((POOL_HALO, POOL_WIDTH), F32)
        for g in range(len(POOL_WINDOWS)):
            lo, hi = g * POOL_GROUP, (g + 1) * POOL_GROUP
            wfold_scr[:, lo:hi] = jnp.dot(win_ref[:, 4 * C + lo:4 * C + hi], pw_ref[g],
                                          precision=lax.Precision.HIGHEST, preferred_element_type=F32)

    h = _rmsnorm(x_ref[...], g_ref[...]).astype(BF16)
    for c0 in range(0, EVEN_IN, EVEN_CHUNK):
        w = wfold_scr[...] if c0 == 4 * C else win_ref[:, c0:c0 + EVEN_CHUNK]
        proj_scr[:, c0:c0 + EVEN_CHUNK] = jnp.dot(h, w.astype(BF16), preferred_element_type=F32)

    uc = proj_scr[:, 2 * C:3 * C] * proj_scr[:, 0:C]
    uc_scr[CONV_HALO:CONV_HALO + T, :] = uc
    conv = (cw_ref[2:3, :] * uc
            + cw_ref[1:2, :] * uc_scr[CONV_HALO - 1:CONV_HALO - 1 + T, :]
            + cw_ref[0:1, :] * uc_scr[CONV_HALO - 2:CONV_HALO - 2 + T, :])
    ya = proj_scr[:, C:2 * C] * conv * jax.nn.silu(proj_scr[:, 3 * C:4 * C])
    y_scr[:, 0:C] = ya.astype(BF16)
    uc_scr[0:CONV_HALO, :] = uc_scr[T:T + CONV_HALO, :]

    up_scr[POOL_HALO:POOL_HALO + T, :] = proj_scr[:, 4 * C:4 * C + POOL_WIDTH]
    t1 = (lax.broadcasted_iota(jnp.int32, (T, POOL_GROUP), 0) + (i * T + 1)).astype(F32)
    for g, w in enumerate(POOL_WINDOWS):
        lo, hi = g * POOL_GROUP, (g + 1) * POOL_GROUP
        s = up_scr[:, lo:hi]
        k = 1
        while k < w:
            s = s + pltpu.roll(s, k, 0)
            k *= 2
        u = up_scr[POOL_HALO:POOL_HALO + T, lo:hi]
        yb = s[POOL_HALO:, :] / jnp.minimum(t1, float(w)) - u
        gate = proj_scr[:, 4 * C + POOL_WIDTH + lo:4 * C + POOL_WIDTH + hi]
        y_scr[:, C + lo:C + hi] = (yb * ps_ref[:, lo:hi] * jax.nn.silu(gate)).astype(BF16)
    up_scr[0:POOL_HALO, :] = up_scr[T:T + POOL_HALO, :]

    o_ref[...] = x_ref[...] + jnp.dot(y_scr[...], wout_ref[...].astype(BF16), preferred_element_type=F32)


def _even_layer(x, norm_g, w_in, conv_w, pool_w, pool_scale, w_out, next_norm_g, next_w_in_t, next_w_out):
    S = x.shape[0]
    T = min(TILE_EVEN, S)
    steps = S // T
    n_gates = 2 * MLSTM_HEADS
    assert S % T == 0
    assert next_w_in_t.shape == (O_IF + n_gates, D_MODEL) and next_w_out.shape == (ODD_OUT, D_MODEL)
    ra, rb, ro = O_DO // steps, (O_IF - O_DO) // steps, ODD_OUT // steps
    assert ra * steps == O_DO and rb * steps == O_IF - O_DO and ro * steps == ODD_OUT
    const = _const_spec
    slab = lambda r: pl.BlockSpec((r, D_MODEL), lambda i: (i, 0))
    return pl.pallas_call(
        _even_kernel,
        grid=(steps,),
        in_specs=[
            pl.BlockSpec((T, D_MODEL), lambda i: (i, 0)),
            const(1, D_MODEL),
            const(D_MODEL, EVEN_IN),
            const(CONV_K, CONV_WIDTH),
            const(len(POOL_WINDOWS), POOL_GROUP, POOL_GROUP),
            const(1, POOL_WIDTH),
            const(EVEN_OUT, D_MODEL),
            const(1, D_MODEL),
            slab(ra),
            pl.BlockSpec((pl.Element(rb), pl.Element(D_MODEL)), lambda i: (pl.multiple_of(O_DO + n_gates + i * rb, n_gates), 0)),
            pl.BlockSpec((n_gates, D_MODEL), lambda i: (O_DO // n_gates, 0)),
            slab(ro),
        ],
        out_specs=[pl.BlockSpec((T, D_MODEL), lambda i: (i, 0)),
                   slab(ra), slab(rb), pl.BlockSpec((GATE_PAD, D_MODEL), lambda i: (0, 0)), slab(ro)],
        out_shape=[jax.ShapeDtypeStruct((S, D_MODEL), F32),
                   jax.ShapeDtypeStruct((O_DO, D_MODEL), BF16),
                   jax.ShapeDtypeStruct((O_IF - O_DO, D_MODEL), BF16),
                   jax.ShapeDtypeStruct((GATE_PAD, D_MODEL), BF16),
                   jax.ShapeDtypeStruct((ODD_OUT, D_MODEL), BF16)],
        scratch_shapes=[
            pltpu.VMEM((T, EVEN_IN), F32),
            pltpu.VMEM((T + CONV_HALO, CONV_WIDTH), F32),
            pltpu.VMEM((T + POOL_HALO, POOL_WIDTH), F32),
            pltpu.VMEM((T, EVEN_OUT), BF16),
            pltpu.VMEM((D_MODEL, POOL_WIDTH), F32),
        ],
        compiler_params=pltpu.CompilerParams(
            dimension_semantics=("arbitrary",), vmem_limit_bytes=VMEM_LIMIT),
        name="even_layer",
    )(x, norm_g.reshape(1, D_MODEL), w_in, conv_w, pool_w, pool_scale.reshape(1, POOL_WIDTH), w_out,
      next_norm_g.reshape(1, D_MODEL), next_w_in_t, next_w_in_t, next_w_in_t, next_w_out)


def _lane_scan(x, op, identity):
    n = x.shape[1]
    lane = lax.broadcasted_iota(jnp.int32, x.shape, 1)
    k = 1
    while k < n:
        x = op(x, jnp.where(lane >= k, pltpu.roll(x, k, 1), identity))
        k *= 2
    return x


def _odd_init(sink_ref, kvbuf, abias, c_scr, m_scr):
    for b in range(2 * ATT_KV_HEADS):
        kvbuf[b, 0:BLOCK, :] = jnp.zeros((BLOCK, LANES), BF16)
    c_scr[...] = jnp.zeros(c_scr.shape, F32)
    m_scr[...] = jnp.zeros(m_scr.shape, F32)
    qi = lax.broadcasted_iota(jnp.int32, (BLOCK, 2 * BLOCK), 0)
    kj = lax.broadcasted_iota(jnp.int32, (BLOCK, 2 * BLOCK), 1)
    dist = qi + BLOCK - kj
    ok = (dist >= 0) & (dist < WINDOW)
    distf = dist.astype(F32)
    for hd in range(ATT_HEADS):
        slope = 2.0 ** (-8.0 * (hd + 1) / ATT_HEADS)
        band = jnp.where(ok, (-slope * LOG2E) * distf, -jnp.inf)
        abias[hd * BLOCK:(hd + 1) * BLOCK, :] = jnp.where(kj == SINK_KEY, sink_ref[hd] * LOG2E, band)


class _InProj:
    def __init__(self, x_ref, wa_ref, wb_ref, wg_ref, h_scr, proj):
        self.x_ref, self.h_scr, self.proj = x_ref, h_scr, proj
        self.todo = []
        rows = x_ref.shape[0]
        for w_ref, base in ((wa_ref, 0), (wb_ref, O_DO), (wg_ref, O_IF)):
            n = w_ref.shape[0]
            for c0 in range(0, n, MXU_WIDTH):
                for r0 in range(0, rows, INPROJ_ROWS):
                    self.todo.append((w_ref, base, c0, min(c0 + MXU_WIDTH, n), r0, min(r0 + INPROJ_ROWS, rows)))

    def normalise(self):
        self.h_scr[...] = _rms_scale(self.x_ref[...]).astype(BF16)

    def step(self):
        if self.todo:
            w_ref, base, c0, c1, r0, r1 = self.todo.pop(0)
            self.proj[r0:r1, base + c0:base + c1] = lax.dot_general(
                self.h_scr[r0:r1, :], w_ref[c0:c1, :], (((1,), (1,)), ((), ())), preferred_element_type=F32)

    def finish(self):
        while self.todo:
            self.step()


class _NoProj:
    def normalise(self):
        pass

    def step(self):
        pass

    def finish(self):
        pass


def _odd_mixers(first, kv0, proj, x_ref, o_ref, bias_ref, wout_ref, fg_ref,
                kvbuf, abias, c_scr, m_scr, y_scr, nxt, after=None):
    T = x_ref.shape[0]
    NB = T // BLOCK
    H = MLSTM_HEADS
    L = min(MCHUNK, T)
    NEG = -jnp.inf

    lane_q = lax.broadcasted_iota(jnp.int32, (BLOCK, LANES), 1)
    low = lane_q < ATT_HEAD_DIM
    low_t = lax.broadcasted_iota(jnp.int32, (T, LANES), 1) < ATT_HEAD_DIM
    for b, off in enumerate((O_CK, O_CV)):
        kv = proj[:, off:off + ATT_KV_WIDTH]
        sw = pltpu.roll(kv, ATT_HEAD_DIM, 1)
        kvbuf[2 * b, kv0 + BLOCK:kv0 + BLOCK + T, :] = jnp.where(low_t, kv, sw).astype(BF16)
        kvbuf[2 * b + 1, kv0 + BLOCK:kv0 + BLOCK + T, :] = jnp.where(low_t, sw, kv).astype(BF16)
    if first is not False:
        key_lane = lax.broadcasted_iota(jnp.int32, (BLOCK, 2 * BLOCK), 1)
        no_prev = jnp.where((key_lane < BLOCK) & (key_lane != SINK_KEY), jnp.where(first, NEG, 0.0), 0.0)
    sink_row = lax.broadcasted_iota(jnp.int32, (BF16_ROWS, LANES), 0) == SINK_KEY

    def drop_sink_row(a):
        head = a[0:BF16_ROWS, :]
        return jnp.concatenate([jnp.where(sink_row, jnp.zeros_like(head), head), a[BF16_ROWS:, :]], axis=0)

    ones_v = jnp.ones((2 * BLOCK, LANES), BF16)

    def scores(j, g):
        r0 = j * BLOCK
        kk = drop_sink_row(kvbuf[g, kv0 + r0:kv0 + r0 + 2 * BLOCK, :])
        qs = []
        for p in range(g * ATT_GROUP // 2, (g + 1) * ATT_GROUP // 2):
            qp = proj[r0:r0 + BLOCK, O_CQ + p * LANES:O_CQ + (p + 1) * LANES]
            qs.append(jnp.where(low, qp, 0.0).astype(BF16))
            qs.append(jnp.where(low, 0.0, qp).astype(BF16))
        return lax.dot_general(jnp.concatenate(qs, axis=0), kk, (((1,), (1,)), ((), ())),
                               preferred_element_type=F32)

    def attention():
        slots = [(j, g) for j in range(NB) for g in range(ATT_KV_HEADS)]
        s_next = scores(*slots[0])
        for n, (j, g) in enumerate(slots):
            r0 = j * BLOCK
            s_all = s_next
            if n + 1 < len(slots):
                s_next = scores(*slots[n + 1])
            if n % ATTN_SKIP_EVERY != ATTN_SKIP_EVERY - 1:
                nxt.step()
            ps = []
            for r in range(ATT_GROUP):
                hd = g * ATT_GROUP + r
                s = s_all[r * BLOCK:(r + 1) * BLOCK, :] + abias[hd * BLOCK:(hd + 1) * BLOCK, :]
                if j == 0 and first is not False:
                    s = s + no_prev
                ps.append(jnp.exp2(s - jnp.max(s, axis=1, keepdims=True)).astype(BF16))
            vv = jnp.concatenate(
                [drop_sink_row(kvbuf[ATT_KV_HEADS + g, kv0 + r0:kv0 + r0 + 2 * BLOCK, :]), ones_v], axis=1)
            o_all = jnp.dot(jnp.concatenate(ps, axis=0), vv, preferred_element_type=F32)
            for pp in range(ATT_GROUP // 2):
                p = g * ATT_GROUP // 2 + pp
                lo = o_all[2 * pp * BLOCK:(2 * pp + 1) * BLOCK, :]
                hi = o_all[(2 * pp + 1) * BLOCK:(2 * pp + 2) * BLOCK, :]
                att = jnp.where(low, lo[:, :LANES] / lo[:, LANES:], hi[:, :LANES] / hi[:, LANES:])
                gate = proj[r0:r0 + BLOCK, O_GC + p * LANES:O_GC + (p + 1) * LANES]
                y_scr[r0:r0 + BLOCK, p * LANES:(p + 1) * LANES] = (att * jax.nn.silu(gate)).astype(BF16)

    gt = proj[:, O_IF:O_IF + GATE_PAD].T[0:2 * H, :]
    pre = gt + jnp.concatenate([bias_ref[...]] * (T // LANES), axis=1)
    row = lax.broadcasted_iota(jnp.int32, (2 * H, T), 0)
    top = row < H
    bcum = _lane_scan(jax.nn.log_sigmoid(pre), jnp.add, 0.0)
    bsw = pltpu.roll(bcum, H, 0)
    a_top = pre - bsw
    ag = jnp.where(top, a_top, pltpu.roll(a_top, H, 0))
    bf = jnp.where(top, bsw, bcum)
    m_in = jnp.concatenate([m_scr[...]] * (T // LANES), axis=1)
    mg = jnp.maximum(m_in, _lane_scan(ag, jnp.maximum, NEG))
    m_t = bf + mg
    nrm = jnp.exp(-m_t)
    inter_parts, wk_parts, decay_parts = [], [], []
    for c in range(T // L):
        m_start = m_in[:, 0:1] if c == 0 else mg[:, c * L - 1:c * L]
        m_last = mg[:, (c + 1) * L - 1:(c + 1) * L]
        inter_parts.append(jnp.exp(m_start - mg[:, c * L:(c + 1) * L]))
        wk_parts.append(jnp.exp(ag[:, c * L:(c + 1) * L] - m_last))
        decay_parts.append(jnp.broadcast_to(jnp.exp(m_start - m_last), (2 * H, 2 * MLSTM_V_DIM)))
    inter = jnp.concatenate(inter_parts, axis=1)
    m_scr[...] = jnp.broadcast_to(m_t[:, T - 1:T], m_scr.shape)
    ag2 = ag * LOG2E
    cols = jnp.concatenate([mg * LOG2E, inter, nrm, jnp.zeros((LANES - 6 * H, T), F32)], axis=0).T

    kT = proj[:, O_DK:O_DK + MLSTM_QK_WIDTH].T
    lane_p = lax.broadcasted_iota(jnp.int32, (L, LANES), 1)
    low_p = lane_p < MLSTM_QK_DIM
    tri = (lax.broadcasted_iota(jnp.int32, (BLOCK, BLOCK), 1) <= lax.broadcasted_iota(jnp.int32, (BLOCK, BLOCK), 0))
    ones = jnp.ones((L, MLSTM_V_DIM), BF16)
    zero_blk = jnp.zeros((BLOCK, BLOCK), BF16)

    def mlstm_chunk(c):
        r0 = c * L
        for pair in range(H // 2):
            qp = proj[r0:r0 + L, O_DQ + pair * LANES:O_DQ + (pair + 1) * LANES]
            kT_pair = kT[pair * LANES:(pair + 1) * LANES, r0:r0 + L].astype(BF16)
            c_pair = c_scr[pair * LANES:(pair + 1) * LANES, :].astype(BF16)
            for half in range(2):
                hd = 2 * pair + half
                q_h = jnp.where(low_p, qp, 0.0) if half == 0 else jnp.where(low_p, 0.0, qp)
                s = jnp.dot(q_h.astype(BF16), kT_pair, preferred_element_type=F32)
                if H * c + hd < MLSTM_CHUNKS:
                    nxt.step()
                p_rows = []
                for bi in range(L // BLOCK):
                    m_col = cols[r0 + bi * BLOCK:r0 + (bi + 1) * BLOCK, hd:hd + 1]
                    blks = []
                    for bj in range(L // BLOCK):
                        if bj > bi:
                            blks.append(zero_blk)
                            continue
                        arg = ag2[hd:hd + 1, r0 + bj * BLOCK:r0 + (bj + 1) * BLOCK] - m_col
                        if bj == bi:
                            arg = jnp.where(tri, arg, NEG)
                        sb = s[bi * BLOCK:(bi + 1) * BLOCK, bj * BLOCK:(bj + 1) * BLOCK]
                        blks.append((sb * jnp.exp2(arg)).astype(BF16))
                    p_rows.append(jnp.concatenate(blks, axis=1))
                p_bf = jnp.concatenate(p_rows, axis=0)
                v_aug = jnp.concatenate(
                    [proj[r0:r0 + L, O_DV + hd * MLSTM_V_DIM:O_DV + (hd + 1) * MLSTM_V_DIM].astype(BF16), ones],
                    axis=1)
                q_int = (q_h * cols[r0:r0 + L, 2 * H + hd:2 * H + hd + 1]).astype(BF16)
                num = jnp.dot(jnp.concatenate([p_bf, q_int], axis=1),
                              jnp.concatenate([v_aug, c_pair], axis=0), preferred_element_type=F32)
                den = jnp.maximum(jnp.abs(num[:, MLSTM_V_DIM:]), cols[r0:r0 + L, 4 * H + hd:4 * H + hd + 1])
                hm = num[:, :MLSTM_V_DIM] / den
                og = proj[r0:r0 + L, O_DO + hd * MLSTM_V_DIM:O_DO + (hd + 1) * MLSTM_V_DIM]
                gd = proj[r0:r0 + L, O_GD + hd * MLSTM_V_DIM:O_GD + (hd + 1) * MLSTM_V_DIM]
                y_scr[r0:r0 + L, ATT_WIDTH + hd * MLSTM_V_DIM:ATT_WIDTH + (hd + 1) * MLSTM_V_DIM] = (
                    hm * jax.nn.sigmoid(og) * jax.nn.silu(gd)).astype(BF16)
                kw = (kT[hd * MLSTM_QK_DIM:(hd + 1) * MLSTM_QK_DIM, r0:r0 + L]
                      * wk_parts[c][hd:hd + 1, :]).astype(BF16)
                upd = jnp.dot(kw, v_aug, preferred_element_type=F32)
                rows = slice(hd * MLSTM_QK_DIM, (hd + 1) * MLSTM_QK_DIM)
                c_scr[rows, :] = decay_parts[c][hd:hd + 1, :] * c_scr[rows, :] + upd

    attention()
    for _ in range(PRE_CHUNKS):
        nxt.step()
    for c in range(T // L):
        mlstm_chunk(c)
    nxt.finish()
    if after is not None:
        after.normalise()
    for r0 in range(0, T, OUT_ROWS):
        x2 = x_ref[r0:r0 + OUT_ROWS, :] + jnp.dot(y_scr[r0:r0 + OUT_ROWS, :], wout_ref[...],
                                                   preferred_element_type=F32)
        o_ref[r0:r0 + OUT_ROWS, :] = _rmsnorm(x2, fg_ref[...])


def _odd_kernel(sink_ref, xa_ref, xb_ref, xn_ref, wa_ref, wb_ref, wg_ref, bias_ref, wout_ref,
                fg_ref, o_ref, proj_a, proj_b, h_a, h_b, kvbuf, abias, c_scr, m_scr, y_a, y_b):
    s = pl.program_id(0)
    T = xa_ref.shape[0]
    w = (wa_ref, wb_ref, wg_ref)
    rest = (bias_ref, wout_ref, fg_ref, kvbuf, abias, c_scr, m_scr)

    @pl.when(s == 0)
    def _():
        _odd_init(sink_ref, kvbuf, abias, c_scr, m_scr)
        first = _InProj(xa_ref, *w, h_a, proj_a)
        first.normalise()
        first.finish()

    def two_tiles(last):
        nxt = _InProj(xb_ref, *w, h_a, proj_b)
        nxt.normalise()
        after = _NoProj() if last else _InProj(xn_ref, *w, h_b, proj_a)
        _odd_mixers(s == 0, 0, proj_a, xa_ref, o_ref.at[0:T], *rest, y_a, nxt, after)
        _odd_mixers(False, T, proj_b, xb_ref, o_ref.at[T:2 * T], *rest, y_b, after)
        for b in range(2 * ATT_KV_HEADS):
            kvbuf[b, 0:BLOCK, :] = kvbuf[b, 2 * T:2 * T + BLOCK, :]

    last_step = pl.num_programs(0) - 1
    pl.when(s < last_step)(lambda: two_tiles(False))
    pl.when(s == last_step)(lambda: two_tiles(True))


def _odd_layer(x, wa, wb, wg, i_bias, f_bias, sinks, wo, final_g):
    S = x.shape[0]
    T = min(TILE_ODD, S // 2)
    NT = S // T
    assert S % (2 * T) == 0 and T % BLOCK == 0 and T % min(MCHUNK, T) == 0
    bias = jnp.broadcast_to(jnp.concatenate([i_bias, f_bias]).astype(F32)[:, None], (2 * MLSTM_HEADS, LANES))
    const = _const_spec
    return pl.pallas_call(
        _odd_kernel,
        grid=(NT // 2,),
        in_specs=[
            pl.BlockSpec(memory_space=pltpu.SMEM),
            pl.BlockSpec((T, D_MODEL), lambda s: (2 * s, 0)),
            pl.BlockSpec((T, D_MODEL), lambda s: (2 * s + 1, 0)),
            pl.BlockSpec((T, D_MODEL), lambda s: (jnp.minimum(2 * s + 2, NT - 1), 0)),
            const(O_DO, D_MODEL),
            const(O_IF - O_DO, D_MODEL),
            const(GATE_PAD, D_MODEL),
            const(2 * MLSTM_HEADS, LANES),
            const(ODD_OUT, D_MODEL),
            const(1, D_MODEL),
        ],
        out_specs=pl.BlockSpec((2 * T, D_MODEL), lambda s: (s, 0)),
        out_shape=jax.ShapeDtypeStruct((S, D_MODEL), F32),
        scratch_shapes=[
            pltpu.VMEM((T, ODD_IN_PAD), F32),
            pltpu.VMEM((T, ODD_IN_PAD), F32),
            pltpu.VMEM((T, D_MODEL), BF16),
            pltpu.VMEM((T, D_MODEL), BF16),
            pltpu.VMEM((2 * ATT_KV_HEADS, 2 * T + BLOCK, LANES), BF16),
            pltpu.VMEM((ATT_HEADS * BLOCK, 2 * BLOCK), F32),
            pltpu.VMEM((MLSTM_HEADS * MLSTM_QK_DIM, 2 * MLSTM_V_DIM), F32),
            pltpu.VMEM((2 * MLSTM_HEADS, LANES), F32),
            pltpu.VMEM((T, ODD_OUT), BF16),
            pltpu.VMEM((T, ODD_OUT), BF16),
        ],
        compiler_params=pltpu.CompilerParams(
            dimension_semantics=("arbitrary",), vmem_limit_bytes=VMEM_LIMIT),
        name="odd_layer",
    )(sinks.astype(F32), x, x, x, wa, wb, wg, bias, wo,
      final_g.reshape(1, D_MODEL))


def kernel(x, even_norm, even_w_in, even_conv_w, even_pool_w, even_pool_scale, even_w_out,
           odd_norm, odd_w_in, odd_i_bias, odd_f_bias, odd_sinks, odd_w_out, final_norm):
    B, S, D = x.shape
    assert B == 1 and D == D_MODEL
    assert even_norm.shape[0] == 1 and odd_norm.shape[0] == 1
    h = x.reshape(S, D)
    h, wa, wb, wg, wo = _even_layer(h, even_norm[0], even_w_in[0], even_conv_w[0], even_pool_w[0],
                                    even_pool_scale[0], even_w_out[0], odd_norm[0], odd_w_in[0].T, odd_w_out[0])
    h = _odd_layer(h, wa, wb, wg, odd_i_bias[0], odd_f_bias[0], odd_sinks[0], wo, final_norm)
    return h.reshape(B, S, D)
```

```python
import jax
import jax.numpy as jnp
from jax import lax
from jax.experimental import pallas as pl
from jax.experimental.pallas import tpu as pltpu

F32 = jnp.float32
BF16 = jnp.bfloat16

D_MODEL = 1024
RMS_EPS = 1e-6

CONV_WIDTH = 512
CONV_K = 3
POOL_WINDOWS = (2, 4, 8, 16)
POOL_WIDTH = 512
POOL_GROUP = POOL_WIDTH // len(POOL_WINDOWS)
EVEN_IN = 4 * CONV_WIDTH + 2 * POOL_WIDTH
EVEN_OUT = CONV_WIDTH + POOL_WIDTH
CONV_HALO = 8
POOL_HALO = 16

ATT_HEADS = 8
ATT_KV_HEADS = 2
ATT_GROUP = ATT_HEADS // ATT_KV_HEADS
ATT_HEAD_DIM = 64
ATT_WIDTH = ATT_HEADS * ATT_HEAD_DIM
ATT_KV_WIDTH = ATT_KV_HEADS * ATT_HEAD_DIM
WINDOW = 128
BLOCK = 128
MLSTM_HEADS = 4
MLSTM_QK_DIM = 64
MLSTM_V_DIM = 128
MLSTM_QK_WIDTH = MLSTM_HEADS * MLSTM_QK_DIM
MLSTM_WIDTH = MLSTM_HEADS * MLSTM_V_DIM
ODD_OUT = ATT_WIDTH + MLSTM_WIDTH
LANES = 128
BF16_ROWS = 16
SINK_KEY = 0
MXU_WIDTH = 256
GATE_PAD = LANES

O_CQ = 0
O_CK = O_CQ + ATT_WIDTH
O_CV = O_CK + ATT_KV_WIDTH
O_GC = O_CV + ATT_KV_WIDTH
O_DQ = O_GC + ATT_WIDTH
O_DK = O_DQ + MLSTM_QK_WIDTH
O_DV = O_DK + MLSTM_QK_WIDTH
O_DO = O_DV + MLSTM_WIDTH
O_GD = O_DO + MLSTM_WIDTH
O_IF = O_GD + MLSTM_WIDTH
ODD_IN_PAD = O_IF + GATE_PAD

TILE_EVEN = 1024
EVEN_CHUNK = 512
TILE_ODD = 512
MCHUNK = 256
INPROJ_ROWS = 512
OUT_ROWS = 256
PRE_CHUNKS = 3
MLSTM_CHUNKS = 5
ATTN_SKIP_EVERY = 4
WEIGHT_LOAD_DEPTH = 4
LOG2E = 1.4426950408889634
ATT_Q_SCALE = ATT_HEAD_DIM ** -0.5 * LOG2E
MLSTM_Q_SCALE = MLSTM_QK_DIM ** -0.5
VMEM_LIMIT = 60 * 1024 * 1024


def _const_spec(*shape):
    return pl.BlockSpec(shape, lambda i: (0,) * len(shape), pipeline_mode=pl.Buffered(1))


class _WeightLoad:
    def __init__(self, slabs, sem, depth):
        self.copies = [pltpu.make_async_copy(hbm.at[idx], vmem.at[idx], sem.at[k])
                       for k, (hbm, vmem, idx) in enumerate(slabs)]
        self.started = self.waited = 0
        for _ in range(depth):
            self._start_one()

    def _start_one(self):
        if self.started < len(self.copies):
            self.copies[self.started].start()
            self.started += 1

    def wait_one(self):
        self.copies[self.waited].wait()
        self.waited += 1
        self._start_one()

    def wait_all(self):
        while self.waited < len(self.copies):
            self.wait_one()


def _row_slabs(hbm, vmem):
    n = vmem.shape[0]
    return [(hbm, vmem, (slice(r0, min(r0 + MXU_WIDTH, n)), slice(None))) for r0 in range(0, n, MXU_WIDTH)]


def _rms_scale(x):
    ms = jnp.mean(x * x, axis=-1, keepdims=True)
    return x * lax.rsqrt(ms + RMS_EPS)


def _rmsnorm(x, g):
    return _rms_scale(x) * g


def _even_kernel(x_ref, g_ref, win_hbm, cw_ref, pw_ref, ps_ref, wout_hbm,
                 ng_ref, nwa32_ref, nwb32_ref, nwg32_ref, nwo32_ref,
                 o_ref, nwa_ref, nwb_ref, nwg_ref, nwo_ref,
                 win_ref, wout_ref, w_sem, proj_scr, uc_scr, up_scr, y_scr, wfold_scr):
    i = pl.program_id(0)
    T = x_ref.shape[0]
    C = CONV_WIDTH

    ng = ng_ref[...]
    ra = nwa32_ref.shape[0]
    feat = lax.broadcasted_iota(jnp.int32, (ra, D_MODEL), 0) + i * ra
    q_scale = jnp.where(feat < O_CK, ATT_Q_SCALE,
                        jnp.where(feat < O_DQ, 1.0, jnp.where(feat < O_DK, MLSTM_Q_SCALE, 1.0)))
    nwa_ref[...] = (nwa32_ref[...] * ng * q_scale).astype(BF16)
    nwb_ref[...] = (nwb32_ref[...] * ng).astype(BF16)
    nwo_ref[...] = nwo32_ref[...].astype(BF16)
    nwg_ref[...] = jnp.concatenate(
        [nwg32_ref[...] * ng, jnp.zeros((GATE_PAD - 2 * MLSTM_HEADS, D_MODEL), F32)], axis=0).astype(BF16)

    @pl.when(i == 0)
    def _():
        cols = [4 * C] + [c0 for c0 in range(0, EVEN_IN, EVEN_CHUNK) if c0 != 4 * C]
        load = _WeightLoad([(win_hbm, win_ref, (slice(None), slice(c0, c0 + EVEN_CHUNK))) for c0 in cols]
                           + _row_slabs(wout_hbm, wout_ref), w_sem, WEIGHT_LOAD_DEPTH)
        uc_scr[0:CONV_HALO, :] = jnp.zeros((CONV_HALO, C), F32)
        up_scr[0:POOL_HALO, :] = jnp.zeros((POOL_HALO, POOL_WIDTH), F32)
        load.wait_one()
        for g in range(len(POOL_WINDOWS)):
            lo, hi = g * POOL_GROUP, (g + 1) * POOL_GROUP
            wfold_scr[:, lo:hi] = jnp.dot(win_ref[:, 4 * C + lo:4 * C + hi], pw_ref[g],
                                          precision=lax.Precision.HIGHEST, preferred_element_type=F32)
        load.wait_all()

    h = _rmsnorm(x_ref[...], g_ref[...]).astype(BF16)
    for c0 in range(0, EVEN_IN, EVEN_CHUNK):
        w = wfold_scr[...] if c0 == 4 * C else win_ref[:, c0:c0 + EVEN_CHUNK]
        proj_scr[:, c0:c0 + EVEN_CHUNK] = jnp.dot(h, w.astype(BF16), preferred_element_type=F32)

    uc = proj_scr[:, 2 * C:3 * C] * proj_scr[:, 0:C]
    uc_scr[CONV_HALO:CONV_HALO + T, :] = uc
    conv = (cw_ref[2:3, :] * uc
            + cw_ref[1:2, :] * uc_scr[CONV_HALO - 1:CONV_HALO - 1 + T, :]
            + cw_ref[0:1, :] * uc_scr[CONV_HALO - 2:CONV_HALO - 2 + T, :])
    ya = proj_scr[:, C:2 * C] * conv * jax.nn.silu(proj_scr[:, 3 * C:4 * C])
    y_scr[:, 0:C] = ya.astype(BF16)
    uc_scr[0:CONV_HALO, :] = uc_scr[T:T + CONV_HALO, :]

    up_scr[POOL_HALO:POOL_HALO + T, :] = proj_scr[:, 4 * C:4 * C + POOL_WIDTH]
    t1 = (lax.broadcasted_iota(jnp.int32, (T, POOL_GROUP), 0) + (i * T + 1)).astype(F32)
    for g, w in enumerate(POOL_WINDOWS):
        lo, hi = g * POOL_GROUP, (g + 1) * POOL_GROUP
        s = up_scr[:, lo:hi]
        k = 1
        while k < w:
            s = s + pltpu.roll(s, k, 0)
            k *= 2
        u = up_scr[POOL_HALO:POOL_HALO + T, lo:hi]
        yb = s[POOL_HALO:, :] / jnp.minimum(t1, float(w)) - u
        gate = proj_scr[:, 4 * C + POOL_WIDTH + lo:4 * C + POOL_WIDTH + hi]
        y_scr[:, C + lo:C + hi] = (yb * ps_ref[:, lo:hi] * jax.nn.silu(gate)).astype(BF16)
    up_scr[0:POOL_HALO, :] = up_scr[T:T + POOL_HALO, :]

    o_ref[...] = x_ref[...] + jnp.dot(y_scr[...], wout_ref[...].astype(BF16), preferred_element_type=F32)


def _even_layer(x, norm_g, w_in, conv_w, pool_w, pool_scale, w_out, next_norm_g, next_w_in_t, next_w_out):
    S = x.shape[0]
    T = min(TILE_EVEN, S)
    steps = S // T
    n_gates = 2 * MLSTM_HEADS
    assert S % T == 0
    assert w_in.shape == (D_MODEL, EVEN_IN) and w_out.shape == (EVEN_OUT, D_MODEL)
    assert next_w_in_t.shape == (O_IF + n_gates, D_MODEL) and next_w_out.shape == (ODD_OUT, D_MODEL)
    ra, rb, ro = O_DO // steps, (O_IF - O_DO) // steps, ODD_OUT // steps
    assert ra * steps == O_DO and rb * steps == O_IF - O_DO and ro * steps == ODD_OUT
    const = _const_spec
    slab = lambda r: pl.BlockSpec((r, D_MODEL), lambda i: (i, 0))
    return pl.pallas_call(
        _even_kernel,
        grid=(steps,),
        in_specs=[
            pl.BlockSpec((T, D_MODEL), lambda i: (i, 0)),
            const(1, D_MODEL),
            pl.BlockSpec(memory_space=pl.ANY),
            const(CONV_K, CONV_WIDTH),
            const(len(POOL_WINDOWS), POOL_GROUP, POOL_GROUP),
            const(1, POOL_WIDTH),
            pl.BlockSpec(memory_space=pl.ANY),
            const(1, D_MODEL),
            slab(ra),
            pl.BlockSpec((pl.Element(rb), pl.Element(D_MODEL)), lambda i: (pl.multiple_of(O_DO + n_gates + i * rb, n_gates), 0)),
            pl.BlockSpec((n_gates, D_MODEL), lambda i: (O_DO // n_gates, 0)),
            slab(ro),
        ],
        out_specs=[pl.BlockSpec((T, D_MODEL), lambda i: (i, 0)),
                   slab(ra), slab(rb), pl.BlockSpec((GATE_PAD, D_MODEL), lambda i: (0, 0)), slab(ro)],
        out_shape=[jax.ShapeDtypeStruct((S, D_MODEL), F32),
                   jax.ShapeDtypeStruct((O_DO, D_MODEL), BF16),
                   jax.ShapeDtypeStruct((O_IF - O_DO, D_MODEL), BF16),
                   jax.ShapeDtypeStruct((GATE_PAD, D_MODEL), BF16),
                   jax.ShapeDtypeStruct((ODD_OUT, D_MODEL), BF16)],
        scratch_shapes=[
            pltpu.VMEM((D_MODEL, EVEN_IN), F32),
            pltpu.VMEM((EVEN_OUT, D_MODEL), F32),
            pltpu.SemaphoreType.DMA((EVEN_IN // EVEN_CHUNK + pl.cdiv(EVEN_OUT, MXU_WIDTH),)),
            pltpu.VMEM((T, EVEN_IN), F32),
            pltpu.VMEM((T + CONV_HALO, CONV_WIDTH), F32),
            pltpu.VMEM((T + POOL_HALO, POOL_WIDTH), F32),
            pltpu.VMEM((T, EVEN_OUT), BF16),
            pltpu.VMEM((D_MODEL, POOL_WIDTH), F32),
        ],
        compiler_params=pltpu.CompilerParams(
            dimension_semantics=("arbitrary",), vmem_limit_bytes=VMEM_LIMIT),
        name="even_layer",
    )(x, norm_g.reshape(1, D_MODEL), w_in, conv_w, pool_w, pool_scale.reshape(1, POOL_WIDTH), w_out,
      next_norm_g.reshape(1, D_MODEL), next_w_in_t, next_w_in_t, next_w_in_t, next_w_out)


def _lane_scan(x, op, identity):
    n = x.shape[1]
    lane = lax.broadcasted_iota(jnp.int32, x.shape, 1)
    k = 1
    while k < n:
        x = op(x, jnp.where(lane >= k, pltpu.roll(x, k, 1), identity))
        k *= 2
    return x


def _odd_init(sink_ref, kvbuf, abias, c_scr, m_scr):
    for b in range(2 * ATT_KV_HEADS):
        kvbuf[b, 0:BLOCK, :] = jnp.zeros((BLOCK, LANES), BF16)
    c_scr[...] = jnp.zeros(c_scr.shape, F32)
    m_scr[...] = jnp.zeros(m_scr.shape, F32)
    qi = lax.broadcasted_iota(jnp.int32, (BLOCK, 2 * BLOCK), 0)
    kj = lax.broadcasted_iota(jnp.int32, (BLOCK, 2 * BLOCK), 1)
    dist = qi + BLOCK - kj
    ok = (dist >= 0) & (dist < WINDOW)
    distf = dist.astype(F32)
    for hd in range(ATT_HEADS):
        slope = 2.0 ** (-8.0 * (hd + 1) / ATT_HEADS)
        band = jnp.where(ok, (-slope * LOG2E) * distf, -jnp.inf)
        abias[hd * BLOCK:(hd + 1) * BLOCK, :] = jnp.where(kj == SINK_KEY, sink_ref[hd] * LOG2E, band)


class _InProj:
    def __init__(self, x_ref, wa_ref, wb_ref, wg_ref, h_scr, proj, before_slab=None):
        self.x_ref, self.h_scr, self.proj, self.before_slab = x_ref, h_scr, proj, before_slab
        self.todo = []
        rows = x_ref.shape[0]
        for w_ref, base in ((wa_ref, 0), (wb_ref, O_DO), (wg_ref, O_IF)):
            n = w_ref.shape[0]
            for c0 in range(0, n, MXU_WIDTH):
                for r0 in range(0, rows, INPROJ_ROWS):
                    self.todo.append((w_ref, base, c0, min(c0 + MXU_WIDTH, n), r0, min(r0 + INPROJ_ROWS, rows)))

    def normalise(self):
        self.h_scr[...] = _rms_scale(self.x_ref[...]).astype(BF16)

    def step(self):
        if self.todo:
            w_ref, base, c0, c1, r0, r1 = self.todo.pop(0)
            if self.before_slab is not None and r0 == 0:
                self.before_slab()
            self.proj[r0:r1, base + c0:base + c1] = lax.dot_general(
                self.h_scr[r0:r1, :], w_ref[c0:c1, :], (((1,), (1,)), ((), ())), preferred_element_type=F32)

    def finish(self):
        while self.todo:
            self.step()


def _odd_mixers(first, kv0, proj, x_ref, o_ref, bias_ref, wout_ref, fg_ref,
                kvbuf, abias, c_scr, m_scr, y_scr, nxt, after=None):
    T = x_ref.shape[0]
    NB = T // BLOCK
    H = MLSTM_HEADS
    L = min(MCHUNK, T)
    NEG = -jnp.inf

    lane_q = lax.broadcasted_iota(jnp.int32, (BLOCK, LANES), 1)
    low = lane_q < ATT_HEAD_DIM
    low_t = lax.broadcasted_iota(jnp.int32, (T, LANES), 1) < ATT_HEAD_DIM
    for b, off in enumerate((O_CK, O_CV)):
        kv = proj[:, off:off + ATT_KV_WIDTH]
        sw = pltpu.roll(kv, ATT_HEAD_DIM, 1)
        kvbuf[2 * b, kv0 + BLOCK:kv0 + BLOCK + T, :] = jnp.where(low_t, kv, sw).astype(BF16)
        kvbuf[2 * b + 1, kv0 + BLOCK:kv0 + BLOCK + T, :] = jnp.where(low_t, sw, kv).astype(BF16)
    if first is not False:
        key_lane = lax.broadcasted_iota(jnp.int32, (BLOCK, 2 * BLOCK), 1)
        no_prev = jnp.where((key_lane < BLOCK) & (key_lane != SINK_KEY), jnp.where(first, NEG, 0.0), 0.0)
    sink_row = lax.broadcasted_iota(jnp.int32, (BF16_ROWS, LANES), 0) == SINK_KEY

    def drop_sink_row(a):
        head = a[0:BF16_ROWS, :]
        return jnp.concatenate([jnp.where(sink_row, jnp.zeros_like(head), head), a[BF16_ROWS:, :]], axis=0)

    ones_v = jnp.ones((2 * BLOCK, LANES), BF16)

    def scores(j, g):
        r0 = j * BLOCK
        kk = drop_sink_row(kvbuf[g, kv0 + r0:kv0 + r0 + 2 * BLOCK, :])
        qs = []
        for p in range(g * ATT_GROUP // 2, (g + 1) * ATT_GROUP // 2):
            qp = proj[r0:r0 + BLOCK, O_CQ + p * LANES:O_CQ + (p + 1) * LANES]
            qs.append(jnp.where(low, qp, 0.0).astype(BF16))
            qs.append(jnp.where(low, 0.0, qp).astype(BF16))
        return lax.dot_general(jnp.concatenate(qs, axis=0), kk, (((1,), (1,)), ((), ())),
                               preferred_element_type=F32)

    def attention():
        slots = [(j, g) for j in range(NB) for g in range(ATT_KV_HEADS)]
        s_next = scores(*slots[0])
        for n, (j, g) in enumerate(slots):
            r0 = j * BLOCK
            s_all = s_next
            if n + 1 < len(slots):
                s_next = scores(*slots[n + 1])
            if n % ATTN_SKIP_EVERY != ATTN_SKIP_EVERY - 1:
                nxt.step()
            ps = []
            for r in range(ATT_GROUP):
                hd = g * ATT_GROUP + r
                s = s_all[r * BLOCK:(r + 1) * BLOCK, :] + abias[hd * BLOCK:(hd + 1) * BLOCK, :]
                if j == 0 and first is not False:
                    s = s + no_prev
                ps.append(jnp.exp2(s - jnp.max(s, axis=1, keepdims=True)).astype(BF16))
            vv = jnp.concatenate(
                [drop_sink_row(kvbuf[ATT_KV_HEADS + g, kv0 + r0:kv0 + r0 + 2 * BLOCK, :]), ones_v], axis=1)
            o_all = jnp.dot(jnp.concatenate(ps, axis=0), vv, preferred_element_type=F32)
            for pp in range(ATT_GROUP // 2):
                p = g * ATT_GROUP // 2 + pp
                lo = o_all[2 * pp * BLOCK:(2 * pp + 1) * BLOCK, :]
                hi = o_all[(2 * pp + 1) * BLOCK:(2 * pp + 2) * BLOCK, :]
                att = jnp.where(low, lo[:, :LANES] / lo[:, LANES:], hi[:, :LANES] / hi[:, LANES:])
                gate = proj[r0:r0 + BLOCK, O_GC + p * LANES:O_GC + (p + 1) * LANES]
                y_scr[r0:r0 + BLOCK, p * LANES:(p + 1) * LANES] = (att * jax.nn.silu(gate)).astype(BF16)

    gt = proj[:, O_IF:O_IF + GATE_PAD].T[0:2 * H, :]
    pre = gt + jnp.concatenate([bias_ref[...]] * (T // LANES), axis=1)
    row = lax.broadcasted_iota(jnp.int32, (2 * H, T), 0)
    top = row < H
    bcum = _lane_scan(jax.nn.log_sigmoid(pre), jnp.add, 0.0)
    bsw = pltpu.roll(bcum, H, 0)
    a_top = pre - bsw
    ag = jnp.where(top, a_top, pltpu.roll(a_top, H, 0))
    bf = jnp.where(top, bsw, bcum)
    m_in = jnp.concatenate([m_scr[...]] * (T // LANES), axis=1)
    mg = jnp.maximum(m_in, _lane_scan(ag, jnp.maximum, NEG))
    m_t = bf + mg
    nrm = jnp.exp(-m_t)
    inter_parts, wk_parts, decay_parts = [], [], []
    for c in range(T // L):
        m_start = m_in[:, 0:1] if c == 0 else mg[:, c * L - 1:c * L]
        m_last = mg[:, (c + 1) * L - 1:(c + 1) * L]
        inter_parts.append(jnp.exp(m_start - mg[:, c * L:(c + 1) * L]))
        wk_parts.append(jnp.exp(ag[:, c * L:(c + 1) * L] - m_last))
        decay_parts.append(jnp.broadcast_to(jnp.exp(m_start - m_last), (2 * H, 2 * MLSTM_V_DIM)))
    inter = jnp.concatenate(inter_parts, axis=1)
    m_scr[...] = jnp.broadcast_to(m_t[:, T - 1:T], m_scr.shape)
    ag2 = ag * LOG2E
    cols = jnp.concatenate([mg * LOG2E, inter, nrm, jnp.zeros((LANES - 6 * H, T), F32)], axis=0).T

    kT = proj[:, O_DK:O_DK + MLSTM_QK_WIDTH].T
    lane_p = lax.broadcasted_iota(jnp.int32, (L, LANES), 1)
    low_p = lane_p < MLSTM_QK_DIM
    tri = (lax.broadcasted_iota(jnp.int32, (BLOCK, BLOCK), 1) <= lax.broadcasted_iota(jnp.int32, (BLOCK, BLOCK), 0))
    ones = jnp.ones((L, MLSTM_V_DIM), BF16)
    zero_blk = jnp.zeros((BLOCK, BLOCK), BF16)

    def mlstm_chunk(c):
        r0 = c * L
        for pair in range(H // 2):
            qp = proj[r0:r0 + L, O_DQ + pair * LANES:O_DQ + (pair + 1) * LANES]
            kT_pair = kT[pair * LANES:(pair + 1) * LANES, r0:r0 + L].astype(BF16)
            c_pair = c_scr[pair * LANES:(pair + 1) * LANES, :].astype(BF16)
            for half in range(2):
                hd = 2 * pair + half
                q_h = jnp.where(low_p, qp, 0.0) if half == 0 else jnp.where(low_p, 0.0, qp)
                s = jnp.dot(q_h.astype(BF16), kT_pair, preferred_element_type=F32)
                if H * c + hd < MLSTM_CHUNKS:
                    nxt.step()
                p_rows = []
                for bi in range(L // BLOCK):
                    m_col = cols[r0 + bi * BLOCK:r0 + (bi + 1) * BLOCK, hd:hd + 1]
                    blks = []
                    for bj in range(L // BLOCK):
                        if bj > bi:
                            blks.append(zero_blk)
                            continue
                        arg = ag2[hd:hd + 1, r0 + bj * BLOCK:r0 + (bj + 1) * BLOCK] - m_col
                        if bj == bi:
                            arg = jnp.where(tri, arg, NEG)
                        sb = s[bi * BLOCK:(bi + 1) * BLOCK, bj * BLOCK:(bj + 1) * BLOCK]
                        blks.append((sb * jnp.exp2(arg)).astype(BF16))
                    p_rows.append(jnp.concatenate(blks, axis=1))
                p_bf = jnp.concatenate(p_rows, axis=0)
                v_aug = jnp.concatenate(
                    [proj[r0:r0 + L, O_DV + hd * MLSTM_V_DIM:O_DV + (hd + 1) * MLSTM_V_DIM].astype(BF16), ones],
                    axis=1)
                q_int = (q_h * cols[r0:r0 + L, 2 * H + hd:2 * H + hd + 1]).astype(BF16)
                num = jnp.dot(jnp.concatenate([p_bf, q_int], axis=1),
                              jnp.concatenate([v_aug, c_pair], axis=0), preferred_element_type=F32)
                den = jnp.maximum(jnp.abs(num[:, MLSTM_V_DIM:]), cols[r0:r0 + L, 4 * H + hd:4 * H + hd + 1])
                hm = num[:, :MLSTM_V_DIM] / den
                og = proj[r0:r0 + L, O_DO + hd * MLSTM_V_DIM:O_DO + (hd + 1) * MLSTM_V_DIM]
                gd = proj[r0:r0 + L, O_GD + hd * MLSTM_V_DIM:O_GD + (hd + 1) * MLSTM_V_DIM]
                y_scr[r0:r0 + L, ATT_WIDTH + hd * MLSTM_V_DIM:ATT_WIDTH + (hd + 1) * MLSTM_V_DIM] = (
                    hm * jax.nn.sigmoid(og) * jax.nn.silu(gd)).astype(BF16)
                kw = (kT[hd * MLSTM_QK_DIM:(hd + 1) * MLSTM_QK_DIM, r0:r0 + L]
                      * wk_parts[c][hd:hd + 1, :]).astype(BF16)
                upd = jnp.dot(kw, v_aug, preferred_element_type=F32)
                rows = slice(hd * MLSTM_QK_DIM, (hd + 1) * MLSTM_QK_DIM)
                c_scr[rows, :] = decay_parts[c][hd:hd + 1, :] * c_scr[rows, :] + upd

    attention()
    for _ in range(PRE_CHUNKS):
        nxt.step()
    for c in range(T // L):
        mlstm_chunk(c)
    nxt.finish()
    if after is not None:
        after.normalise()
    for r0 in range(0, T, OUT_ROWS):
        x2 = x_ref[r0:r0 + OUT_ROWS, :] + jnp.dot(y_scr[r0:r0 + OUT_ROWS, :], wout_ref[...],
                                                   preferred_element_type=F32)
        o_ref[r0:r0 + OUT_ROWS, :] = _rmsnorm(x2, fg_ref[...])


def _odd_kernel(sink_ref, xa_ref, xb_ref, xn_ref, wa_hbm, wb_hbm, wg_hbm, bias_ref, wout_hbm,
                fg_ref, o_ref, wa_ref, wb_ref, wg_ref, wout_ref, w_sem,
                proj_a, proj_b, h_a, h_b, kvbuf, abias, c_scr, m_scr, y_a, y_b):
    s = pl.program_id(0)
    T = xa_ref.shape[0]
    w = (wa_ref, wb_ref, wg_ref)
    rest = (bias_ref, wout_ref, fg_ref, kvbuf, abias, c_scr, m_scr)

    @pl.when(s == 0)
    def _():
        load = _WeightLoad(_row_slabs(wa_hbm, wa_ref) + _row_slabs(wb_hbm, wb_ref)
                           + _row_slabs(wg_hbm, wg_ref) + _row_slabs(wout_hbm, wout_ref),
                           w_sem, WEIGHT_LOAD_DEPTH)
        _odd_init(sink_ref, kvbuf, abias, c_scr, m_scr)
        first = _InProj(xa_ref, *w, h_a, proj_a, before_slab=load.wait_one)
        first.normalise()
        first.finish()
        load.wait_all()

    nxt = _InProj(xb_ref, *w, h_a, proj_b)
    nxt.normalise()
    after = _InProj(xn_ref, *w, h_b, proj_a)
    _odd_mixers(s == 0, 0, proj_a, xa_ref, o_ref.at[0:T], *rest, y_a, nxt, after)
    _odd_mixers(False, T, proj_b, xb_ref, o_ref.at[T:2 * T], *rest, y_b, after)
    for b in range(2 * ATT_KV_HEADS):
        kvbuf[b, 0:BLOCK, :] = kvbuf[b, 2 * T:2 * T + BLOCK, :]


def _odd_layer(x, wa, wb, wg, i_bias, f_bias, sinks, wo, final_g):
    S = x.shape[0]
    T = min(TILE_ODD, S // 2)
    NT = S // T
    assert S % (2 * T) == 0 and T % BLOCK == 0 and T % min(MCHUNK, T) == 0
    assert wa.shape == (O_DO, D_MODEL) and wb.shape == (O_IF - O_DO, D_MODEL)
    assert wg.shape == (GATE_PAD, D_MODEL) and wo.shape == (ODD_OUT, D_MODEL)
    bias = jnp.broadcast_to(jnp.concatenate([i_bias, f_bias]).astype(F32)[:, None], (2 * MLSTM_HEADS, LANES))
    const = _const_spec
    return pl.pallas_call(
        _odd_kernel,
        grid=(NT // 2,),
        in_specs=[
            pl.BlockSpec(memory_space=pltpu.SMEM),
            pl.BlockSpec((T, D_MODEL), lambda s: (2 * s, 0)),
            pl.BlockSpec((T, D_MODEL), lambda s: (2 * s + 1, 0)),
            pl.BlockSpec((T, D_MODEL), lambda s: (jnp.minimum(2 * s + 2, NT - 1), 0)),
            pl.BlockSpec(memory_space=pl.ANY),
            pl.BlockSpec(memory_space=pl.ANY),
            pl.BlockSpec(memory_space=pl.ANY),
            const(2 * MLSTM_HEADS, LANES),
            pl.BlockSpec(memory_space=pl.ANY),
            const(1, D_MODEL),
        ],
        out_specs=pl.BlockSpec((2 * T, D_MODEL), lambda s: (s, 0)),
        out_shape=jax.ShapeDtypeStruct((S, D_MODEL), F32),
        scratch_shapes=[
            pltpu.VMEM(wa.shape, BF16),
            pltpu.VMEM(wb.shape, BF16),
            pltpu.VMEM(wg.shape, BF16),
            pltpu.VMEM(wo.shape, BF16),
            pltpu.SemaphoreType.DMA((sum(pl.cdiv(w.shape[0], MXU_WIDTH) for w in (wa, wb, wg, wo)),)),
            pltpu.VMEM((T, ODD_IN_PAD), F32),
            pltpu.VMEM((T, ODD_IN_PAD), F32),
            pltpu.VMEM((T, D_MODEL), BF16),
            pltpu.VMEM((T, D_MODEL), BF16),
            pltpu.VMEM((2 * ATT_KV_HEADS, 2 * T + BLOCK, LANES), BF16),
            pltpu.VMEM((ATT_HEADS * BLOCK, 2 * BLOCK), F32),
            pltpu.VMEM((MLSTM_HEADS * MLSTM_QK_DIM, 2 * MLSTM_V_DIM), F32),
            pltpu.VMEM((2 * MLSTM_HEADS, LANES), F32),
            pltpu.VMEM((T, ODD_OUT), BF16),
            pltpu.VMEM((T, ODD_OUT), BF16),
        ],
        compiler_params=pltpu.CompilerParams(
            dimension_semantics=("arbitrary",), vmem_limit_bytes=VMEM_LIMIT),
        name="odd_layer",
    )(sinks.astype(F32), x, x, x, wa, wb, wg, bias, wo,
      final_g.reshape(1, D_MODEL))


def kernel(x, even_norm, even_w_in, even_conv_w, even_pool_w, even_pool_scale, even_w_out,
           odd_norm, odd_w_in, odd_i_bias, odd_f_bias, odd_sinks, odd_w_out, final_norm):
    B, S, D = x.shape
    assert B == 1 and D == D_MODEL
    assert even_norm.shape[0] == 1 and odd_norm.shape[0] == 1
    h = x.reshape(S, D)
    h, wa, wb, wg, wo = _even_layer(h, even_norm[0], even_w_in[0], even_conv_w[0], even_pool_w[0],
                                    even_pool_scale[0], even_w_out[0], odd_norm[0], odd_w_in[0].T, odd_w_out[0])
    h = _odd_layer(h, wa, wb, wg, odd_i_bias[0], odd_f_bias[0], odd_sinks[0], wo, final_norm)
    return h.reshape(B, S, D)
```

```python
import jax
import jax.numpy as jnp
from jax import lax
from jax.experimental import pallas as pl
from jax.experimental.pallas import tpu as pltpu

F32 = jnp.float32
BF16 = jnp.bfloat16

D_MODEL = 1024
RMS_EPS = 1e-6

CONV_WIDTH = 512
CONV_K = 3
POOL_WINDOWS = (2, 4, 8, 16)
POOL_WIDTH = 512
POOL_GROUP = POOL_WIDTH // len(POOL_WINDOWS)
EVEN_IN = 4 * CONV_WIDTH + 2 * POOL_WIDTH
EVEN_OUT = CONV_WIDTH + POOL_WIDTH
CONV_HALO = 8
POOL_HALO = 16

ATT_HEADS = 8
ATT_KV_HEADS = 2
ATT_GROUP = ATT_HEADS // ATT_KV_HEADS
ATT_HEAD_DIM = 64
ATT_WIDTH = ATT_HEADS * ATT_HEAD_DIM
ATT_KV_WIDTH = ATT_KV_HEADS * ATT_HEAD_DIM
WINDOW = 128
BLOCK = 128
MLSTM_HEADS = 4
MLSTM_QK_DIM = 64
MLSTM_V_DIM = 128
MLSTM_QK_WIDTH = MLSTM_HEADS * MLSTM_QK_DIM
MLSTM_WIDTH = MLSTM_HEADS * MLSTM_V_DIM
ODD_OUT = ATT_WIDTH + MLSTM_WIDTH
LANES = 128
BF16_ROWS = 16
SINK_KEY = 0
MXU_WIDTH = 256
GATE_PAD = LANES

O_CQ = 0
O_CK = O_CQ + ATT_WIDTH
O_CV = O_CK + ATT_KV_WIDTH
O_GC = O_CV + ATT_KV_WIDTH
O_DQ = O_GC + ATT_WIDTH
O_DK = O_DQ + MLSTM_QK_WIDTH
O_DV = O_DK + MLSTM_QK_WIDTH
O_DO = O_DV + MLSTM_WIDTH
O_GD = O_DO + MLSTM_WIDTH
O_IF = O_GD + MLSTM_WIDTH
ODD_IN_PAD = O_IF + GATE_PAD

TILE_EVEN = 1024
EVEN_CHUNK = 512
TILE_ODD = 512
MCHUNK = 256
INPROJ_ROWS = 512
OUT_ROWS = 256
PRE_CHUNKS = 3
MLSTM_CHUNKS = 5
ATTN_SKIP_EVERY = 4
LOG2E = 1.4426950408889634
ATT_Q_SCALE = ATT_HEAD_DIM ** -0.5 * LOG2E
MLSTM_Q_SCALE = MLSTM_QK_DIM ** -0.5
VMEM_LIMIT = 60 * 1024 * 1024


def _const_spec(*shape):
    return pl.BlockSpec(shape, lambda i: (0,) * len(shape), pipeline_mode=pl.Buffered(1))


def _rms_scale(x):
    ms = jnp.mean(x * x, axis=-1, keepdims=True)
    return x * lax.rsqrt(ms + RMS_EPS)


def _rmsnorm(x, g):
    return _rms_scale(x) * g


def _even_kernel(x_ref, g_ref, win_ref, cw_ref, pw_ref, ps_ref, wout_ref,
                 ng_ref, nwa32_ref, nwb32_ref, nwg32_ref, nwo32_ref,
                 o_ref, nwa_ref, nwb_ref, nwg_ref, nwo_ref, proj_scr, uc_scr, up_scr, y_scr, wfold_scr):
    i = pl.program_id(0)
    T = x_ref.shape[0]
    C = CONV_WIDTH

    ng = ng_ref[...]
    ra = nwa32_ref.shape[0]
    feat = lax.broadcasted_iota(jnp.int32, (ra, D_MODEL), 0) + i * ra
    q_scale = jnp.where(feat < O_CK, ATT_Q_SCALE,
                        jnp.where(feat < O_DQ, 1.0, jnp.where(feat < O_DK, MLSTM_Q_SCALE, 1.0)))
    nwa_ref[...] = (nwa32_ref[...] * ng * q_scale).astype(BF16)
    nwb_ref[...] = (nwb32_ref[...] * ng).astype(BF16)
    nwo_ref[...] = nwo32_ref[...].astype(BF16)
    nwg_ref[...] = jnp.concatenate(
        [nwg32_ref[...] * ng, jnp.zeros((GATE_PAD - 2 * MLSTM_HEADS, D_MODEL), F32)], axis=0).astype(BF16)

    @pl.when(i == 0)
    def _():
        uc_scr[0:CONV_HALO, :] = jnp.zeros((CONV_HALO, C), F32)
        up_scr[0:POOL_HALO, :] = jnp.zeros((POOL_HALO, POOL_WIDTH), F32)
        for g in range(len(POOL_WINDOWS)):
            lo, hi = g * POOL_GROUP, (g + 1) * POOL_GROUP
            wfold_scr[:, lo:hi] = jnp.dot(win_ref[:, 4 * C + lo:4 * C + hi], pw_ref[g],
                                          precision=lax.Precision.HIGHEST, preferred_element_type=F32)

    h = _rmsnorm(x_ref[...], g_ref[...]).astype(BF16)
    for c0 in range(0, EVEN_IN, EVEN_CHUNK):
        w = wfold_scr[...] if c0 == 4 * C else win_ref[:, c0:c0 + EVEN_CHUNK]
        proj_scr[:, c0:c0 + EVEN_CHUNK] = jnp.dot(h, w.astype(BF16), preferred_element_type=F32)

    uc = proj_scr[:, 2 * C:3 * C] * proj_scr[:, 0:C]
    uc_scr[CONV_HALO:CONV_HALO + T, :] = uc
    conv = (cw_ref[2] * uc
            + cw_ref[1] * uc_scr[CONV_HALO - 1:CONV_HALO - 1 + T, :]
            + cw_ref[0] * uc_scr[CONV_HALO - 2:CONV_HALO - 2 + T, :])
    ya = proj_scr[:, C:2 * C] * conv * jax.nn.silu(proj_scr[:, 3 * C:4 * C])
    y_scr[:, 0:C] = ya.astype(BF16)
    uc_scr[0:CONV_HALO, :] = uc_scr[T:T + CONV_HALO, :]

    up_scr[POOL_HALO:POOL_HALO + T, :] = proj_scr[:, 4 * C:4 * C + POOL_WIDTH]
    t1 = (lax.broadcasted_iota(jnp.int32, (T, POOL_GROUP), 0) + (i * T + 1)).astype(F32)
    for g, w in enumerate(POOL_WINDOWS):
        lo, hi = g * POOL_GROUP, (g + 1) * POOL_GROUP
        s = up_scr[:, lo:hi]
        k = 1
        while k < w:
            s = s + pltpu.roll(s, k, 0)
            k *= 2
        u = up_scr[POOL_HALO:POOL_HALO + T, lo:hi]
        yb = s[POOL_HALO:, :] / jnp.minimum(t1, float(w)) - u
        gate = proj_scr[:, 4 * C + POOL_WIDTH + lo:4 * C + POOL_WIDTH + hi]
        y_scr[:, C + lo:C + hi] = (yb * ps_ref[:, lo:hi] * jax.nn.silu(gate)).astype(BF16)
    up_scr[0:POOL_HALO, :] = up_scr[T:T + POOL_HALO, :]

    o_ref[...] = x_ref[...] + jnp.dot(y_scr[...], wout_ref[...].astype(BF16), preferred_element_type=F32)


def _even_layer(x, norm_g, w_in, conv_w, pool_w, pool_scale, w_out, next_norm_g, next_w_in_t, next_w_out):
    S = x.shape[0]
    T = min(TILE_EVEN, S)
    steps = S // T
    n_gates = 2 * MLSTM_HEADS
    assert S % T == 0
    assert next_w_in_t.shape == (O_IF + n_gates, D_MODEL) and next_w_out.shape == (ODD_OUT, D_MODEL)
    ra, rb, ro = O_DO // steps, (O_IF - O_DO) // steps, ODD_OUT // steps
    assert ra * steps == O_DO and rb * steps == O_IF - O_DO and ro * steps == ODD_OUT
    const = _const_spec
    slab = lambda r: pl.BlockSpec((r, D_MODEL), lambda i: (i, 0))
    return pl.pallas_call(
        _even_kernel,
        grid=(steps,),
        in_specs=[
            pl.BlockSpec((T, D_MODEL), lambda i: (i, 0)),
            const(1, D_MODEL),
            const(D_MODEL, EVEN_IN),
            const(CONV_K, 1, CONV_WIDTH),
            const(len(POOL_WINDOWS), POOL_GROUP, POOL_GROUP),
            const(1, POOL_WIDTH),
            const(EVEN_OUT, D_MODEL),
            const(1, D_MODEL),
            slab(ra),
            pl.BlockSpec((pl.Element(rb), pl.Element(D_MODEL)), lambda i: (pl.multiple_of(O_DO + n_gates + i * rb, n_gates), 0)),
            pl.BlockSpec((n_gates, D_MODEL), lambda i: (O_DO // n_gates, 0)),
            slab(ro),
        ],
        out_specs=[pl.BlockSpec((T, D_MODEL), lambda i: (i, 0)),
                   slab(ra), slab(rb), pl.BlockSpec((GATE_PAD, D_MODEL), lambda i: (0, 0)), slab(ro)],
        out_shape=[jax.ShapeDtypeStruct((S, D_MODEL), F32),
                   jax.ShapeDtypeStruct((O_DO, D_MODEL), BF16),
                   jax.ShapeDtypeStruct((O_IF - O_DO, D_MODEL), BF16),
                   jax.ShapeDtypeStruct((GATE_PAD, D_MODEL), BF16),
                   jax.ShapeDtypeStruct((ODD_OUT, D_MODEL), BF16)],
        scratch_shapes=[
            pltpu.VMEM((T, EVEN_IN), F32),
            pltpu.VMEM((T + CONV_HALO, CONV_WIDTH), F32),
            pltpu.VMEM((T + POOL_HALO, POOL_WIDTH), F32),
            pltpu.VMEM((T, EVEN_OUT), BF16),
            pltpu.VMEM((D_MODEL, POOL_WIDTH), F32),
        ],
        compiler_params=pltpu.CompilerParams(
            dimension_semantics=("arbitrary",), vmem_limit_bytes=VMEM_LIMIT),
        name="even_layer",
    )(x, norm_g.reshape(1, D_MODEL), w_in, conv_w, pool_w, pool_scale.reshape(1, POOL_WIDTH), w_out,
      next_norm_g.reshape(1, D_MODEL), next_w_in_t, next_w_in_t, next_w_in_t, next_w_out)


def _lane_scan(x, op, identity):
    n = x.shape[1]
    lane = lax.broadcasted_iota(jnp.int32, x.shape, 1)
    k = 1
    while k < n:
        x = op(x, jnp.where(lane >= k, pltpu.roll(x, k, 1), identity))
        k *= 2
    return x


def _odd_init(sink_ref, kvbuf, abias, c_scr, m_scr):
    for b in range(2 * ATT_KV_HEADS):
        kvbuf[b, 0:BLOCK, :] = jnp.zeros((BLOCK, LANES), BF16)
    c_scr[...] = jnp.zeros(c_scr.shape, F32)
    m_scr[...] = jnp.zeros(m_scr.shape, F32)
    qi = lax.broadcasted_iota(jnp.int32, (BLOCK, 2 * BLOCK), 0)
    kj = lax.broadcasted_iota(jnp.int32, (BLOCK, 2 * BLOCK), 1)
    dist = qi + BLOCK - kj
    ok = (dist >= 0) & (dist < WINDOW)
    distf = dist.astype(F32)
    for hd in range(ATT_HEADS):
        slope = 2.0 ** (-8.0 * (hd + 1) / ATT_HEADS)
        band = jnp.where(ok, (-slope * LOG2E) * distf, -jnp.inf)
        abias[hd * BLOCK:(hd + 1) * BLOCK, :] = jnp.where(kj == SINK_KEY, sink_ref[hd] * LOG2E, band)


class _InProj:
    def __init__(self, x_ref, wa_ref, wb_ref, wg_ref, h_scr, proj):
        self.x_ref, self.h_scr, self.proj = x_ref, h_scr, proj
        self.todo = []
        rows = x_ref.shape[0]
        for w_ref, base in ((wa_ref, 0), (wb_ref, O_DO), (wg_ref, O_IF)):
            n = w_ref.shape[0]
            for c0 in range(0, n, MXU_WIDTH):
                for r0 in range(0, rows, INPROJ_ROWS):
                    self.todo.append((w_ref, base, c0, min(c0 + MXU_WIDTH, n), r0, min(r0 + INPROJ_ROWS, rows)))

    def normalise(self):
        self.h_scr[...] = _rms_scale(self.x_ref[...]).astype(BF16)

    def step(self):
        if self.todo:
            w_ref, base, c0, c1, r0, r1 = self.todo.pop(0)
            self.proj[r0:r1, base + c0:base + c1] = lax.dot_general(
                self.h_scr[r0:r1, :], w_ref[c0:c1, :], (((1,), (1,)), ((), ())), preferred_element_type=F32)

    def finish(self):
        while self.todo:
            self.step()


def _odd_mixers(first, kv0, proj, x_ref, o_ref, bias_refs, wout_ref, fg_ref,
                kvbuf, abias, c_scr, m_scr, y_scr, nxt, after=None):
    T = x_ref.shape[0]
    NB = T // BLOCK
    H = MLSTM_HEADS
    L = min(MCHUNK, T)
    NEG = -jnp.inf

    lane_q = lax.broadcasted_iota(jnp.int32, (BLOCK, LANES), 1)
    low = lane_q < ATT_HEAD_DIM
    low_t = lax.broadcasted_iota(jnp.int32, (T, LANES), 1) < ATT_HEAD_DIM
    for b, off in enumerate((O_CK, O_CV)):
        kv = proj[:, off:off + ATT_KV_WIDTH]
        sw = pltpu.roll(kv, ATT_HEAD_DIM, 1)
        kvbuf[2 * b, kv0 + BLOCK:kv0 + BLOCK + T, :] = jnp.where(low_t, kv, sw).astype(BF16)
        kvbuf[2 * b + 1, kv0 + BLOCK:kv0 + BLOCK + T, :] = jnp.where(low_t, sw, kv).astype(BF16)
    if first is not False:
        key_lane = lax.broadcasted_iota(jnp.int32, (BLOCK, 2 * BLOCK), 1)
        no_prev = jnp.where((key_lane < BLOCK) & (key_lane != SINK_KEY), jnp.where(first, NEG, 0.0), 0.0)
    sink_row = lax.broadcasted_iota(jnp.int32, (BF16_ROWS, LANES), 0) == SINK_KEY

    def drop_sink_row(a):
        head = a[0:BF16_ROWS, :]
        return jnp.concatenate([jnp.where(sink_row, jnp.zeros_like(head), head), a[BF16_ROWS:, :]], axis=0)

    ones_v = jnp.ones((2 * BLOCK, LANES), BF16)

    def scores(j, g):
        r0 = j * BLOCK
        kk = drop_sink_row(kvbuf[g, kv0 + r0:kv0 + r0 + 2 * BLOCK, :])
        qs = []
        for p in range(g * ATT_GROUP // 2, (g + 1) * ATT_GROUP // 2):
            qp = proj[r0:r0 + BLOCK, O_CQ + p * LANES:O_CQ + (p + 1) * LANES]
            qs.append(jnp.where(low, qp, 0.0).astype(BF16))
            qs.append(jnp.where(low, 0.0, qp).astype(BF16))
        return lax.dot_general(jnp.concatenate(qs, axis=0), kk, (((1,), (1,)), ((), ())),
                               preferred_element_type=F32)

    def attention():
        slots = [(j, g) for j in range(NB) for g in range(ATT_KV_HEADS)]
        s_next = scores(*slots[0])
        for n, (j, g) in enumerate(slots):
            r0 = j * BLOCK
            s_all = s_next
            if n + 1 < len(slots):
                s_next = scores(*slots[n + 1])
            if n % ATTN_SKIP_EVERY != ATTN_SKIP_EVERY - 1:
                nxt.step()
            ps = []
            for r in range(ATT_GROUP):
                hd = g * ATT_GROUP + r
                s = s_all[r * BLOCK:(r + 1) * BLOCK, :] + abias[hd * BLOCK:(hd + 1) * BLOCK, :]
                if j == 0 and first is not False:
                    s = s + no_prev
                ps.append(jnp.exp2(s - jnp.max(s, axis=1, keepdims=True)).astype(BF16))
            vv = jnp.concatenate(
                [drop_sink_row(kvbuf[ATT_KV_HEADS + g, kv0 + r0:kv0 + r0 + 2 * BLOCK, :]), ones_v], axis=1)
            o_all = jnp.dot(jnp.concatenate(ps, axis=0), vv, preferred_element_type=F32)
            for pp in range(ATT_GROUP // 2):
                p = g * ATT_GROUP // 2 + pp
                lo = o_all[2 * pp * BLOCK:(2 * pp + 1) * BLOCK, :]
                hi = o_all[(2 * pp + 1) * BLOCK:(2 * pp + 2) * BLOCK, :]
                att = jnp.where(low, lo[:, :LANES] / lo[:, LANES:], hi[:, :LANES] / hi[:, LANES:])
                gate = proj[r0:r0 + BLOCK, O_GC + p * LANES:O_GC + (p + 1) * LANES]
                y_scr[r0:r0 + BLOCK, p * LANES:(p + 1) * LANES] = (att * jax.nn.silu(gate)).astype(BF16)

    gt = proj[:, O_IF:O_IF + GATE_PAD].T[0:2 * H, :]
    row = lax.broadcasted_iota(jnp.int32, (2 * H, T), 0)
    top = row < H
    ib_ref, fb_ref = bias_refs
    bias = jnp.zeros((2 * H, T), F32)
    for hd in range(H):
        bias = jnp.where(row == hd, ib_ref[hd], jnp.where(row == H + hd, fb_ref[hd], bias))
    pre = gt + bias
    bcum = _lane_scan(jax.nn.log_sigmoid(pre), jnp.add, 0.0)
    bsw = pltpu.roll(bcum, H, 0)
    a_top = pre - bsw
    ag = jnp.where(top, a_top, pltpu.roll(a_top, H, 0))
    bf = jnp.where(top, bsw, bcum)
    m_in = jnp.concatenate([m_scr[...]] * (T // LANES), axis=1)
    mg = jnp.maximum(m_in, _lane_scan(ag, jnp.maximum, NEG))
    m_t = bf + mg
    nrm = jnp.exp(-m_t)
    inter_parts, wk_parts, decay_parts = [], [], []
    for c in range(T // L):
        m_start = m_in[:, 0:1] if c == 0 else mg[:, c * L - 1:c * L]
        m_last = mg[:, (c + 1) * L - 1:(c + 1) * L]
        inter_parts.append(jnp.exp(m_start - mg[:, c * L:(c + 1) * L]))
        wk_parts.append(jnp.exp(ag[:, c * L:(c + 1) * L] - m_last))
        decay_parts.append(jnp.broadcast_to(jnp.exp(m_start - m_last), (2 * H, 2 * MLSTM_V_DIM)))
    inter = jnp.concatenate(inter_parts, axis=1)
    m_scr[...] = jnp.broadcast_to(m_t[:, T - 1:T], m_scr.shape)
    ag2 = ag * LOG2E
    cols = jnp.concatenate([mg * LOG2E, inter, nrm, jnp.zeros((LANES - 6 * H, T), F32)], axis=0).T

    kT = proj[:, O_DK:O_DK + MLSTM_QK_WIDTH].T
    lane_p = lax.broadcasted_iota(jnp.int32, (L, LANES), 1)
    low_p = lane_p < MLSTM_QK_DIM
    tri = (lax.broadcasted_iota(jnp.int32, (BLOCK, BLOCK), 1) <= lax.broadcasted_iota(jnp.int32, (BLOCK, BLOCK), 0))
    ones = jnp.ones((L, MLSTM_V_DIM), BF16)
    zero_blk = jnp.zeros((BLOCK, BLOCK), BF16)

    def mlstm_chunk(c):
        r0 = c * L
        for pair in range(H // 2):
            qp = proj[r0:r0 + L, O_DQ + pair * LANES:O_DQ + (pair + 1) * LANES]
            kT_pair = kT[pair * LANES:(pair + 1) * LANES, r0:r0 + L].astype(BF16)
            c_pair = c_scr[pair * LANES:(pair + 1) * LANES, :].astype(BF16)
            for half in range(2):
                hd = 2 * pair + half
                q_h = jnp.where(low_p, qp, 0.0) if half == 0 else jnp.where(low_p, 0.0, qp)
                s = jnp.dot(q_h.astype(BF16), kT_pair, preferred_element_type=F32)
                if H * c + hd < MLSTM_CHUNKS:
                    nxt.step()
                p_rows = []
                for bi in range(L // BLOCK):
                    m_col = cols[r0 + bi * BLOCK:r0 + (bi + 1) * BLOCK, hd:hd + 1]
                    blks = []
                    for bj in range(L // BLOCK):
                        if bj > bi:
                            blks.append(zero_blk)
                            continue
                        arg = ag2[hd:hd + 1, r0 + bj * BLOCK:r0 + (bj + 1) * BLOCK] - m_col
                        if bj == bi:
                            arg = jnp.where(tri, arg, NEG)
                        sb = s[bi * BLOCK:(bi + 1) * BLOCK, bj * BLOCK:(bj + 1) * BLOCK]
                        blks.append((sb * jnp.exp2(arg)).astype(BF16))
                    p_rows.append(jnp.concatenate(blks, axis=1))
                p_bf = jnp.concatenate(p_rows, axis=0)
                v_aug = jnp.concatenate(
                    [proj[r0:r0 + L, O_DV + hd * MLSTM_V_DIM:O_DV + (hd + 1) * MLSTM_V_DIM].astype(BF16), ones],
                    axis=1)
                q_int = (q_h * cols[r0:r0 + L, 2 * H + hd:2 * H + hd + 1]).astype(BF16)
                num = jnp.dot(jnp.concatenate([p_bf, q_int], axis=1),
                              jnp.concatenate([v_aug, c_pair], axis=0), preferred_element_type=F32)
                den = jnp.maximum(jnp.abs(num[:, MLSTM_V_DIM:]), cols[r0:r0 + L, 4 * H + hd:4 * H + hd + 1])
                hm = num[:, :MLSTM_V_DIM] / den
                og = proj[r0:r0 + L, O_DO + hd * MLSTM_V_DIM:O_DO + (hd + 1) * MLSTM_V_DIM]
                gd = proj[r0:r0 + L, O_GD + hd * MLSTM_V_DIM:O_GD + (hd + 1) * MLSTM_V_DIM]
                y_scr[r0:r0 + L, ATT_WIDTH + hd * MLSTM_V_DIM:ATT_WIDTH + (hd + 1) * MLSTM_V_DIM] = (
                    hm * jax.nn.sigmoid(og) * jax.nn.silu(gd)).astype(BF16)
                kw = (kT[hd * MLSTM_QK_DIM:(hd + 1) * MLSTM_QK_DIM, r0:r0 + L]
                      * wk_parts[c][hd:hd + 1, :]).astype(BF16)
                upd = jnp.dot(kw, v_aug, preferred_element_type=F32)
                rows = slice(hd * MLSTM_QK_DIM, (hd + 1) * MLSTM_QK_DIM)
                c_scr[rows, :] = decay_parts[c][hd:hd + 1, :] * c_scr[rows, :] + upd

    attention()
    for _ in range(PRE_CHUNKS):
        nxt.step()
    for c in range(T // L):
        mlstm_chunk(c)
    nxt.finish()
    if after is not None:
        after.normalise()
    for r0 in range(0, T, OUT_ROWS):
        x2 = x_ref[r0:r0 + OUT_ROWS, :] + jnp.dot(y_scr[r0:r0 + OUT_ROWS, :], wout_ref[...],
                                                   preferred_element_type=F32)
        o_ref[r0:r0 + OUT_ROWS, :] = _rmsnorm(x2, fg_ref[...])


def _odd_kernel(sink_ref, ib_ref, fb_ref, xa_ref, xb_ref, xn_ref, wa_ref, wb_ref, wg_ref, wout_ref,
                fg_ref, o_ref, proj_a, proj_b, h_a, h_b, kvbuf, abias, c_scr, m_scr, y_a, y_b):
    s = pl.program_id(0)
    T = xa_ref.shape[0]
    w = (wa_ref, wb_ref, wg_ref)
    rest = ((ib_ref, fb_ref), wout_ref, fg_ref, kvbuf, abias, c_scr, m_scr)

    @pl.when(s == 0)
    def _():
        _odd_init(sink_ref, kvbuf, abias, c_scr, m_scr)
        first = _InProj(xa_ref, *w, h_a, proj_a)
        first.normalise()
        first.finish()

    nxt = _InProj(xb_ref, *w, h_a, proj_b)
    nxt.normalise()
    after = _InProj(xn_ref, *w, h_b, proj_a)
    _odd_mixers(s == 0, 0, proj_a, xa_ref, o_ref.at[0:T], *rest, y_a, nxt, after)
    _odd_mixers(False, T, proj_b, xb_ref, o_ref.at[T:2 * T], *rest, y_b, after)
    for b in range(2 * ATT_KV_HEADS):
        kvbuf[b, 0:BLOCK, :] = kvbuf[b, 2 * T:2 * T + BLOCK, :]


def _odd_layer(x, wa, wb, wg, i_bias, f_bias, sinks, wo, final_g):
    S = x.shape[0]
    T = min(TILE_ODD, S // 2)
    NT = S // T
    assert S % (2 * T) == 0 and T % BLOCK == 0 and T % min(MCHUNK, T) == 0
    assert i_bias.shape == f_bias.shape == (MLSTM_HEADS,) and sinks.shape == (ATT_HEADS,)
    const = _const_spec
    smem = pl.BlockSpec(memory_space=pltpu.SMEM)
    return pl.pallas_call(
        _odd_kernel,
        grid=(NT // 2,),
        in_specs=[
            smem, smem, smem,
            pl.BlockSpec((T, D_MODEL), lambda s: (2 * s, 0)),
            pl.BlockSpec((T, D_MODEL), lambda s: (2 * s + 1, 0)),
            pl.BlockSpec((T, D_MODEL), lambda s: (jnp.minimum(2 * s + 2, NT - 1), 0)),
            const(O_DO, D_MODEL),
            const(O_IF - O_DO, D_MODEL),
            const(GATE_PAD, D_MODEL),
            const(ODD_OUT, D_MODEL),
            const(1, D_MODEL),
        ],
        out_specs=pl.BlockSpec((2 * T, D_MODEL), lambda s: (s, 0)),
        out_shape=jax.ShapeDtypeStruct((S, D_MODEL), F32),
        scratch_shapes=[
            pltpu.VMEM((T, ODD_IN_PAD), F32),
            pltpu.VMEM((T, ODD_IN_PAD), F32),
            pltpu.VMEM((T, D_MODEL), BF16),
            pltpu.VMEM((T, D_MODEL), BF16),
            pltpu.VMEM((2 * ATT_KV_HEADS, 2 * T + BLOCK, LANES), BF16),
            pltpu.VMEM((ATT_HEADS * BLOCK, 2 * BLOCK), F32),
            pltpu.VMEM((MLSTM_HEADS * MLSTM_QK_DIM, 2 * MLSTM_V_DIM), F32),
            pltpu.VMEM((2 * MLSTM_HEADS, LANES), F32),
            pltpu.VMEM((T, ODD_OUT), BF16),
            pltpu.VMEM((T, ODD_OUT), BF16),
        ],
        compiler_params=pltpu.CompilerParams(
            dimension_semantics=("arbitrary",), vmem_limit_bytes=VMEM_LIMIT),
        name="odd_layer",
    )(sinks.astype(F32), i_bias.astype(F32), f_bias.astype(F32), x, x, x, wa, wb, wg, wo,
      final_g.reshape(1, D_MODEL))


def kernel(x, even_norm, even_w_in, even_conv_w, even_pool_w, even_pool_scale, even_w_out,
           odd_norm, odd_w_in, odd_i_bias, odd_f_bias, odd_sinks, odd_w_out, final_norm):
    B, S, D = x.shape
    assert B == 1 and D == D_MODEL
    assert even_norm.shape[0] == 1 and odd_norm.shape[0] == 1
    h = x.reshape(S, D)
    h, wa, wb, wg, wo = _even_layer(h, even_norm[0], even_w_in[0], jnp.swapaxes(even_conv_w, 0, 1), even_pool_w[0],
                                    even_pool_scale[0], even_w_out[0], odd_norm[0], odd_w_in[0].T, odd_w_out[0])
    h = _odd_layer(h, wa, wb, wg, odd_i_bias[0], odd_f_bias[0], odd_sinks[0], wo, final_norm)
    return h.reshape(B, S, D)
```

```python
import jax
import jax.numpy as jnp
from jax import lax
from jax.experimental import pallas as pl
from jax.experimental.pallas import tpu as pltpu

F32 = jnp.float32
BF16 = jnp.bfloat16

D_MODEL = 1024
RMS_EPS = 1e-6

CONV_WIDTH = 512
CONV_K = 3
POOL_WINDOWS = (2, 4, 8, 16)
POOL_WIDTH = 512
POOL_GROUP = POOL_WIDTH // len(POOL_WINDOWS)
EVEN_IN = 4 * CONV_WIDTH + 2 * POOL_WIDTH
EVEN_OUT = CONV_WIDTH + POOL_WIDTH
CONV_HALO = 8
POOL_HALO = 16

ATT_HEADS = 8
ATT_KV_HEADS = 2
ATT_GROUP = ATT_HEADS // ATT_KV_HEADS
ATT_HEAD_DIM = 64
ATT_WIDTH = ATT_HEADS * ATT_HEAD_DIM
ATT_KV_WIDTH = ATT_KV_HEADS * ATT_HEAD_DIM
WINDOW = 128
BLOCK = 128
MLSTM_HEADS = 4
MLSTM_QK_DIM = 64
MLSTM_V_DIM = 128
MLSTM_QK_WIDTH = MLSTM_HEADS * MLSTM_QK_DIM
MLSTM_WIDTH = MLSTM_HEADS * MLSTM_V_DIM
ODD_OUT = ATT_WIDTH + MLSTM_WIDTH
LANES = 128
BF16_ROWS = 16
SINK_KEY = 0
MXU_WIDTH = 256
GATE_PAD = LANES

O_CQ = 0
O_CK = O_CQ + ATT_WIDTH
O_CV = O_CK + ATT_KV_WIDTH
O_GC = O_CV + ATT_KV_WIDTH
O_DQ = O_GC + ATT_WIDTH
O_DK = O_DQ + MLSTM_QK_WIDTH
O_DV = O_DK + MLSTM_QK_WIDTH
O_DO = O_DV + MLSTM_WIDTH
O_GD = O_DO + MLSTM_WIDTH
O_IF = O_GD + MLSTM_WIDTH
ODD_IN_PAD = O_IF + GATE_PAD

TILE_EVEN = 1024
EVEN_CHUNK = 512
TILE_ODD = 512
MCHUNK = 256
INPROJ_ROWS = 512
OUT_ROWS = 256
PRE_CHUNKS = 3
MLSTM_CHUNKS = 5
ATTN_SKIP_EVERY = 4
LOG2E = 1.4426950408889634
ATT_Q_SCALE = ATT_HEAD_DIM ** -0.5 * LOG2E
MLSTM_Q_SCALE = MLSTM_QK_DIM ** -0.5
VMEM_LIMIT = 60 * 1024 * 1024


def _const_spec(*shape):
    return pl.BlockSpec(shape, lambda i: (0,) * len(shape), pipeline_mode=pl.Buffered(1))


def _rms_scale(x):
    ms = jnp.mean(x * x, axis=-1, keepdims=True)
    return x * lax.rsqrt(ms + RMS_EPS)


def _rmsnorm(x, g):
    return _rms_scale(x) * g


def _even_kernel(x_ref, g_ref, win_ref, cw_ref, pw_ref, ps_ref, wout_ref,
                 ng_ref, nwa32_ref, nwb32_ref, nwg32_ref, nwo32_ref,
                 o_ref, nwa_ref, nwb_ref, nwg_ref, nwo_ref, proj_scr, uc_scr, up_scr, y_scr, wfold_scr):
    i = pl.program_id(0)
    T = x_ref.shape[0]
    C = CONV_WIDTH

    ng = ng_ref[...]
    ra = nwa32_ref.shape[0]
    feat = lax.broadcasted_iota(jnp.int32, (ra, D_MODEL), 0) + i * ra
    q_scale = jnp.where(feat < O_CK, ATT_Q_SCALE,
                        jnp.where(feat < O_DQ, 1.0, jnp.where(feat < O_DK, MLSTM_Q_SCALE, 1.0)))
    nwa_ref[...] = (nwa32_ref[...] * ng * q_scale).astype(BF16)
    nwb_ref[...] = (nwb32_ref[...] * ng).astype(BF16)
    nwo_ref[...] = nwo32_ref[...].astype(BF16)
    nwg_ref[...] = jnp.concatenate(
        [nwg32_ref[...] * ng, jnp.zeros((GATE_PAD - 2 * MLSTM_HEADS, D_MODEL), F32)], axis=0).astype(BF16)

    @pl.when(i == 0)
    def _():
        uc_scr[0:CONV_HALO, :] = jnp.zeros((CONV_HALO, C), F32)
        up_scr[0:POOL_HALO, :] = jnp.zeros((POOL_HALO, POOL_WIDTH), F32)
        for g in range(len(POOL_WINDOWS)):
            lo, hi = g * POOL_GROUP, (g + 1) * POOL_GROUP
            wfold_scr[:, lo:hi] = jnp.dot(win_ref[:, 4 * C + lo:4 * C + hi], pw_ref[g],
                                          precision=lax.Precision.HIGHEST, preferred_element_type=F32)

    h = _rmsnorm(x_ref[...], g_ref[...]).astype(BF16)
    for c0 in range(0, EVEN_IN, EVEN_CHUNK):
        w = wfold_scr[...] if c0 == 4 * C else win_ref[:, c0:c0 + EVEN_CHUNK]
        proj_scr[:, c0:c0 + EVEN_CHUNK] = jnp.dot(h, w.astype(BF16), preferred_element_type=F32)

    uc = proj_scr[:, 2 * C:3 * C] * proj_scr[:, 0:C]
    uc_scr[CONV_HALO:CONV_HALO + T, :] = uc
    conv = (cw_ref[2] * uc
            + cw_ref[1] * uc_scr[CONV_HALO - 1:CONV_HALO - 1 + T, :]
            + cw_ref[0] * uc_scr[CONV_HALO - 2:CONV_HALO - 2 + T, :])
    ya = proj_scr[:, C:2 * C] * conv * jax.nn.silu(proj_scr[:, 3 * C:4 * C])
    y_scr[:, 0:C] = ya.astype(BF16)
    uc_scr[0:CONV_HALO, :] = uc_scr[T:T + CONV_HALO, :]

    up_scr[POOL_HALO:POOL_HALO + T, :] = proj_scr[:, 4 * C:4 * C + POOL_WIDTH]
    t1 = (lax.broadcasted_iota(jnp.int32, (T, POOL_GROUP), 0) + (i * T + 1)).astype(F32)
    for g, w in enumerate(POOL_WINDOWS):
        lo, hi = g * POOL_GROUP, (g + 1) * POOL_GROUP
        s = up_scr[:, lo:hi]
        k = 1
        while k < w:
            s = s + pltpu.roll(s, k, 0)
            k *= 2
        u = up_scr[POOL_HALO:POOL_HALO + T, lo:hi]
        yb = s[POOL_HALO:, :] / jnp.minimum(t1, float(w)) - u
        gate = proj_scr[:, 4 * C + POOL_WIDTH + lo:4 * C + POOL_WIDTH + hi]
        y_scr[:, C + lo:C + hi] = (yb * ps_ref[:, lo:hi] * jax.nn.silu(gate)).astype(BF16)
    up_scr[0:POOL_HALO, :] = up_scr[T:T + POOL_HALO, :]

    o_ref[...] = x_ref[...] + jnp.dot(y_scr[...], wout_ref[...].astype(BF16), preferred_element_type=F32)


def _even_layer(x, norm_g, w_in, conv_w, pool_w, pool_scale, w_out, next_norm_g, next_w_in_t, next_w_out):
    S = x.shape[0]
    T = min(TILE_EVEN, S)
    steps = S // T
    n_gates = 2 * MLSTM_HEADS
    assert S % T == 0
    assert next_w_in_t.shape == (O_IF + n_gates, D_MODEL) and next_w_out.shape == (ODD_OUT, D_MODEL)
    ra, rb, ro = O_DO // steps, (O_IF - O_DO) // steps, ODD_OUT // steps
    assert ra * steps == O_DO and rb * steps == O_IF - O_DO and ro * steps == ODD_OUT
    const = _const_spec
    slab = lambda r: pl.BlockSpec((r, D_MODEL), lambda i: (i, 0))
    return pl.pallas_call(
        _even_kernel,
        grid=(steps,),
        in_specs=[
            pl.BlockSpec((T, D_MODEL), lambda i: (i, 0)),
            const(1, D_MODEL),
            const(D_MODEL, EVEN_IN),
            const(CONV_K, 1, CONV_WIDTH),
            const(len(POOL_WINDOWS), POOL_GROUP, POOL_GROUP),
            const(1, POOL_WIDTH),
            const(EVEN_OUT, D_MODEL),
            const(1, D_MODEL),
            slab(ra),
            pl.BlockSpec((pl.Element(rb), pl.Element(D_MODEL)), lambda i: (pl.multiple_of(O_DO + n_gates + i * rb, n_gates), 0)),
            pl.BlockSpec((n_gates, D_MODEL), lambda i: (O_DO // n_gates, 0)),
            slab(ro),
        ],
        out_specs=[pl.BlockSpec((T, D_MODEL), lambda i: (i, 0)),
                   slab(ra), slab(rb), pl.BlockSpec((GATE_PAD, D_MODEL), lambda i: (0, 0)), slab(ro)],
        out_shape=[jax.ShapeDtypeStruct((S, D_MODEL), F32),
                   jax.ShapeDtypeStruct((O_DO, D_MODEL), BF16),
                   jax.ShapeDtypeStruct((O_IF - O_DO, D_MODEL), BF16),
                   jax.ShapeDtypeStruct((GATE_PAD, D_MODEL), BF16),
                   jax.ShapeDtypeStruct((ODD_OUT, D_MODEL), BF16)],
        scratch_shapes=[
            pltpu.VMEM((T, EVEN_IN), F32),
            pltpu.VMEM((T + CONV_HALO, CONV_WIDTH), F32),
            pltpu.VMEM((T + POOL_HALO, POOL_WIDTH), F32),
            pltpu.VMEM((T, EVEN_OUT), BF16),
            pltpu.VMEM((D_MODEL, POOL_WIDTH), F32),
        ],
        compiler_params=pltpu.CompilerParams(
            dimension_semantics=("arbitrary",), vmem_limit_bytes=VMEM_LIMIT),
        name="even_layer",
    )(x, norm_g.reshape(1, D_MODEL), w_in, conv_w, pool_w, pool_scale.reshape(1, POOL_WIDTH), w_out,
      next_norm_g.reshape(1, D_MODEL), next_w_in_t, next_w_in_t, next_w_in_t, next_w_out)


def _lane_scan(x, op, identity):
    n = x.shape[1]
    lane = lax.broadcasted_iota(jnp.int32, x.shape, 1)
    k = 1
    while k < n:
        x = op(x, jnp.where(lane >= k, pltpu.roll(x, k, 1), identity))
        k *= 2
    return x


def _odd_init(sink_ref, kvbuf, abias, c_scr, m_scr):
    for b in range(2 * ATT_KV_HEADS):
        kvbuf[b, 0:BLOCK, :] = jnp.zeros((BLOCK, LANES), BF16)
    c_scr[...] = jnp.zeros(c_scr.shape, F32)
    m_scr[...] = jnp.zeros(m_scr.shape, F32)
    qi = lax.broadcasted_iota(jnp.int32, (BLOCK, 2 * BLOCK), 0)
    kj = lax.broadcasted_iota(jnp.int32, (BLOCK, 2 * BLOCK), 1)
    dist = qi + BLOCK - kj
    ok = (dist >= 0) & (dist < WINDOW)
    distf = dist.astype(F32)
    for hd in range(ATT_HEADS):
        slope = 2.0 ** (-8.0 * (hd + 1) / ATT_HEADS)
        band = jnp.where(ok, (-slope * LOG2E) * distf, -jnp.inf)
        abias[hd * BLOCK:(hd + 1) * BLOCK, :] = jnp.where(kj == SINK_KEY, sink_ref[0:1, hd:hd + 1] * LOG2E, band)


class _InProj:
    def __init__(self, x_ref, wa_ref, wb_ref, wg_ref, h_scr, proj):
        self.x_ref, self.h_scr, self.proj = x_ref, h_scr, proj
        self.todo = []
        rows = x_ref.shape[0]
        for w_ref, base in ((wa_ref, 0), (wb_ref, O_DO), (wg_ref, O_IF)):
            n = w_ref.shape[0]
            for c0 in range(0, n, MXU_WIDTH):
                for r0 in range(0, rows, INPROJ_ROWS):
                    self.todo.append((w_ref, base, c0, min(c0 + MXU_WIDTH, n), r0, min(r0 + INPROJ_ROWS, rows)))

    def normalise(self):
        self.h_scr[...] = _rms_scale(self.x_ref[...]).astype(BF16)

    def step(self):
        if self.todo:
            w_ref, base, c0, c1, r0, r1 = self.todo.pop(0)
            self.proj[r0:r1, base + c0:base + c1] = lax.dot_general(
                self.h_scr[r0:r1, :], w_ref[c0:c1, :], (((1,), (1,)), ((), ())), preferred_element_type=F32)

    def finish(self):
        while self.todo:
            self.step()


def _odd_mixers(first, kv0, proj, x_ref, o_ref, bias_refs, wout_ref, fg_ref,
                kvbuf, abias, c_scr, m_scr, y_scr, nxt, after=None):
    T = x_ref.shape[0]
    NB = T // BLOCK
    H = MLSTM_HEADS
    L = min(MCHUNK, T)
    NEG = -jnp.inf

    lane_q = lax.broadcasted_iota(jnp.int32, (BLOCK, LANES), 1)
    low = lane_q < ATT_HEAD_DIM
    low_t = lax.broadcasted_iota(jnp.int32, (T, LANES), 1) < ATT_HEAD_DIM
    for b, off in enumerate((O_CK, O_CV)):
        kv = proj[:, off:off + ATT_KV_WIDTH]
        sw = pltpu.roll(kv, ATT_HEAD_DIM, 1)
        kvbuf[2 * b, kv0 + BLOCK:kv0 + BLOCK + T, :] = jnp.where(low_t, kv, sw).astype(BF16)
        kvbuf[2 * b + 1, kv0 + BLOCK:kv0 + BLOCK + T, :] = jnp.where(low_t, sw, kv).astype(BF16)
    if first is not False:
        key_lane = lax.broadcasted_iota(jnp.int32, (BLOCK, 2 * BLOCK), 1)
        no_prev = jnp.where((key_lane < BLOCK) & (key_lane != SINK_KEY), jnp.where(first, NEG, 0.0), 0.0)
    sink_row = lax.broadcasted_iota(jnp.int32, (BF16_ROWS, LANES), 0) == SINK_KEY

    def drop_sink_row(a):
        head = a[0:BF16_ROWS, :]
        return jnp.concatenate([jnp.where(sink_row, jnp.zeros_like(head), head), a[BF16_ROWS:, :]], axis=0)

    ones_v = jnp.ones((2 * BLOCK, LANES), BF16)

    def scores(j, g):
        r0 = j * BLOCK
        kk = drop_sink_row(kvbuf[g, kv0 + r0:kv0 + r0 + 2 * BLOCK, :])
        qs = []
        for p in range(g * ATT_GROUP // 2, (g + 1) * ATT_GROUP // 2):
            qp = proj[r0:r0 + BLOCK, O_CQ + p * LANES:O_CQ + (p + 1) * LANES]
            qs.append(jnp.where(low, qp, 0.0).astype(BF16))
            qs.append(jnp.where(low, 0.0, qp).astype(BF16))
        return lax.dot_general(jnp.concatenate(qs, axis=0), kk, (((1,), (1,)), ((), ())),
                               preferred_element_type=F32)

    def attention():
        slots = [(j, g) for j in range(NB) for g in range(ATT_KV_HEADS)]
        s_next = scores(*slots[0])
        for n, (j, g) in enumerate(slots):
            r0 = j * BLOCK
            s_all = s_next
            if n + 1 < len(slots):
                s_next = scores(*slots[n + 1])
            if n % ATTN_SKIP_EVERY != ATTN_SKIP_EVERY - 1:
                nxt.step()
            ps = []
            for r in range(ATT_GROUP):
                hd = g * ATT_GROUP + r
                s = s_all[r * BLOCK:(r + 1) * BLOCK, :] + abias[hd * BLOCK:(hd + 1) * BLOCK, :]
                if j == 0 and first is not False:
                    s = s + no_prev
                ps.append(jnp.exp2(s - jnp.max(s, axis=1, keepdims=True)).astype(BF16))
            vv = jnp.concatenate(
                [drop_sink_row(kvbuf[ATT_KV_HEADS + g, kv0 + r0:kv0 + r0 + 2 * BLOCK, :]), ones_v], axis=1)
            o_all = jnp.dot(jnp.concatenate(ps, axis=0), vv, preferred_element_type=F32)
            for pp in range(ATT_GROUP // 2):
                p = g * ATT_GROUP // 2 + pp
                lo = o_all[2 * pp * BLOCK:(2 * pp + 1) * BLOCK, :]
                hi = o_all[(2 * pp + 1) * BLOCK:(2 * pp + 2) * BLOCK, :]
                att = jnp.where(low, lo[:, :LANES] / lo[:, LANES:], hi[:, :LANES] / hi[:, LANES:])
                gate = proj[r0:r0 + BLOCK, O_GC + p * LANES:O_GC + (p + 1) * LANES]
                y_scr[r0:r0 + BLOCK, p * LANES:(p + 1) * LANES] = (att * jax.nn.silu(gate)).astype(BF16)

    gt = proj[:, O_IF:O_IF + GATE_PAD].T[0:2 * H, :]
    row = lax.broadcasted_iota(jnp.int32, (2 * H, T), 0)
    top = row < H
    ib_ref, fb_ref = bias_refs
    bias = jnp.zeros((2 * H, T), F32)
    for hd in range(H):
        bias = jnp.where(row == hd, ib_ref[0:1, hd:hd + 1],
                         jnp.where(row == H + hd, fb_ref[0:1, hd:hd + 1], bias))
    pre = gt + bias
    bcum = _lane_scan(jax.nn.log_sigmoid(pre), jnp.add, 0.0)
    bsw = pltpu.roll(bcum, H, 0)
    a_top = pre - bsw
    ag = jnp.where(top, a_top, pltpu.roll(a_top, H, 0))
    bf = jnp.where(top, bsw, bcum)
    m_in = jnp.concatenate([m_scr[...]] * (T // LANES), axis=1)
    mg = jnp.maximum(m_in, _lane_scan(ag, jnp.maximum, NEG))
    m_t = bf + mg
    nrm = jnp.exp(-m_t)
    inter_parts, wk_parts, decay_parts = [], [], []
    for c in range(T // L):
        m_start = m_in[:, 0:1] if c == 0 else mg[:, c * L - 1:c * L]
        m_last = mg[:, (c + 1) * L - 1:(c + 1) * L]
        inter_parts.append(jnp.exp(m_start - mg[:, c * L:(c + 1) * L]))
        wk_parts.append(jnp.exp(ag[:, c * L:(c + 1) * L] - m_last))
        decay_parts.append(jnp.broadcast_to(jnp.exp(m_start - m_last), (2 * H, 2 * MLSTM_V_DIM)))
    inter = jnp.concatenate(inter_parts, axis=1)
    m_scr[...] = jnp.broadcast_to(m_t[:, T - 1:T], m_scr.shape)
    ag2 = ag * LOG2E
    cols = jnp.concatenate([mg * LOG2E, inter, nrm, jnp.zeros((LANES - 6 * H, T), F32)], axis=0).T

    kT = proj[:, O_DK:O_DK + MLSTM_QK_WIDTH].T
    lane_p = lax.broadcasted_iota(jnp.int32, (L, LANES), 1)
    low_p = lane_p < MLSTM_QK_DIM
    tri = (lax.broadcasted_iota(jnp.int32, (BLOCK, BLOCK), 1) <= lax.broadcasted_iota(jnp.int32, (BLOCK, BLOCK), 0))
    ones = jnp.ones((L, MLSTM_V_DIM), BF16)
    zero_blk = jnp.zeros((BLOCK, BLOCK), BF16)

    def mlstm_chunk(c):
        r0 = c * L
        for pair in range(H // 2):
            qp = proj[r0:r0 + L, O_DQ + pair * LANES:O_DQ + (pair + 1) * LANES]
            kT_pair = kT[pair * LANES:(pair + 1) * LANES, r0:r0 + L].astype(BF16)
            c_pair = c_scr[pair * LANES:(pair + 1) * LANES, :].astype(BF16)
            for half in range(2):
                hd = 2 * pair + half
                q_h = jnp.where(low_p, qp, 0.0) if half == 0 else jnp.where(low_p, 0.0, qp)
                s = jnp.dot(q_h.astype(BF16), kT_pair, preferred_element_type=F32)
                if H * c + hd < MLSTM_CHUNKS:
                    nxt.step()
                p_rows = []
                for bi in range(L // BLOCK):
                    m_col = cols[r0 + bi * BLOCK:r0 + (bi + 1) * BLOCK, hd:hd + 1]
                    blks = []
                    for bj in range(L // BLOCK):
                        if bj > bi:
                            blks.append(zero_blk)
                            continue
                        arg = ag2[hd:hd + 1, r0 + bj * BLOCK:r0 + (bj + 1) * BLOCK] - m_col
                        if bj == bi:
                            arg = jnp.where(tri, arg, NEG)
                        sb = s[bi * BLOCK:(bi + 1) * BLOCK, bj * BLOCK:(bj + 1) * BLOCK]
                        blks.append((sb * jnp.exp2(arg)).astype(BF16))
                    p_rows.append(jnp.concatenate(blks, axis=1))
                p_bf = jnp.concatenate(p_rows, axis=0)
                v_aug = jnp.concatenate(
                    [proj[r0:r0 + L, O_DV + hd * MLSTM_V_DIM:O_DV + (hd + 1) * MLSTM_V_DIM].astype(BF16), ones],
                    axis=1)
                q_int = (q_h * cols[r0:r0 + L, 2 * H + hd:2 * H + hd + 1]).astype(BF16)
                num = jnp.dot(jnp.concatenate([p_bf, q_int], axis=1),
                              jnp.concatenate([v_aug, c_pair], axis=0), preferred_element_type=F32)
                den = jnp.maximum(jnp.abs(num[:, MLSTM_V_DIM:]), cols[r0:r0 + L, 4 * H + hd:4 * H + hd + 1])
                hm = num[:, :MLSTM_V_DIM] / den
                og = proj[r0:r0 + L, O_DO + hd * MLSTM_V_DIM:O_DO + (hd + 1) * MLSTM_V_DIM]
                gd = proj[r0:r0 + L, O_GD + hd * MLSTM_V_DIM:O_GD + (hd + 1) * MLSTM_V_DIM]
                y_scr[r0:r0 + L, ATT_WIDTH + hd * MLSTM_V_DIM:ATT_WIDTH + (hd + 1) * MLSTM_V_DIM] = (
                    hm * jax.nn.sigmoid(og) * jax.nn.silu(gd)).astype(BF16)
                kw = (kT[hd * MLSTM_QK_DIM:(hd + 1) * MLSTM_QK_DIM, r0:r0 + L]
                      * wk_parts[c][hd:hd + 1, :]).astype(BF16)
                upd = jnp.dot(kw, v_aug, preferred_element_type=F32)
                rows = slice(hd * MLSTM_QK_DIM, (hd + 1) * MLSTM_QK_DIM)
                c_scr[rows, :] = decay_parts[c][hd:hd + 1, :] * c_scr[rows, :] + upd

    attention()
    for _ in range(PRE_CHUNKS):
        nxt.step()
    for c in range(T // L):
        mlstm_chunk(c)
    nxt.finish()
    if after is not None:
        after.normalise()
    for r0 in range(0, T, OUT_ROWS):
        x2 = x_ref[r0:r0 + OUT_ROWS, :] + jnp.dot(y_scr[r0:r0 + OUT_ROWS, :], wout_ref[...],
                                                   preferred_element_type=F32)
        o_ref[r0:r0 + OUT_ROWS, :] = _rmsnorm(x2, fg_ref[...])


def _odd_kernel(sink_ref, ib_ref, fb_ref, xa_ref, xb_ref, xn_ref, wa_ref, wb_ref, wg_ref, wout_ref,
                fg_ref, o_ref, proj_a, proj_b, h_a, h_b, kvbuf, abias, c_scr, m_scr, y_a, y_b):
    s = pl.program_id(0)
    T = xa_ref.shape[0]
    w = (wa_ref, wb_ref, wg_ref)
    rest = ((ib_ref, fb_ref), wout_ref, fg_ref, kvbuf, abias, c_scr, m_scr)

    @pl.when(s == 0)
    def _():
        _odd_init(sink_ref, kvbuf, abias, c_scr, m_scr)
        first = _InProj(xa_ref, *w, h_a, proj_a)
        first.normalise()
        first.finish()

    nxt = _InProj(xb_ref, *w, h_a, proj_b)
    nxt.normalise()
    after = _InProj(xn_ref, *w, h_b, proj_a)
    _odd_mixers(s == 0, 0, proj_a, xa_ref, o_ref.at[0:T], *rest, y_a, nxt, after)
    _odd_mixers(False, T, proj_b, xb_ref, o_ref.at[T:2 * T], *rest, y_b, after)
    for b in range(2 * ATT_KV_HEADS):
        kvbuf[b, 0:BLOCK, :] = kvbuf[b, 2 * T:2 * T + BLOCK, :]


def _odd_layer(x, wa, wb, wg, i_bias, f_bias, sinks, wo, final_g):
    S = x.shape[0]
    T = min(TILE_ODD, S // 2)
    NT = S // T
    assert S % (2 * T) == 0 and T % BLOCK == 0 and T % min(MCHUNK, T) == 0
    assert i_bias.shape == f_bias.shape == (1, MLSTM_HEADS) and sinks.shape == (1, ATT_HEADS)
    const = _const_spec
    return pl.pallas_call(
        _odd_kernel,
        grid=(NT // 2,),
        in_specs=[
            const(1, ATT_HEADS), const(1, MLSTM_HEADS), const(1, MLSTM_HEADS),
            pl.BlockSpec((T, D_MODEL), lambda s: (2 * s, 0)),
            pl.BlockSpec((T, D_MODEL), lambda s: (2 * s + 1, 0)),
            pl.BlockSpec((T, D_MODEL), lambda s: (jnp.minimum(2 * s + 2, NT - 1), 0)),
            const(O_DO, D_MODEL),
            const(O_IF - O_DO, D_MODEL),
            const(GATE_PAD, D_MODEL),
            const(ODD_OUT, D_MODEL),
            const(1, D_MODEL),
        ],
        out_specs=pl.BlockSpec((2 * T, D_MODEL), lambda s: (s, 0)),
        out_shape=jax.ShapeDtypeStruct((S, D_MODEL), F32),
        scratch_shapes=[
            pltpu.VMEM((T, ODD_IN_PAD), F32),
            pltpu.VMEM((T, ODD_IN_PAD), F32),
            pltpu.VMEM((T, D_MODEL), BF16),
            pltpu.VMEM((T, D_MODEL), BF16),
            pltpu.VMEM((2 * ATT_KV_HEADS, 2 * T + BLOCK, LANES), BF16),
            pltpu.VMEM((ATT_HEADS * BLOCK, 2 * BLOCK), F32),
            pltpu.VMEM((MLSTM_HEADS * MLSTM_QK_DIM, 2 * MLSTM_V_DIM), F32),
            pltpu.VMEM((2 * MLSTM_HEADS, LANES), F32),
            pltpu.VMEM((T, ODD_OUT), BF16),
            pltpu.VMEM((T, ODD_OUT), BF16),
        ],
        compiler_params=pltpu.CompilerParams(
            dimension_semantics=("arbitrary",), vmem_limit_bytes=VMEM_LIMIT),
        name="odd_layer",
    )(sinks.astype(F32), i_bias.astype(F32), f_bias.astype(F32), x, x, x, wa, wb, wg, wo,
      final_g.reshape(1, D_MODEL))


def kernel(x, even_norm, even_w_in, even_conv_w, even_pool_w, even_pool_scale, even_w_out,
           odd_norm, odd_w_in, odd_i_bias, odd_f_bias, odd_sinks, odd_w_out, final_norm):
    B, S, D = x.shape
    assert B == 1 and D == D_MODEL
    assert even_norm.shape[0] == 1 and odd_norm.shape[0] == 1
    h = x.reshape(S, D)
    h, wa, wb, wg, wo = _even_layer(h, even_norm[0], even_w_in[0], jnp.swapaxes(even_conv_w, 0, 1), even_pool_w[0],
                                    even_pool_scale[0], even_w_out[0], odd_norm[0], odd_w_in[0].T, odd_w_out[0])
    h = _odd_layer(h, wa, wb, wg, odd_i_bias, odd_f_bias, odd_sinks, wo, final_norm)
    return h.reshape(B, S, D)
```

```python
import jax
import jax.numpy as jnp
from jax import lax
from jax.experimental import pallas as pl
from jax.experimental.pallas import tpu as pltpu

F32 = jnp.float32
BF16 = jnp.bfloat16

D_MODEL = 1024
RMS_EPS = 1e-6

CONV_WIDTH = 512
CONV_K = 3
POOL_WINDOWS = (2, 4, 8, 16)
POOL_WIDTH = 512
POOL_GROUP = POOL_WIDTH // len(POOL_WINDOWS)
EVEN_IN = 4 * CONV_WIDTH + 2 * POOL_WIDTH
EVEN_OUT = CONV_WIDTH + POOL_WIDTH
CONV_HALO = 8
POOL_HALO = 16

ATT_HEADS = 8
ATT_KV_HEADS = 2
ATT_GROUP = ATT_HEADS // ATT_KV_HEADS
ATT_HEAD_DIM = 64
ATT_WIDTH = ATT_HEADS * ATT_HEAD_DIM
ATT_KV_WIDTH = ATT_KV_HEADS * ATT_HEAD_DIM
WINDOW = 128
BLOCK = 128
MLSTM_HEADS = 4
MLSTM_QK_DIM = 64
MLSTM_V_DIM = 128
MLSTM_QK_WIDTH = MLSTM_HEADS * MLSTM_QK_DIM
MLSTM_WIDTH = MLSTM_HEADS * MLSTM_V_DIM
ODD_OUT = ATT_WIDTH + MLSTM_WIDTH
LANES = 128
BF16_ROWS = 16
SINK_KEY = 0
MXU_WIDTH = 256
GATE_PAD = LANES

O_CQ = 0
O_CK = O_CQ + ATT_WIDTH
O_CV = O_CK + ATT_KV_WIDTH
O_GC = O_CV + ATT_KV_WIDTH
O_DQ = O_GC + ATT_WIDTH
O_DK = O_DQ + MLSTM_QK_WIDTH
O_DV = O_DK + MLSTM_QK_WIDTH
O_DO = O_DV + MLSTM_WIDTH
O_GD = O_DO + MLSTM_WIDTH
O_IF = O_GD + MLSTM_WIDTH
ODD_IN_PAD = O_IF + GATE_PAD

TILE_EVEN = 1024
EVEN_CHUNK = 512
TILE_ODD = 512
MCHUNK = 256
INPROJ_ROWS = 512
OUT_ROWS = 256
PRE_CHUNKS = 3
MLSTM_CHUNKS = 5
ATTN_SKIP_EVERY = 4
LOG2E = 1.4426950408889634
ATT_Q_SCALE = ATT_HEAD_DIM ** -0.5 * LOG2E
MLSTM_Q_SCALE = MLSTM_QK_DIM ** -0.5
VMEM_LIMIT = 60 * 1024 * 1024


def _const_spec(*shape):
    return pl.BlockSpec(shape, lambda i: (0,) * len(shape), pipeline_mode=pl.Buffered(1))


def _rms_scale(x):
    ms = jnp.mean(x * x, axis=-1, keepdims=True)
    return x * lax.rsqrt(ms + RMS_EPS)


def _rmsnorm(x, g):
    return _rms_scale(x) * g


def _even_kernel(x_ref, g_ref, win_ref, cw_ref, pw_ref, ps_ref, wout_ref,
                 ng_ref, nwa32_ref, nwb32_ref, nwg32_ref, nwo32_ref,
                 o_ref, nwa_ref, nwb_ref, nwg_ref, nwo_ref, proj_scr, uc_scr, up_scr, y_scr, wfold_scr):
    i = pl.program_id(0)
    T = x_ref.shape[0]
    C = CONV_WIDTH

    ng = ng_ref[...]
    ra = nwa32_ref.shape[0]
    feat = lax.broadcasted_iota(jnp.int32, (ra, D_MODEL), 0) + i * ra
    q_scale = jnp.where(feat < O_CK, ATT_Q_SCALE,
                        jnp.where(feat < O_DQ, 1.0, jnp.where(feat < O_DK, MLSTM_Q_SCALE, 1.0)))
    nwa_ref[...] = (nwa32_ref[...] * ng * q_scale).astype(BF16)
    nwb_ref[...] = (nwb32_ref[...] * ng).astype(BF16)
    nwo_ref[...] = nwo32_ref[...].astype(BF16)
    nwg_ref[...] = jnp.concatenate(
        [nwg32_ref[...] * ng, jnp.zeros((GATE_PAD - 2 * MLSTM_HEADS, D_MODEL), F32)], axis=0).astype(BF16)

    @pl.when(i == 0)
    def _():
        uc_scr[0:CONV_HALO, :] = jnp.zeros((CONV_HALO, C), F32)
        up_scr[0:POOL_HALO, :] = jnp.zeros((POOL_HALO, POOL_WIDTH), F32)
        for g in range(len(POOL_WINDOWS)):
            lo, hi = g * POOL_GROUP, (g + 1) * POOL_GROUP
            wfold_scr[:, lo:hi] = jnp.dot(win_ref[:, 4 * C + lo:4 * C + hi], pw_ref[g],
                                          precision=lax.Precision.HIGHEST, preferred_element_type=F32)

    h = _rmsnorm(x_ref[...], g_ref[...]).astype(BF16)
    for c0 in range(0, EVEN_IN, EVEN_CHUNK):
        w = wfold_scr[...] if c0 == 4 * C else win_ref[:, c0:c0 + EVEN_CHUNK]
        proj_scr[:, c0:c0 + EVEN_CHUNK] = jnp.dot(h, w.astype(BF16), preferred_element_type=F32)

    uc = proj_scr[:, 2 * C:3 * C] * proj_scr[:, 0:C]
    uc_scr[CONV_HALO:CONV_HALO + T, :] = uc
    conv = (cw_ref[2] * uc
            + cw_ref[1] * uc_scr[CONV_HALO - 1:CONV_HALO - 1 + T, :]
            + cw_ref[0] * uc_scr[CONV_HALO - 2:CONV_HALO - 2 + T, :])
    ya = proj_scr[:, C:2 * C] * conv * jax.nn.silu(proj_scr[:, 3 * C:4 * C])
    y_scr[:, 0:C] = ya.astype(BF16)
    uc_scr[0:CONV_HALO, :] = uc_scr[T:T + CONV_HALO, :]

    up_scr[POOL_HALO:POOL_HALO + T, :] = proj_scr[:, 4 * C:4 * C + POOL_WIDTH]
    t1 = (lax.broadcasted_iota(jnp.int32, (T, POOL_GROUP), 0) + (i * T + 1)).astype(F32)
    for g, w in enumerate(POOL_WINDOWS):
        lo, hi = g * POOL_GROUP, (g + 1) * POOL_GROUP
        s = up_scr[:, lo:hi]
        k = 1
        while k < w:
            s = s + pltpu.roll(s, k, 0)
            k *= 2
        u = up_scr[POOL_HALO:POOL_HALO + T, lo:hi]
        yb = s[POOL_HALO:, :] / jnp.minimum(t1, float(w)) - u
        gate = proj_scr[:, 4 * C + POOL_WIDTH + lo:4 * C + POOL_WIDTH + hi]
        y_scr[:, C + lo:C + hi] = (yb * ps_ref[:, lo:hi] * jax.nn.silu(gate)).astype(BF16)
    up_scr[0:POOL_HALO, :] = up_scr[T:T + POOL_HALO, :]

    o_ref[...] = x_ref[...] + jnp.dot(y_scr[...], wout_ref[...].astype(BF16), preferred_element_type=F32)


def _even_layer(x, norm_g, w_in, conv_w, pool_w, pool_scale, w_out, next_norm_g, next_w_in_t, next_w_out):
    S = x.shape[0]
    T = min(TILE_EVEN, S)
    steps = S // T
    n_gates = 2 * MLSTM_HEADS
    assert S % T == 0
    assert next_w_in_t.shape == (O_IF + n_gates, D_MODEL) and next_w_out.shape == (ODD_OUT, D_MODEL)
    ra, rb, ro = O_DO // steps, (O_IF - O_DO) // steps, ODD_OUT // steps
    assert ra * steps == O_DO and rb * steps == O_IF - O_DO and ro * steps == ODD_OUT
    const = _const_spec
    slab = lambda r: pl.BlockSpec((r, D_MODEL), lambda i: (i, 0))
    return pl.pallas_call(
        _even_kernel,
        grid=(steps,),
        in_specs=[
            pl.BlockSpec((T, D_MODEL), lambda i: (i, 0)),
            const(1, D_MODEL),
            const(D_MODEL, EVEN_IN),
            const(CONV_K, 1, CONV_WIDTH),
            const(len(POOL_WINDOWS), POOL_GROUP, POOL_GROUP),
            const(1, POOL_WIDTH),
            const(EVEN_OUT, D_MODEL),
            const(1, D_MODEL),
            slab(ra),
            pl.BlockSpec((pl.Element(rb), pl.Element(D_MODEL)), lambda i: (pl.multiple_of(O_DO + n_gates + i * rb, n_gates), 0)),
            pl.BlockSpec((n_gates, D_MODEL), lambda i: (O_DO // n_gates, 0)),
            slab(ro),
        ],
        out_specs=[pl.BlockSpec((T, D_MODEL), lambda i: (i, 0)),
                   slab(ra), slab(rb), pl.BlockSpec((GATE_PAD, D_MODEL), lambda i: (0, 0)), slab(ro)],
        out_shape=[jax.ShapeDtypeStruct((S, D_MODEL), F32),
                   jax.ShapeDtypeStruct((O_DO, D_MODEL), BF16),
                   jax.ShapeDtypeStruct((O_IF - O_DO, D_MODEL), BF16),
                   jax.ShapeDtypeStruct((GATE_PAD, D_MODEL), BF16),
                   jax.ShapeDtypeStruct((ODD_OUT, D_MODEL), BF16)],
        scratch_shapes=[
            pltpu.VMEM((T, EVEN_IN), F32),
            pltpu.VMEM((T + CONV_HALO, CONV_WIDTH), F32),
            pltpu.VMEM((T + POOL_HALO, POOL_WIDTH), F32),
            pltpu.VMEM((T, EVEN_OUT), BF16),
            pltpu.VMEM((D_MODEL, POOL_WIDTH), F32),
        ],
        compiler_params=pltpu.CompilerParams(
            dimension_semantics=("arbitrary",), vmem_limit_bytes=VMEM_LIMIT),
        name="even_layer",
    )(x, norm_g.reshape(1, D_MODEL), w_in, conv_w, pool_w, pool_scale.reshape(1, POOL_WIDTH), w_out,
      next_norm_g.reshape(1, D_MODEL), next_w_in_t, next_w_in_t, next_w_in_t, next_w_out)


def _lane_scan(x, op, identity):
    n = x.shape[1]
    lane = lax.broadcasted_iota(jnp.int32, x.shape, 1)
    k = 1
    while k < n:
        x = op(x, jnp.where(lane >= k, pltpu.roll(x, k, 1), identity))
        k *= 2
    return x


def _odd_init(sink_ref, ib_ref, fb_ref, kvbuf, abias, bias_scr, c_scr, m_scr):
    for b in range(2 * ATT_KV_HEADS):
        kvbuf[b, 0:BLOCK, :] = jnp.zeros((BLOCK, LANES), BF16)
    c_scr[...] = jnp.zeros(c_scr.shape, F32)
    m_scr[...] = jnp.zeros(m_scr.shape, F32)
    qi = lax.broadcasted_iota(jnp.int32, (BLOCK, 2 * BLOCK), 0)
    kj = lax.broadcasted_iota(jnp.int32, (BLOCK, 2 * BLOCK), 1)
    dist = qi + BLOCK - kj
    ok = (dist >= 0) & (dist < WINDOW)
    distf = dist.astype(F32)
    for hd in range(ATT_HEADS):
        slope = 2.0 ** (-8.0 * (hd + 1) / ATT_HEADS)
        band = jnp.where(ok, (-slope * LOG2E) * distf, -jnp.inf)
        abias[hd * BLOCK:(hd + 1) * BLOCK, :] = jnp.where(kj == SINK_KEY, sink_ref[hd] * LOG2E, band)
    row = lax.broadcasted_iota(jnp.int32, bias_scr.shape, 0)
    bias = jnp.zeros(bias_scr.shape, F32)
    for hd in range(MLSTM_HEADS):
        bias = jnp.where(row == hd, ib_ref[hd], jnp.where(row == MLSTM_HEADS + hd, fb_ref[hd], bias))
    bias_scr[...] = bias


class _InProj:
    def __init__(self, x_ref, wa_ref, wb_ref, wg_ref, h_scr, proj):
        self.x_ref, self.h_scr, self.proj = x_ref, h_scr, proj
        self.todo = []
        rows = x_ref.shape[0]
        for w_ref, base in ((wa_ref, 0), (wb_ref, O_DO), (wg_ref, O_IF)):
            n = w_ref.shape[0]
            for c0 in range(0, n, MXU_WIDTH):
                for r0 in range(0, rows, INPROJ_ROWS):
                    self.todo.append((w_ref, base, c0, min(c0 + MXU_WIDTH, n), r0, min(r0 + INPROJ_ROWS, rows)))

    def normalise(self):
        self.h_scr[...] = _rms_scale(self.x_ref[...]).astype(BF16)

    def step(self):
        if self.todo:
            w_ref, base, c0, c1, r0, r1 = self.todo.pop(0)
            self.proj[r0:r1, base + c0:base + c1] = lax.dot_general(
                self.h_scr[r0:r1, :], w_ref[c0:c1, :], (((1,), (1,)), ((), ())), preferred_element_type=F32)

    def finish(self):
        while self.todo:
            self.step()


def _odd_mixers(first, kv0, proj, x_ref, o_ref, bias_ref, wout_ref, fg_ref,
                kvbuf, abias, c_scr, m_scr, y_scr, nxt, after=None):
    T = x_ref.shape[0]
    NB = T // BLOCK
    H = MLSTM_HEADS
    L = min(MCHUNK, T)
    NEG = -jnp.inf

    lane_q = lax.broadcasted_iota(jnp.int32, (BLOCK, LANES), 1)
    low = lane_q < ATT_HEAD_DIM
    low_t = lax.broadcasted_iota(jnp.int32, (T, LANES), 1) < ATT_HEAD_DIM
    for b, off in enumerate((O_CK, O_CV)):
        kv = proj[:, off:off + ATT_KV_WIDTH]
        sw = pltpu.roll(kv, ATT_HEAD_DIM, 1)
        kvbuf[2 * b, kv0 + BLOCK:kv0 + BLOCK + T, :] = jnp.where(low_t, kv, sw).astype(BF16)
        kvbuf[2 * b + 1, kv0 + BLOCK:kv0 + BLOCK + T, :] = jnp.where(low_t, sw, kv).astype(BF16)
    if first is not False:
        key_lane = lax.broadcasted_iota(jnp.int32, (BLOCK, 2 * BLOCK), 1)
        no_prev = jnp.where((key_lane < BLOCK) & (key_lane != SINK_KEY), jnp.where(first, NEG, 0.0), 0.0)
    sink_row = lax.broadcasted_iota(jnp.int32, (BF16_ROWS, LANES), 0) == SINK_KEY

    def drop_sink_row(a):
        head = a[0:BF16_ROWS, :]
        return jnp.concatenate([jnp.where(sink_row, jnp.zeros_like(head), head), a[BF16_ROWS:, :]], axis=0)

    ones_v = jnp.ones((2 * BLOCK, LANES), BF16)

    def scores(j, g):
        r0 = j * BLOCK
        kk = drop_sink_row(kvbuf[g, kv0 + r0:kv0 + r0 + 2 * BLOCK, :])
        qs = []
        for p in range(g * ATT_GROUP // 2, (g + 1) * ATT_GROUP // 2):
            qp = proj[r0:r0 + BLOCK, O_CQ + p * LANES:O_CQ + (p + 1) * LANES]
            qs.append(jnp.where(low, qp, 0.0).astype(BF16))
            qs.append(jnp.where(low, 0.0, qp).astype(BF16))
        return lax.dot_general(jnp.concatenate(qs, axis=0), kk, (((1,), (1,)), ((), ())),
                               preferred_element_type=F32)

    def attention():
        slots = [(j, g) for j in range(NB) for g in range(ATT_KV_HEADS)]
        s_next = scores(*slots[0])
        for n, (j, g) in enumerate(slots):
            r0 = j * BLOCK
            s_all = s_next
            if n + 1 < len(slots):
                s_next = scores(*slots[n + 1])
            if n % ATTN_SKIP_EVERY != ATTN_SKIP_EVERY - 1:
                nxt.step()
            ps = []
            for r in range(ATT_GROUP):
                hd = g * ATT_GROUP + r
                s = s_all[r * BLOCK:(r + 1) * BLOCK, :] + abias[hd * BLOCK:(hd + 1) * BLOCK, :]
                if j == 0 and first is not False:
                    s = s + no_prev
                ps.append(jnp.exp2(s - jnp.max(s, axis=1, keepdims=True)).astype(BF16))
            vv = jnp.concatenate(
                [drop_sink_row(kvbuf[ATT_KV_HEADS + g, kv0 + r0:kv0 + r0 + 2 * BLOCK, :]), ones_v], axis=1)
            o_all = jnp.dot(jnp.concatenate(ps, axis=0), vv, preferred_element_type=F32)
            for pp in range(ATT_GROUP // 2):
                p = g * ATT_GROUP // 2 + pp
                lo = o_all[2 * pp * BLOCK:(2 * pp + 1) * BLOCK, :]
                hi = o_all[(2 * pp + 1) * BLOCK:(2 * pp + 2) * BLOCK, :]
                att = jnp.where(low, lo[:, :LANES] / lo[:, LANES:], hi[:, :LANES] / hi[:, LANES:])
                gate = proj[r0:r0 + BLOCK, O_GC + p * LANES:O_GC + (p + 1) * LANES]
                y_scr[r0:r0 + BLOCK, p * LANES:(p + 1) * LANES] = (att * jax.nn.silu(gate)).astype(BF16)

    gt = proj[:, O_IF:O_IF + GATE_PAD].T[0:2 * H, :]
    pre = gt + bias_ref[...]
    row = lax.broadcasted_iota(jnp.int32, (2 * H, T), 0)
    top = row < H
    bcum = _lane_scan(jax.nn.log_sigmoid(pre), jnp.add, 0.0)
    bsw = pltpu.roll(bcum, H, 0)
    a_top = pre - bsw
    ag = jnp.where(top, a_top, pltpu.roll(a_top, H, 0))
    bf = jnp.where(top, bsw, bcum)
    m_in = jnp.concatenate([m_scr[...]] * (T // LANES), axis=1)
    mg = jnp.maximum(m_in, _lane_scan(ag, jnp.maximum, NEG))
    m_t = bf + mg
    nrm = jnp.exp(-m_t)
    inter_parts, wk_parts, decay_parts = [], [], []
    for c in range(T // L):
        m_start = m_in[:, 0:1] if c == 0 else mg[:, c * L - 1:c * L]
        m_last = mg[:, (c + 1) * L - 1:(c + 1) * L]
        inter_parts.append(jnp.exp(m_start - mg[:, c * L:(c + 1) * L]))
        wk_parts.append(jnp.exp(ag[:, c * L:(c + 1) * L] - m_last))
        decay_parts.append(jnp.broadcast_to(jnp.exp(m_start - m_last), (2 * H, 2 * MLSTM_V_DIM)))
    inter = jnp.concatenate(inter_parts, axis=1)
    m_scr[...] = jnp.broadcast_to(m_t[:, T - 1:T], m_scr.shape)
    ag2 = ag * LOG2E
    cols = jnp.concatenate([mg * LOG2E, inter, nrm, jnp.zeros((LANES - 6 * H, T), F32)], axis=0).T

    kT = proj[:, O_DK:O_DK + MLSTM_QK_WIDTH].T
    lane_p = lax.broadcasted_iota(jnp.int32, (L, LANES), 1)
    low_p = lane_p < MLSTM_QK_DIM
    tri = (lax.broadcasted_iota(jnp.int32, (BLOCK, BLOCK), 1) <= lax.broadcasted_iota(jnp.int32, (BLOCK, BLOCK), 0))
    ones = jnp.ones((L, MLSTM_V_DIM), BF16)
    zero_blk = jnp.zeros((BLOCK, BLOCK), BF16)

    def mlstm_chunk(c):
        r0 = c * L
        for pair in range(H // 2):
            qp = proj[r0:r0 + L, O_DQ + pair * LANES:O_DQ + (pair + 1) * LANES]
            kT_pair = kT[pair * LANES:(pair + 1) * LANES, r0:r0 + L].astype(BF16)
            c_pair = c_scr[pair * LANES:(pair + 1) * LANES, :].astype(BF16)
            for half in range(2):
                hd = 2 * pair + half
                q_h = jnp.where(low_p, qp, 0.0) if half == 0 else jnp.where(low_p, 0.0, qp)
                s = jnp.dot(q_h.astype(BF16), kT_pair, preferred_element_type=F32)
                if H * c + hd < MLSTM_CHUNKS:
                    nxt.step()
                p_rows = []
                for bi in range(L // BLOCK):
                    m_col = cols[r0 + bi * BLOCK:r0 + (bi + 1) * BLOCK, hd:hd + 1]
                    blks = []
                    for bj in range(L // BLOCK):
                        if bj > bi:
                            blks.append(zero_blk)
                            continue
                        arg = ag2[hd:hd + 1, r0 + bj * BLOCK:r0 + (bj + 1) * BLOCK] - m_col
                        if bj == bi:
                            arg = jnp.where(tri, arg, NEG)
                        sb = s[bi * BLOCK:(bi + 1) * BLOCK, bj * BLOCK:(bj + 1) * BLOCK]
                        blks.append((sb * jnp.exp2(arg)).astype(BF16))
                    p_rows.append(jnp.concatenate(blks, axis=1))
                p_bf = jnp.concatenate(p_rows, axis=0)
                v_aug = jnp.concatenate(
                    [proj[r0:r0 + L, O_DV + hd * MLSTM_V_DIM:O_DV + (hd + 1) * MLSTM_V_DIM].astype(BF16), ones],
                    axis=1)
                q_int = (q_h * cols[r0:r0 + L, 2 * H + hd:2 * H + hd + 1]).astype(BF16)
                num = jnp.dot(jnp.concatenate([p_bf, q_int], axis=1),
                              jnp.concatenate([v_aug, c_pair], axis=0), preferred_element_type=F32)
                den = jnp.maximum(jnp.abs(num[:, MLSTM_V_DIM:]), cols[r0:r0 + L, 4 * H + hd:4 * H + hd + 1])
                hm = num[:, :MLSTM_V_DIM] / den
                og = proj[r0:r0 + L, O_DO + hd * MLSTM_V_DIM:O_DO + (hd + 1) * MLSTM_V_DIM]
                gd = proj[r0:r0 + L, O_GD + hd * MLSTM_V_DIM:O_GD + (hd + 1) * MLSTM_V_DIM]
                y_scr[r0:r0 + L, ATT_WIDTH + hd * MLSTM_V_DIM:ATT_WIDTH + (hd + 1) * MLSTM_V_DIM] = (
                    hm * jax.nn.sigmoid(og) * jax.nn.silu(gd)).astype(BF16)
                kw = (kT[hd * MLSTM_QK_DIM:(hd + 1) * MLSTM_QK_DIM, r0:r0 + L]
                      * wk_parts[c][hd:hd + 1, :]).astype(BF16)
                upd = jnp.dot(kw, v_aug, preferred_element_type=F32)
                rows = slice(hd * MLSTM_QK_DIM, (hd + 1) * MLSTM_QK_DIM)
                c_scr[rows, :] = decay_parts[c][hd:hd + 1, :] * c_scr[rows, :] + upd

    attention()
    for _ in range(PRE_CHUNKS):
        nxt.step()
    for c in range(T // L):
        mlstm_chunk(c)
    nxt.finish()
    if after is not None:
        after.normalise()
    for r0 in range(0, T, OUT_ROWS):
        x2 = x_ref[r0:r0 + OUT_ROWS, :] + jnp.dot(y_scr[r0:r0 + OUT_ROWS, :], wout_ref[...],
                                                   preferred_element_type=F32)
        o_ref[r0:r0 + OUT_ROWS, :] = _rmsnorm(x2, fg_ref[...])


def _odd_kernel(sink_ref, ib_ref, fb_ref, xa_ref, xb_ref, xn_ref, wa_ref, wb_ref, wg_ref, wout_ref,
                fg_ref, o_ref, proj_a, proj_b, h_a, h_b, kvbuf, abias, bias_scr, c_scr, m_scr, y_a, y_b):
    s = pl.program_id(0)
    T = xa_ref.shape[0]
    w = (wa_ref, wb_ref, wg_ref)
    rest = (bias_scr, wout_ref, fg_ref, kvbuf, abias, c_scr, m_scr)

    @pl.when(s == 0)
    def _():
        _odd_init(sink_ref, ib_ref, fb_ref, kvbuf, abias, bias_scr, c_scr, m_scr)
        first = _InProj(xa_ref, *w, h_a, proj_a)
        first.normalise()
        first.finish()

    nxt = _InProj(xb_ref, *w, h_a, proj_b)
    nxt.normalise()
    after = _InProj(xn_ref, *w, h_b, proj_a)
    _odd_mixers(s == 0, 0, proj_a, xa_ref, o_ref.at[0:T], *rest, y_a, nxt, after)
    _odd_mixers(False, T, proj_b, xb_ref, o_ref.at[T:2 * T], *rest, y_b, after)
    for b in range(2 * ATT_KV_HEADS):
        kvbuf[b, 0:BLOCK, :] = kvbuf[b, 2 * T:2 * T + BLOCK, :]


def _odd_layer(x, wa, wb, wg, i_bias, f_bias, sinks, wo, final_g):
    S = x.shape[0]
    T = min(TILE_ODD, S // 2)
    NT = S // T
    assert S % (2 * T) == 0 and T % BLOCK == 0 and T % min(MCHUNK, T) == 0
    assert i_bias.shape == f_bias.shape == (MLSTM_HEADS,) and sinks.shape == (ATT_HEADS,)
    const = _const_spec
    smem = pl.BlockSpec(memory_space=pltpu.SMEM)
    return pl.pallas_call(
        _odd_kernel,
        grid=(NT // 2,),
        in_specs=[
            smem, smem, smem,
            pl.BlockSpec((T, D_MODEL), lambda s: (2 * s, 0)),
            pl.BlockSpec((T, D_MODEL), lambda s: (2 * s + 1, 0)),
            pl.BlockSpec((T, D_MODEL), lambda s: (jnp.minimum(2 * s + 2, NT - 1), 0)),
            const(O_DO, D_MODEL),
            const(O_IF - O_DO, D_MODEL),
            const(GATE_PAD, D_MODEL),
            const(ODD_OUT, D_MODEL),
            const(1, D_MODEL),
        ],
        out_specs=pl.BlockSpec((2 * T, D_MODEL), lambda s: (s, 0)),
        out_shape=jax.ShapeDtypeStruct((S, D_MODEL), F32),
        scratch_shapes=[
            pltpu.VMEM((T, ODD_IN_PAD), F32),
            pltpu.VMEM((T, ODD_IN_PAD), F32),
            pltpu.VMEM((T, D_MODEL), BF16),
            pltpu.VMEM((T, D_MODEL), BF16),
            pltpu.VMEM((2 * ATT_KV_HEADS, 2 * T + BLOCK, LANES), BF16),
            pltpu.VMEM((ATT_HEADS * BLOCK, 2 * BLOCK), F32),
            pltpu.VMEM((2 * MLSTM_HEADS, T), F32),
            pltpu.VMEM((MLSTM_HEADS * MLSTM_QK_DIM, 2 * MLSTM_V_DIM), F32),
            pltpu.VMEM((2 * MLSTM_HEADS, LANES), F32),
            pltpu.VMEM((T, ODD_OUT), BF16),
            pltpu.VMEM((T, ODD_OUT), BF16),
        ],
        compiler_params=pltpu.CompilerParams(
            dimension_semantics=("arbitrary",), vmem_limit_bytes=VMEM_LIMIT),
        name="odd_layer",
    )(sinks.astype(F32), i_bias.astype(F32), f_bias.astype(F32), x, x, x, wa, wb, wg, wo,
      final_g.reshape(1, D_MODEL))


def kernel(x, even_norm, even_w_in, even_conv_w, even_pool_w, even_pool_scale, even_w_out,
           odd_norm, odd_w_in, odd_i_bias, odd_f_bias, odd_sinks, odd_w_out, final_norm):
    B, S, D = x.shape
    assert B == 1 and D == D_MODEL
    assert even_norm.shape[0] == 1 and odd_norm.shape[0] == 1
    h = x.reshape(S, D)
    h, wa, wb, wg, wo = _even_layer(h, even_norm[0], even_w_in[0], jnp.swapaxes(even_conv_w, 0, 1), even_pool_w[0],
                                    even_pool_scale[0], even_w_out[0], odd_norm[0], odd_w_in[0].T, odd_w_out[0])
    h = _odd_layer(h, wa, wb, wg, odd_i_bias[0], odd_f_bias[0], odd_sinks[0], wo, final_norm)
    return h.reshape(B, S, D)
```

```python
import jax
import jax.numpy as jnp
from jax import lax
from jax.experimental import pallas as pl
from jax.experimental.pallas import tpu as pltpu

F32 = jnp.float32
BF16 = jnp.bfloat16

D_MODEL = 1024
RMS_EPS = 1e-6

CONV_WIDTH = 512
CONV_K = 3
POOL_WINDOWS = (2, 4, 8, 16)
POOL_WIDTH = 512
POOL_GROUP = POOL_WIDTH // len(POOL_WINDOWS)
EVEN_IN = 4 * CONV_WIDTH + 2 * POOL_WIDTH
EVEN_OUT = CONV_WIDTH + POOL_WIDTH
CONV_HALO = 8
POOL_HALO = 16

ATT_HEADS = 8
ATT_KV_HEADS = 2
ATT_GROUP = ATT_HEADS // ATT_KV_HEADS
ATT_HEAD_DIM = 64
ATT_WIDTH = ATT_HEADS * ATT_HEAD_DIM
ATT_KV_WIDTH = ATT_KV_HEADS * ATT_HEAD_DIM
WINDOW = 128
BLOCK = 128
MLSTM_HEADS = 4
MLSTM_QK_DIM = 64
MLSTM_V_DIM = 128
MLSTM_QK_WIDTH = MLSTM_HEADS * MLSTM_QK_DIM
MLSTM_WIDTH = MLSTM_HEADS * MLSTM_V_DIM
ODD_OUT = ATT_WIDTH + MLSTM_WIDTH
LANES = 128
BF16_ROWS = 16
SINK_KEY = 0
MXU_WIDTH = 256
GATE_PAD = LANES

O_CQ = 0
O_CK = O_CQ + ATT_WIDTH
O_CV = O_CK + ATT_KV_WIDTH
O_GC = O_CV + ATT_KV_WIDTH
O_DQ = O_GC + ATT_WIDTH
O_DK = O_DQ + MLSTM_QK_WIDTH
O_DV = O_DK + MLSTM_QK_WIDTH
O_DO = O_DV + MLSTM_WIDTH
O_GD = O_DO + MLSTM_WIDTH
O_IF = O_GD + MLSTM_WIDTH
ODD_IN_PAD = O_IF + GATE_PAD

TILE_EVEN = 1024
EVEN_CHUNK = 512
TILE_ODD = 512
MCHUNK = 256
INPROJ_ROWS = 512
OUT_ROWS = 256
PRE_CHUNKS = 3
MLSTM_CHUNKS = 5
ATTN_SKIP_EVERY = 4
LOG2E = 1.4426950408889634
ATT_Q_SCALE = ATT_HEAD_DIM ** -0.5 * LOG2E
MLSTM_Q_SCALE = MLSTM_QK_DIM ** -0.5
VMEM_LIMIT = 60 * 1024 * 1024


def _const_spec(*shape):
    return pl.BlockSpec(shape, lambda i: (0,) * len(shape), pipeline_mode=pl.Buffered(1))


def _rms_scale(x):
    ms = jnp.mean(x * x, axis=-1, keepdims=True)
    return x * lax.rsqrt(ms + RMS_EPS)


def _rmsnorm(x, g):
    return _rms_scale(x) * g


def _split_bf16(a):
    hi = a.astype(BF16)
    return hi, (a - hi.astype(F32)).astype(BF16)


def _even_kernel(x_ref, g_ref, win_ref, cw_ref, pw_ref, ps_ref, wout_ref,
                 ng_ref, nwa32_ref, nwb32_ref, nwg32_ref, nwo32_ref,
                 o_ref, nwa_ref, nwb_ref, nwg_ref, nwo_ref, proj_scr, uc_scr, up_scr, y_scr, wfold_scr):
    i = pl.program_id(0)
    T = x_ref.shape[0]
    C = CONV_WIDTH

    ng = ng_ref[...]
    ra = nwa32_ref.shape[0]
    feat = lax.broadcasted_iota(jnp.int32, (ra, D_MODEL), 0) + i * ra
    q_scale = jnp.where(feat < O_CK, ATT_Q_SCALE,
                        jnp.where(feat < O_DQ, 1.0, jnp.where(feat < O_DK, MLSTM_Q_SCALE, 1.0)))
    nwa_ref[...] = (nwa32_ref[...] * ng * q_scale).astype(BF16)
    nwb_ref[...] = (nwb32_ref[...] * ng).astype(BF16)
    nwo_ref[...] = nwo32_ref[...].astype(BF16)
    nwg_ref[...] = jnp.concatenate(
        [nwg32_ref[...] * ng, jnp.zeros((GATE_PAD - 2 * MLSTM_HEADS, D_MODEL), F32)], axis=0).astype(BF16)

    @pl.when(i == 0)
    def _():
        uc_scr[0:CONV_HALO, :] = jnp.zeros((CONV_HALO, C), F32)
        up_scr[0:POOL_HALO, :] = jnp.zeros((POOL_HALO, POOL_WIDTH), F32)
        for g in range(len(POOL_WINDOWS)):
            lo, hi = g * POOL_GROUP, (g + 1) * POOL_GROUP
            a_hi, a_lo = _split_bf16(win_ref[:, 4 * C + lo:4 * C + hi])
            b_hi, b_lo = _split_bf16(pw_ref[g])
            wfold_scr[:, lo:hi] = jnp.dot(jnp.concatenate([a_hi, a_lo, a_hi], axis=1),
                                          jnp.concatenate([b_hi, b_hi, b_lo], axis=0),
                                          preferred_element_type=F32)

    h = _rmsnorm(x_ref[...], g_ref[...]).astype(BF16)
    for c0 in range(0, EVEN_IN, EVEN_CHUNK):
        w = wfold_scr[...] if c0 == 4 * C else win_ref[:, c0:c0 + EVEN_CHUNK]
        proj_scr[:, c0:c0 + EVEN_CHUNK] = jnp.dot(h, w.astype(BF16), preferred_element_type=F32)

    uc = proj_scr[:, 2 * C:3 * C] * proj_scr[:, 0:C]
    uc_scr[CONV_HALO:CONV_HALO + T, :] = uc
    conv = (cw_ref[2] * uc
            + cw_ref[1] * uc_scr[CONV_HALO - 1:CONV_HALO - 1 + T, :]
            + cw_ref[0] * uc_scr[CONV_HALO - 2:CONV_HALO - 2 + T, :])
    ya = proj_scr[:, C:2 * C] * conv * jax.nn.silu(proj_scr[:, 3 * C:4 * C])
    y_scr[:, 0:C] = ya.astype(BF16)
    uc_scr[0:CONV_HALO, :] = uc_scr[T:T + CONV_HALO, :]

    up_scr[POOL_HALO:POOL_HALO + T, :] = proj_scr[:, 4 * C:4 * C + POOL_WIDTH]
    t1 = (lax.broadcasted_iota(jnp.int32, (T, POOL_GROUP), 0) + (i * T + 1)).astype(F32)
    for g, w in enumerate(POOL_WINDOWS):
        lo, hi = g * POOL_GROUP, (g + 1) * POOL_GROUP
        s = up_scr[:, lo:hi]
        k = 1
        while k < w:
            s = s + pltpu.roll(s, k, 0)
            k *= 2
        u = up_scr[POOL_HALO:POOL_HALO + T, lo:hi]
        yb = s[POOL_HALO:, :] / jnp.minimum(t1, float(w)) - u
        gate = proj_scr[:, 4 * C + POOL_WIDTH + lo:4 * C + POOL_WIDTH + hi]
        y_scr[:, C + lo:C + hi] = (yb * ps_ref[:, lo:hi] * jax.nn.silu(gate)).astype(BF16)
    up_scr[0:POOL_HALO, :] = up_scr[T:T + POOL_HALO, :]

    o_ref[...] = x_ref[...] + jnp.dot(y_scr[...], wout_ref[...].astype(BF16), preferred_element_type=F32)


def _even_layer(x, norm_g, w_in, conv_w, pool_w, pool_scale, w_out, next_norm_g, next_w_in_t, next_w_out):
    S = x.shape[0]
    T = min(TILE_EVEN, S)
    steps = S // T
    n_gates = 2 * MLSTM_HEADS
    assert S % T == 0
    assert next_w_in_t.shape == (O_IF + n_gates, D_MODEL) and next_w_out.shape == (ODD_OUT, D_MODEL)
    ra, rb, ro = O_DO // steps, (O_IF - O_DO) // steps, ODD_OUT // steps
    assert ra * steps == O_DO and rb * steps == O_IF - O_DO and ro * steps == ODD_OUT
    const = _const_spec
    slab = lambda r: pl.BlockSpec((r, D_MODEL), lambda i: (i, 0))
    return pl.pallas_call(
        _even_kernel,
        grid=(steps,),
        in_specs=[
            pl.BlockSpec((T, D_MODEL), lambda i: (i, 0)),
            const(1, D_MODEL),
            const(D_MODEL, EVEN_IN),
            const(CONV_K, 1, CONV_WIDTH),
            const(len(POOL_WINDOWS), POOL_GROUP, POOL_GROUP),
            const(1, POOL_WIDTH),
            const(EVEN_OUT, D_MODEL),
            const(1, D_MODEL),
            slab(ra),
            pl.BlockSpec((pl.Element(rb), pl.Element(D_MODEL)), lambda i: (pl.multiple_of(O_DO + n_gates + i * rb, n_gates), 0)),
            pl.BlockSpec((n_gates, D_MODEL), lambda i: (O_DO // n_gates, 0)),
            slab(ro),
        ],
        out_specs=[pl.BlockSpec((T, D_MODEL), lambda i: (i, 0)),
                   slab(ra), slab(rb), pl.BlockSpec((GATE_PAD, D_MODEL), lambda i: (0, 0)), slab(ro)],
        out_shape=[jax.ShapeDtypeStruct((S, D_MODEL), F32),
                   jax.ShapeDtypeStruct((O_DO, D_MODEL), BF16),
                   jax.ShapeDtypeStruct((O_IF - O_DO, D_MODEL), BF16),
                   jax.ShapeDtypeStruct((GATE_PAD, D_MODEL), BF16),
                   jax.ShapeDtypeStruct((ODD_OUT, D_MODEL), BF16)],
        scratch_shapes=[
            pltpu.VMEM((T, EVEN_IN), F32),
            pltpu.VMEM((T + CONV_HALO, CONV_WIDTH), F32),
            pltpu.VMEM((T + POOL_HALO, POOL_WIDTH), F32),
            pltpu.VMEM((T, EVEN_OUT), BF16),
            pltpu.VMEM((D_MODEL, POOL_WIDTH), F32),
        ],
        compiler_params=pltpu.CompilerParams(
            dimension_semantics=("arbitrary",), vmem_limit_bytes=VMEM_LIMIT),
        name="even_layer",
    )(x, norm_g.reshape(1, D_MODEL), w_in, conv_w, pool_w, pool_scale.reshape(1, POOL_WIDTH), w_out,
      next_norm_g.reshape(1, D_MODEL), next_w_in_t, next_w_in_t, next_w_in_t, next_w_out)


def _lane_scan(x, op, identity):
    n = x.shape[1]
    lane = lax.broadcasted_iota(jnp.int32, x.shape, 1)
    k = 1
    while k < n:
        x = op(x, jnp.where(lane >= k, pltpu.roll(x, k, 1), identity))
        k *= 2
    return x


def _odd_init(sink_ref, ib_ref, fb_ref, kvbuf, abias, bias_scr, c_scr, m_scr):
    for b in range(2 * ATT_KV_HEADS):
        kvbuf[b, 0:BLOCK, :] = jnp.zeros((BLOCK, LANES), BF16)
    c_scr[...] = jnp.zeros(c_scr.shape, F32)
    m_scr[...] = jnp.zeros(m_scr.shape, F32)
    qi = lax.broadcasted_iota(jnp.int32, (BLOCK, 2 * BLOCK), 0)
    kj = lax.broadcasted_iota(jnp.int32, (BLOCK, 2 * BLOCK), 1)
    dist = qi + BLOCK - kj
    ok = (dist >= 0) & (dist < WINDOW)
    distf = dist.astype(F32)
    for hd in range(ATT_HEADS):
        slope = 2.0 ** (-8.0 * (hd + 1) / ATT_HEADS)
        band = jnp.where(ok, (-slope * LOG2E) * distf, -jnp.inf)
        abias[hd * BLOCK:(hd + 1) * BLOCK, :] = jnp.where(kj == SINK_KEY, sink_ref[hd] * LOG2E, band)
    row = lax.broadcasted_iota(jnp.int32, bias_scr.shape, 0)
    bias = jnp.zeros(bias_scr.shape, F32)
    for hd in range(MLSTM_HEADS):
        bias = jnp.where(row == hd, ib_ref[hd], jnp.where(row == MLSTM_HEADS + hd, fb_ref[hd], bias))
    bias_scr[...] = bias


class _InProj:
    def __init__(self, x_ref, wa_ref, wb_ref, wg_ref, h_scr, proj):
        self.x_ref, self.h_scr, self.proj = x_ref, h_scr, proj
        self.todo = []
        rows = x_ref.shape[0]
        for w_ref, base in ((wa_ref, 0), (wb_ref, O_DO), (wg_ref, O_IF)):
            n = w_ref.shape[0]
            for c0 in range(0, n, MXU_WIDTH):
                for r0 in range(0, rows, INPROJ_ROWS):
                    self.todo.append((w_ref, base, c0, min(c0 + MXU_WIDTH, n), r0, min(r0 + INPROJ_ROWS, rows)))

    def normalise(self):
        self.h_scr[...] = _rms_scale(self.x_ref[...]).astype(BF16)

    def step(self):
        if self.todo:
            w_ref, base, c0, c1, r0, r1 = self.todo.pop(0)
            self.proj[r0:r1, base + c0:base + c1] = lax.dot_general(
                self.h_scr[r0:r1, :], w_ref[c0:c1, :], (((1,), (1,)), ((), ())), preferred_element_type=F32)

    def finish(self):
        while self.todo:
            self.step()


def _odd_mixers(first, kv0, proj, x_ref, o_ref, bias_ref, wout_ref, fg_ref,
                kvbuf, abias, c_scr, m_scr, y_scr, nxt, after=None):
    T = x_ref.shape[0]
    NB = T // BLOCK
    H = MLSTM_HEADS
    L = min(MCHUNK, T)
    NEG = -jnp.inf

    lane_q = lax.broadcasted_iota(jnp.int32, (BLOCK, LANES), 1)
    low = lane_q < ATT_HEAD_DIM
    low_t = lax.broadcasted_iota(jnp.int32, (T, LANES), 1) < ATT_HEAD_DIM
    for b, off in enumerate((O_CK, O_CV)):
        kv = proj[:, off:off + ATT_KV_WIDTH]
        sw = pltpu.roll(kv, ATT_HEAD_DIM, 1)
        kvbuf[2 * b, kv0 + BLOCK:kv0 + BLOCK + T, :] = jnp.where(low_t, kv, sw).astype(BF16)
        kvbuf[2 * b + 1, kv0 + BLOCK:kv0 + BLOCK + T, :] = jnp.where(low_t, sw, kv).astype(BF16)
    if first is not False:
        key_lane = lax.broadcasted_iota(jnp.int32, (BLOCK, 2 * BLOCK), 1)
        no_prev = jnp.where((key_lane < BLOCK) & (key_lane != SINK_KEY), jnp.where(first, NEG, 0.0), 0.0)
    sink_row = lax.broadcasted_iota(jnp.int32, (BF16_ROWS, LANES), 0) == SINK_KEY

    def drop_sink_row(a):
        head = a[0:BF16_ROWS, :]
        return jnp.concatenate([jnp.where(sink_row, jnp.zeros_like(head), head), a[BF16_ROWS:, :]], axis=0)

    ones_v = jnp.ones((2 * BLOCK, LANES), BF16)

    def scores(j, g):
        r0 = j * BLOCK
        kk = drop_sink_row(kvbuf[g, kv0 + r0:kv0 + r0 + 2 * BLOCK, :])
        qs = []
        for p in range(g * ATT_GROUP // 2, (g + 1) * ATT_GROUP // 2):
            qp = proj[r0:r0 + BLOCK, O_CQ + p * LANES:O_CQ + (p + 1) * LANES]
            qs.append(jnp.where(low, qp, 0.0).astype(BF16))
            qs.append(jnp.where(low, 0.0, qp).astype(BF16))
        return lax.dot_general(jnp.concatenate(qs, axis=0), kk, (((1,), (1,)), ((), ())),
                               preferred_element_type=F32)

    def attention():
        slots = [(j, g) for j in range(NB) for g in range(ATT_KV_HEADS)]
        s_next = scores(*slots[0])
        for n, (j, g) in enumerate(slots):
            r0 = j * BLOCK
            s_all = s_next
            if n + 1 < len(slots):
                s_next = scores(*slots[n + 1])
            if n % ATTN_SKIP_EVERY != ATTN_SKIP_EVERY - 1:
                nxt.step()
            ps = []
            for r in range(ATT_GROUP):
                hd = g * ATT_GROUP + r
                s = s_all[r * BLOCK:(r + 1) * BLOCK, :] + abias[hd * BLOCK:(hd + 1) * BLOCK, :]
                if j == 0 and first is not False:
                    s = s + no_prev
                ps.append(jnp.exp2(s - jnp.max(s, axis=1, keepdims=True)).astype(BF16))
            vv = jnp.concatenate(
                [drop_sink_row(kvbuf[ATT_KV_HEADS + g, kv0 + r0:kv0 + r0 + 2 * BLOCK, :]), ones_v], axis=1)
            o_all = jnp.dot(jnp.concatenate(ps, axis=0), vv, preferred_element_type=F32)
            for pp in range(ATT_GROUP // 2):
                p = g * ATT_GROUP // 2 + pp
                lo = o_all[2 * pp * BLOCK:(2 * pp + 1) * BLOCK, :]
                hi = o_all[(2 * pp + 1) * BLOCK:(2 * pp + 2) * BLOCK, :]
                att = jnp.where(low, lo[:, :LANES] / lo[:, LANES:], hi[:, :LANES] / hi[:, LANES:])
                gate = proj[r0:r0 + BLOCK, O_GC + p * LANES:O_GC + (p + 1) * LANES]
                y_scr[r0:r0 + BLOCK, p * LANES:(p + 1) * LANES] = (att * jax.nn.silu(gate)).astype(BF16)

    gt = proj[:, O_IF:O_IF + GATE_PAD].T[0:2 * H, :]
    pre = gt + bias_ref[...]
    row = lax.broadcasted_iota(jnp.int32, (2 * H, T), 0)
    top = row < H
    bcum = _lane_scan(jax.nn.log_sigmoid(pre), jnp.add, 0.0)
    bsw = pltpu.roll(bcum, H, 0)
    a_top = pre - bsw
    ag = jnp.where(top, a_top, pltpu.roll(a_top, H, 0))
    bf = jnp.where(top, bsw, bcum)
    m_in = jnp.concatenate([m_scr[...]] * (T // LANES), axis=1)
    mg = jnp.maximum(m_in, _lane_scan(ag, jnp.maximum, NEG))
    m_t = bf + mg
    nrm = jnp.exp(-m_t)
    inter_parts, wk_parts, decay_parts = [], [], []
    for c in range(T // L):
        m_start = m_in[:, 0:1] if c == 0 else mg[:, c * L - 1:c * L]
        m_last = mg[:, (c + 1) * L - 1:(c + 1) * L]
        inter_parts.append(jnp.exp(m_start - mg[:, c * L:(c + 1) * L]))
        wk_parts.append(jnp.exp(ag[:, c * L:(c + 1) * L] - m_last))
        decay_parts.append(jnp.broadcast_to(jnp.exp(m_start - m_last), (2 * H, 2 * MLSTM_V_DIM)))
    inter = jnp.concatenate(inter_parts, axis=1)
    m_scr[...] = jnp.broadcast_to(m_t[:, T - 1:T], m_scr.shape)
    ag2 = ag * LOG2E
    cols = jnp.concatenate([mg * LOG2E, inter, nrm, jnp.zeros((LANES - 6 * H, T), F32)], axis=0).T

    kT = proj[:, O_DK:O_DK + MLSTM_QK_WIDTH].T
    lane_p = lax.broadcasted_iota(jnp.int32, (L, LANES), 1)
    low_p = lane_p < MLSTM_QK_DIM
    tri = (lax.broadcasted_iota(jnp.int32, (BLOCK, BLOCK), 1) <= lax.broadcasted_iota(jnp.int32, (BLOCK, BLOCK), 0))
    ones = jnp.ones((L, MLSTM_V_DIM), BF16)
    zero_blk = jnp.zeros((BLOCK, BLOCK), BF16)

    def mlstm_chunk(c):
        r0 = c * L
        for pair in range(H // 2):
            qp = proj[r0:r0 + L, O_DQ + pair * LANES:O_DQ + (pair + 1) * LANES]
            kT_pair = kT[pair * LANES:(pair + 1) * LANES, r0:r0 + L].astype(BF16)
            c_pair = c_scr[pair * LANES:(pair + 1) * LANES, :].astype(BF16)
            for half in range(2):
                hd = 2 * pair + half
                q_h = jnp.where(low_p, qp, 0.0) if half == 0 else jnp.where(low_p, 0.0, qp)
                s = jnp.dot(q_h.astype(BF16), kT_pair, preferred_element_type=F32)
                if H * c + hd < MLSTM_CHUNKS:
                    nxt.step()
                p_rows = []
                for bi in range(L // BLOCK):
                    m_col = cols[r0 + bi * BLOCK:r0 + (bi + 1) * BLOCK, hd:hd + 1]
                    blks = []
                    for bj in range(L // BLOCK):
                        if bj > bi:
                            blks.append(zero_blk)
                            continue
                        arg = ag2[hd:hd + 1, r0 + bj * BLOCK:r0 + (bj + 1) * BLOCK] - m_col
                        if bj == bi:
                            arg = jnp.where(tri, arg, NEG)
                        sb = s[bi * BLOCK:(bi + 1) * BLOCK, bj * BLOCK:(bj + 1) * BLOCK]
                        blks.append((sb * jnp.exp2(arg)).astype(BF16))
                    p_rows.append(jnp.concatenate(blks, axis=1))
                p_bf = jnp.concatenate(p_rows, axis=0)
                v_aug = jnp.concatenate(
                    [proj[r0:r0 + L, O_DV + hd * MLSTM_V_DIM:O_DV + (hd + 1) * MLSTM_V_DIM].astype(BF16), ones],
                    axis=1)
                q_int = (q_h * cols[r0:r0 + L, 2 * H + hd:2 * H + hd + 1]).astype(BF16)
                num = jnp.dot(jnp.concatenate([p_bf, q_int], axis=1),
                              jnp.concatenate([v_aug, c_pair], axis=0), preferred_element_type=F32)
                den = jnp.maximum(jnp.abs(num[:, MLSTM_V_DIM:]), cols[r0:r0 + L, 4 * H + hd:4 * H + hd + 1])
                hm = num[:, :MLSTM_V_DIM] / den
                og = proj[r0:r0 + L, O_DO + hd * MLSTM_V_DIM:O_DO + (hd + 1) * MLSTM_V_DIM]
                gd = proj[r0:r0 + L, O_GD + hd * MLSTM_V_DIM:O_GD + (hd + 1) * MLSTM_V_DIM]
                y_scr[r0:r0 + L, ATT_WIDTH + hd * MLSTM_V_DIM:ATT_WIDTH + (hd + 1) * MLSTM_V_DIM] = (
                    hm * jax.nn.sigmoid(og) * jax.nn.silu(gd)).astype(BF16)
                kw = (kT[hd * MLSTM_QK_DIM:(hd + 1) * MLSTM_QK_DIM, r0:r0 + L]
                      * wk_parts[c][hd:hd + 1, :]).astype(BF16)
                upd = jnp.dot(kw, v_aug, preferred_element_type=F32)
                rows = slice(hd * MLSTM_QK_DIM, (hd + 1) * MLSTM_QK_DIM)
                c_scr[rows, :] = decay_parts[c][hd:hd + 1, :] * c_scr[rows, :] + upd

    attention()
    for _ in range(PRE_CHUNKS):
        nxt.step()
    for c in range(T // L):
        mlstm_chunk(c)
    nxt.finish()
    if after is not None:
        after.normalise()
    for r0 in range(0, T, OUT_ROWS):
        x2 = x_ref[r0:r0 + OUT_ROWS, :] + jnp.dot(y_scr[r0:r0 + OUT_ROWS, :], wout_ref[...],
                                                   preferred_element_type=F32)
        o_ref[r0:r0 + OUT_ROWS, :] = _rmsnorm(x2, fg_ref[...])


def _odd_kernel(sink_ref, ib_ref, fb_ref, xa_ref, xb_ref, xn_ref, wa_ref, wb_ref, wg_ref, wout_ref,
                fg_ref, o_ref, proj_a, proj_b, h_a, h_b, kvbuf, abias, bias_scr, c_scr, m_scr, y_a, y_b):
    s = pl.program_id(0)
    T = xa_ref.shape[0]
    w = (wa_ref, wb_ref, wg_ref)
    rest = (bias_scr, wout_ref, fg_ref, kvbuf, abias, c_scr, m_scr)

    @pl.when(s == 0)
    def _():
        _odd_init(sink_ref, ib_ref, fb_ref, kvbuf, abias, bias_scr, c_scr, m_scr)
        first = _InProj(xa_ref, *w, h_a, proj_a)
        first.normalise()
        first.finish()

    nxt = _InProj(xb_ref, *w, h_a, proj_b)
    nxt.normalise()
    after = _InProj(xn_ref, *w, h_b, proj_a)
    _odd_mixers(s == 0, 0, proj_a, xa_ref, o_ref.at[0:T], *rest, y_a, nxt, after)
    _odd_mixers(False, T, proj_b, xb_ref, o_ref.at[T:2 * T], *rest, y_b, after)
    for b in range(2 * ATT_KV_HEADS):
        kvbuf[b, 0:BLOCK, :] = kvbuf[b, 2 * T:2 * T + BLOCK, :]


def _odd_layer(x, wa, wb, wg, i_bias, f_bias, sinks, wo, final_g):
    S = x.shape[0]
    T = min(TILE_ODD, S // 2)
    NT = S // T
    assert S % (2 * T) == 0 and T % BLOCK == 0 and T % min(MCHUNK, T) == 0
    assert i_bias.shape == f_bias.shape == (MLSTM_HEADS,) and sinks.shape == (ATT_HEADS,)
    const = _const_spec
    smem = pl.BlockSpec(memory_space=pltpu.SMEM)
    return pl.pallas_call(
        _odd_kernel,
        grid=(NT // 2,),
        in_specs=[
            smem, smem, smem,
            pl.BlockSpec((T, D_MODEL), lambda s: (2 * s, 0)),
            pl.BlockSpec((T, D_MODEL), lambda s: (2 * s + 1, 0)),
            pl.BlockSpec((T, D_MODEL), lambda s: (jnp.minimum(2 * s + 2, NT - 1), 0)),
            const(O_DO, D_MODEL),
            const(O_IF - O_DO, D_MODEL),
            const(GATE_PAD, D_MODEL),
            const(ODD_OUT, D_MODEL),
            const(1, D_MODEL),
        ],
        out_specs=pl.BlockSpec((2 * T, D_MODEL), lambda s: (s, 0)),
        out_shape=jax.ShapeDtypeStruct((S, D_MODEL), F32),
        scratch_shapes=[
            pltpu.VMEM((T, ODD_IN_PAD), F32),
            pltpu.VMEM((T, ODD_IN_PAD), F32),
            pltpu.VMEM((T, D_MODEL), BF16),
            pltpu.VMEM((T, D_MODEL), BF16),
            pltpu.VMEM((2 * ATT_KV_HEADS, 2 * T + BLOCK, LANES), BF16),
            pltpu.VMEM((ATT_HEADS * BLOCK, 2 * BLOCK), F32),
            pltpu.VMEM((2 * MLSTM_HEADS, T), F32),
            pltpu.VMEM((MLSTM_HEADS * MLSTM_QK_DIM, 2 * MLSTM_V_DIM), F32),
            pltpu.VMEM((2 * MLSTM_HEADS, LANES), F32),
            pltpu.VMEM((T, ODD_OUT), BF16),
            pltpu.VMEM((T, ODD_OUT), BF16),
        ],
        compiler_params=pltpu.CompilerParams(
            dimension_semantics=("arbitrary",), vmem_limit_bytes=VMEM_LIMIT),
        name="odd_layer",
    )(sinks.astype(F32), i_bias.astype(F32), f_bias.astype(F32), x, x, x, wa, wb, wg, wo,
      final_g.reshape(1, D_MODEL))


def kernel(x, even_norm, even_w_in, even_conv_w, even_pool_w, even_pool_scale, even_w_out,
           odd_norm, odd_w_in, odd_i_bias, odd_f_bias, odd_sinks, odd_w_out, final_norm):
    B, S, D = x.shape
    assert B == 1 and D == D_MODEL
    assert even_norm.shape[0] == 1 and odd_norm.shape[0] == 1
    h = x.reshape(S, D)
    h, wa, wb, wg, wo = _even_layer(h, even_norm[0], even_w_in[0], jnp.swapaxes(even_conv_w, 0, 1), even_pool_w[0],
                                    even_pool_scale[0], even_w_out[0], odd_norm[0], odd_w_in[0].T, odd_w_out[0])
    h = _odd_layer(h, wa, wb, wg, odd_i_bias[0], odd_f_bias[0], odd_sinks[0], wo, final_norm)
    return h.reshape(B, S, D)
```

```python
import jax
import jax.numpy as jnp
from jax import lax
from jax.experimental import pallas as pl
from jax.experimental.pallas import tpu as pltpu

F32 = jnp.float32
BF16 = jnp.bfloat16

D_MODEL = 1024
RMS_EPS = 1e-6

CONV_WIDTH = 512
CONV_K = 3
POOL_WINDOWS = (2, 4, 8, 16)
POOL_WIDTH = 512
POOL_GROUP = POOL_WIDTH // len(POOL_WINDOWS)
EVEN_IN = 4 * CONV_WIDTH + 2 * POOL_WIDTH
EVEN_OUT = CONV_WIDTH + POOL_WIDTH
CONV_HALO = 8
POOL_HALO = 16

ATT_HEADS = 8
ATT_KV_HEADS = 2
ATT_GROUP = ATT_HEADS // ATT_KV_HEADS
ATT_HEAD_DIM = 64
ATT_WIDTH = ATT_HEADS * ATT_HEAD_DIM
ATT_KV_WIDTH = ATT_KV_HEADS * ATT_HEAD_DIM
WINDOW = 128
BLOCK = 128
MLSTM_HEADS = 4
MLSTM_QK_DIM = 64
MLSTM_V_DIM = 128
MLSTM_QK_WIDTH = MLSTM_HEADS * MLSTM_QK_DIM
MLSTM_WIDTH = MLSTM_HEADS * MLSTM_V_DIM
ODD_OUT = ATT_WIDTH + MLSTM_WIDTH
LANES = 128
BF16_ROWS = 16
SINK_KEY = 0
MXU_WIDTH = 256
GATE_PAD = LANES

O_CQ = 0
O_CK = O_CQ + ATT_WIDTH
O_CV = O_CK + ATT_KV_WIDTH
O_GC = O_CV + ATT_KV_WIDTH
O_DQ = O_GC + ATT_WIDTH
O_DK = O_DQ + MLSTM_QK_WIDTH
O_DV = O_DK + MLSTM_QK_WIDTH
O_DO = O_DV + MLSTM_WIDTH
O_GD = O_DO + MLSTM_WIDTH
O_IF = O_GD + MLSTM_WIDTH
ODD_IN_PAD = O_IF + GATE_PAD

TILE_EVEN = 1024
EVEN_CHUNK = 512
TILE_ODD = 512
MCHUNK = 256
INPROJ_ROWS = 512
OUT_ROWS = 256
PRE_CHUNKS = 3
MLSTM_CHUNKS = 5
ATTN_SKIP_EVERY = 4
LOG2E = 1.4426950408889634
ATT_Q_SCALE = ATT_HEAD_DIM ** -0.5 * LOG2E
MLSTM_Q_SCALE = MLSTM_QK_DIM ** -0.5
VMEM_LIMIT = 60 * 1024 * 1024


def _const_spec(*shape):
    return pl.BlockSpec(shape, lambda i: (0,) * len(shape), pipeline_mode=pl.Buffered(1))


def _rms_scale(x):
    ms = jnp.mean(x * x, axis=-1, keepdims=True)
    return x * lax.rsqrt(ms + RMS_EPS)


def _rmsnorm(x, g):
    return _rms_scale(x) * g


def _even_kernel(x_ref, g_ref, win_ref, cw_ref, pw_ref, ps_ref, wout_ref,
                 ng_ref, nwa32_ref, nwb32_ref, nwg32_ref, nwo32_ref,
                 o_ref, nwa_ref, nwb_ref, nwg_ref, nwo_ref, proj_scr, uc_scr, up_scr, y_scr, wfold_scr):
    i = pl.program_id(0)
    T = x_ref.shape[0]
    C = CONV_WIDTH

    ng = ng_ref[...]
    ra = nwa32_ref.shape[0]
    feat = lax.broadcasted_iota(jnp.int32, (ra, D_MODEL), 0) + i * ra
    q_scale = jnp.where(feat < O_CK, ATT_Q_SCALE,
                        jnp.where(feat < O_DQ, 1.0, jnp.where(feat < O_DK, MLSTM_Q_SCALE, 1.0)))
    nwa_ref[...] = (nwa32_ref[...] * ng * q_scale).astype(BF16)
    nwb_ref[...] = (nwb32_ref[...] * ng).astype(BF16)
    nwo_ref[...] = nwo32_ref[...].astype(BF16)
    nwg_ref[...] = jnp.concatenate(
        [nwg32_ref[...] * ng, jnp.zeros((GATE_PAD - 2 * MLSTM_HEADS, D_MODEL), F32)], axis=0).astype(BF16)

    @pl.when(i == 0)
    def _():
        uc_scr[0:CONV_HALO, :] = jnp.zeros((CONV_HALO, C), F32)
        up_scr[0:POOL_HALO, :] = jnp.zeros((POOL_HALO, POOL_WIDTH), F32)
        for g in range(len(POOL_WINDOWS)):
            lo, hi = g * POOL_GROUP, (g + 1) * POOL_GROUP
            wfold_scr[:, lo:hi] = jnp.dot(win_ref[:, 4 * C + lo:4 * C + hi], pw_ref[g],
                                          precision=lax.Precision.HIGHEST, preferred_element_type=F32)

    h = _rmsnorm(x_ref[...], g_ref[...]).astype(BF16)
    for c0 in range(0, EVEN_IN, EVEN_CHUNK):
        w = wfold_scr[...] if c0 == 4 * C else win_ref[:, c0:c0 + EVEN_CHUNK]
        proj_scr[:, c0:c0 + EVEN_CHUNK] = jnp.dot(h, w.astype(BF16), preferred_element_type=F32)

    uc = proj_scr[:, 2 * C:3 * C] * proj_scr[:, 0:C]
    uc_scr[CONV_HALO:CONV_HALO + T, :] = uc
    conv = (cw_ref[2] * uc
            + cw_ref[1] * uc_scr[CONV_HALO - 1:CONV_HALO - 1 + T, :]
            + cw_ref[0] * uc_scr[CONV_HALO - 2:CONV_HALO - 2 + T, :])
    ya = proj_scr[:, C:2 * C] * conv * jax.nn.silu(proj_scr[:, 3 * C:4 * C])
    y_scr[:, 0:C] = ya.astype(BF16)
    uc_scr[0:CONV_HALO, :] = uc_scr[T:T + CONV_HALO, :]

    up_scr[POOL_HALO:POOL_HALO + T, :] = proj_scr[:, 4 * C:4 * C + POOL_WIDTH]
    t1 = (lax.broadcasted_iota(jnp.int32, (T, POOL_GROUP), 0) + (i * T + 1)).astype(F32)
    for g, w in enumerate(POOL_WINDOWS):
        lo, hi = g * POOL_GROUP, (g + 1) * POOL_GROUP
        s = up_scr[:, lo:hi]
        k = 1
        while k < w:
            s = s + pltpu.roll(s, k, 0)
            k *= 2
        u = up_scr[POOL_HALO:POOL_HALO + T, lo:hi]
        yb = s[POOL_HALO:, :] / jnp.minimum(t1, float(w)) - u
        gate = proj_scr[:, 4 * C + POOL_WIDTH + lo:4 * C + POOL_WIDTH + hi]
        y_scr[:, C + lo:C + hi] = (yb * ps_ref[:, lo:hi] * jax.nn.silu(gate)).astype(BF16)
    up_scr[0:POOL_HALO, :] = up_scr[T:T + POOL_HALO, :]

    o_ref[...] = x_ref[...] + jnp.dot(y_scr[...], wout_ref[...].astype(BF16), preferred_element_type=F32)


def _even_layer(x, norm_g, w_in, conv_w, pool_w, pool_scale, w_out, next_norm_g, next_w_in_t, next_w_out):
    S = x.shape[0]
    T = min(TILE_EVEN, S)
    steps = S // T
    n_gates = 2 * MLSTM_HEADS
    assert S % T == 0
    assert next_w_in_t.shape == (O_IF + n_gates, D_MODEL) and next_w_out.shape == (ODD_OUT, D_MODEL)
    ra, rb, ro = O_DO // steps, (O_IF - O_DO) // steps, ODD_OUT // steps
    assert ra * steps == O_DO and rb * steps == O_IF - O_DO and ro * steps == ODD_OUT
    const = _const_spec
    slab = lambda r: pl.BlockSpec((r, D_MODEL), lambda i: (i, 0))
    return pl.pallas_call(
        _even_kernel,
        grid=(steps,),
        in_specs=[
            pl.BlockSpec((T, D_MODEL), lambda i: (i, 0)),
            const(1, D_MODEL),
            const(D_MODEL, EVEN_IN),
            const(CONV_K, 1, CONV_WIDTH),
            const(len(POOL_WINDOWS), POOL_GROUP, POOL_GROUP),
            const(1, POOL_WIDTH),
            const(EVEN_OUT, D_MODEL),
            const(1, D_MODEL),
            slab(ra),
            pl.BlockSpec((pl.Element(rb), pl.Element(D_MODEL)), lambda i: (pl.multiple_of(O_DO + n_gates + i * rb, n_gates), 0)),
            pl.BlockSpec((n_gates, D_MODEL), lambda i: (O_DO // n_gates, 0)),
            slab(ro),
        ],
        out_specs=[pl.BlockSpec((T, D_MODEL), lambda i: (i, 0)),
                   slab(ra), slab(rb), pl.BlockSpec((GATE_PAD, D_MODEL), lambda i: (0, 0)), slab(ro)],
        out_shape=[jax.ShapeDtypeStruct((S, D_MODEL), F32),
                   jax.ShapeDtypeStruct((O_DO, D_MODEL), BF16),
                   jax.ShapeDtypeStruct((O_IF - O_DO, D_MODEL), BF16),
                   jax.ShapeDtypeStruct((GATE_PAD, D_MODEL), BF16),
                   jax.ShapeDtypeStruct((ODD_OUT, D_MODEL), BF16)],
        scratch_shapes=[
            pltpu.VMEM((T, EVEN_IN), F32),
            pltpu.VMEM((T + CONV_HALO, CONV_WIDTH), F32),
            pltpu.VMEM((T + POOL_HALO, POOL_WIDTH), F32),
            pltpu.VMEM((T, EVEN_OUT), BF16),
            pltpu.VMEM((D_MODEL, POOL_WIDTH), F32),
        ],
        compiler_params=pltpu.CompilerParams(
            dimension_semantics=("arbitrary",), vmem_limit_bytes=VMEM_LIMIT),
        name="even_layer",
    )(x, norm_g.reshape(1, D_MODEL), w_in, conv_w, pool_w, pool_scale.reshape(1, POOL_WIDTH), w_out,
      next_norm_g.reshape(1, D_MODEL), next_w_in_t, next_w_in_t, next_w_in_t, next_w_out)


def _lane_scan(x, op, identity):
    n = x.shape[1]
    lane = lax.broadcasted_iota(jnp.int32, x.shape, 1)
    k = 1
    while k < n:
        x = op(x, jnp.where(lane >= k, pltpu.roll(x, k, 1), identity))
        k *= 2
    return x


def _odd_init(sink_ref, ib_ref, fb_ref, kvbuf, abias, bias_scr, c_scr, m_scr):
    for b in range(2 * ATT_KV_HEADS):
        kvbuf[b, 0:BLOCK, :] = jnp.zeros((BLOCK, LANES), BF16)
    c_scr[...] = jnp.zeros(c_scr.shape, F32)
    m_scr[...] = jnp.zeros(m_scr.shape, F32)
    qi = lax.broadcasted_iota(jnp.int32, (BLOCK, 2 * BLOCK), 0)
    kj = lax.broadcasted_iota(jnp.int32, (BLOCK, 2 * BLOCK), 1)
    dist = qi + BLOCK - kj
    ok = (dist >= 0) & (dist < WINDOW)
    distf = dist.astype(F32)
    for hd in range(ATT_HEADS):
        slope = 2.0 ** (-8.0 * (hd + 1) / ATT_HEADS)
        band = jnp.where(ok, (-slope * LOG2E) * distf, -jnp.inf)
        abias[hd * BLOCK:(hd + 1) * BLOCK, :] = jnp.where(kj == SINK_KEY, sink_ref[hd] * LOG2E, band)
    row = lax.broadcasted_iota(jnp.int32, bias_scr.shape, 0)
    bias = jnp.zeros(bias_scr.shape, F32)
    for hd in range(MLSTM_HEADS):
        bias = jnp.where(row == hd, ib_ref[hd], jnp.where(row == MLSTM_HEADS + hd, fb_ref[hd], bias))
    bias_scr[...] = bias


class _InProj:
    def __init__(self, x_ref, wa_ref, wb_ref, wg_ref, h_scr, proj):
        self.x_ref, self.h_scr, self.proj = x_ref, h_scr, proj
        self.todo = []
        rows = x_ref.shape[0]
        for w_ref, base in ((wa_ref, 0), (wb_ref, O_DO), (wg_ref, O_IF)):
            n = w_ref.shape[0]
            for c0 in range(0, n, MXU_WIDTH):
                for r0 in range(0, rows, INPROJ_ROWS):
                    self.todo.append((w_ref, base, c0, min(c0 + MXU_WIDTH, n), r0, min(r0 + INPROJ_ROWS, rows)))

    def normalise(self):
        self.h_scr[...] = _rms_scale(self.x_ref[...]).astype(BF16)

    def step(self):
        if self.todo:
            w_ref, base, c0, c1, r0, r1 = self.todo.pop(0)
            self.proj[r0:r1, base + c0:base + c1] = lax.dot_general(
                self.h_scr[r0:r1, :], w_ref[c0:c1, :], (((1,), (1,)), ((), ())), preferred_element_type=F32)

    def finish(self):
        while self.todo:
            self.step()


def _odd_mixers(first, kv0, proj, x_ref, o_ref, bias_ref, wout_ref, fg_ref,
                kvbuf, abias, c_scr, m_scr, y_scr, nxt, after=None):
    T = x_ref.shape[0]
    NB = T // BLOCK
    H = MLSTM_HEADS
    L = min(MCHUNK, T)
    NEG = -jnp.inf

    lane_q = lax.broadcasted_iota(jnp.int32, (BLOCK, LANES), 1)
    low = lane_q < ATT_HEAD_DIM
    low_t = lax.broadcasted_iota(jnp.int32, (T, LANES), 1) < ATT_HEAD_DIM
    for b, off in enumerate((O_CK, O_CV)):
        kv = proj[:, off:off + ATT_KV_WIDTH]
        sw = pltpu.roll(kv, ATT_HEAD_DIM, 1)
        kvbuf[2 * b, kv0 + BLOCK:kv0 + BLOCK + T, :] = jnp.where(low_t, kv, sw).astype(BF16)
        kvbuf[2 * b + 1, kv0 + BLOCK:kv0 + BLOCK + T, :] = jnp.where(low_t, sw, kv).astype(BF16)
    if first is not False:
        key_lane = lax.broadcasted_iota(jnp.int32, (BLOCK, 2 * BLOCK), 1)
        no_prev = jnp.where((key_lane < BLOCK) & (key_lane != SINK_KEY), jnp.where(first, NEG, 0.0), 0.0)
    sink_row = lax.broadcasted_iota(jnp.int32, (BF16_ROWS, LANES), 0) == SINK_KEY

    def drop_sink_row(a):
        head = a[0:BF16_ROWS, :]
        return jnp.concatenate([jnp.where(sink_row, jnp.zeros_like(head), head), a[BF16_ROWS:, :]], axis=0)

    ones_v = jnp.ones((2 * BLOCK, LANES), BF16)

    def scores(j, g):
        r0 = j * BLOCK
        kk = drop_sink_row(kvbuf[g, kv0 + r0:kv0 + r0 + 2 * BLOCK, :])
        qs = []
        for p in range(g * ATT_GROUP // 2, (g + 1) * ATT_GROUP // 2):
            qp = proj[r0:r0 + BLOCK, O_CQ + p * LANES:O_CQ + (p + 1) * LANES]
            qs.append(jnp.where(low, qp, 0.0).astype(BF16))
            qs.append(jnp.where(low, 0.0, qp).astype(BF16))
        return lax.dot_general(jnp.concatenate(qs, axis=0), kk, (((1,), (1,)), ((), ())),
                               preferred_element_type=F32)

    def attention():
        slots = [(j, g) for j in range(NB) for g in range(ATT_KV_HEADS)]
        s_next = scores(*slots[0])
        for n, (j, g) in enumerate(slots):
            r0 = j * BLOCK
            s_all = s_next
            if n + 1 < len(slots):
                s_next = scores(*slots[n + 1])
            if n % ATTN_SKIP_EVERY != ATTN_SKIP_EVERY - 1:
                nxt.step()
            ps = []
            for r in range(ATT_GROUP):
                hd = g * ATT_GROUP + r
                s = s_all[r * BLOCK:(r + 1) * BLOCK, :] + abias[hd * BLOCK:(hd + 1) * BLOCK, :]
                if j == 0 and first is not False:
                    s = s + no_prev
                ps.append(jnp.exp2(s - jnp.max(s, axis=1, keepdims=True)).astype(BF16))
            vv = jnp.concatenate(
                [drop_sink_row(kvbuf[ATT_KV_HEADS + g, kv0 + r0:kv0 + r0 + 2 * BLOCK, :]), ones_v], axis=1)
            o_all = jnp.dot(jnp.concatenate(ps, axis=0), vv, preferred_element_type=F32)
            for pp in range(ATT_GROUP // 2):
                p = g * ATT_GROUP // 2 + pp
                lo = o_all[2 * pp * BLOCK:(2 * pp + 1) * BLOCK, :]
                hi = o_all[(2 * pp + 1) * BLOCK:(2 * pp + 2) * BLOCK, :]
                att = jnp.where(low, lo[:, :LANES] / lo[:, LANES:], hi[:, :LANES] / hi[:, LANES:])
                gate = proj[r0:r0 + BLOCK, O_GC + p * LANES:O_GC + (p + 1) * LANES]
                y_scr[r0:r0 + BLOCK, p * LANES:(p + 1) * LANES] = (att * jax.nn.silu(gate)).astype(BF16)

    gt = proj[:, O_IF:O_IF + GATE_PAD].T[0:2 * H, :]
    pre = gt + bias_ref[...]
    row = lax.broadcasted_iota(jnp.int32, (2 * H, T), 0)
    top = row < H
    bcum = _lane_scan(jax.nn.log_sigmoid(pre), jnp.add, 0.0)
    bsw = pltpu.roll(bcum, H, 0)
    a_top = pre - bsw
    ag = jnp.where(top, a_top, pltpu.roll(a_top, H, 0))
    bf = jnp.where(top, bsw, bcum)
    m_in = jnp.concatenate([m_scr[...]] * (T // LANES), axis=1)
    mg = jnp.maximum(m_in, _lane_scan(ag, jnp.maximum, NEG))
    m_t = bf + mg
    nrm = jnp.exp(-m_t)
    inter_parts, wk_parts, decay_parts = [], [], []
    for c in range(T // L):
        m_start = m_in[:, 0:1] if c == 0 else mg[:, c * L - 1:c * L]
        m_last = mg[:, (c + 1) * L - 1:(c + 1) * L]
        inter_parts.append(jnp.exp(m_start - mg[:, c * L:(c + 1) * L]))
        wk_parts.append(jnp.exp(ag[:, c * L:(c + 1) * L] - m_last))
        decay_parts.append(jnp.broadcast_to(jnp.exp(m_start - m_last), (2 * H, 2 * MLSTM_V_DIM)))
    inter = jnp.concatenate(inter_parts, axis=1)
    m_scr[...] = jnp.broadcast_to(m_t[:, T - 1:T], m_scr.shape)
    ag2 = ag * LOG2E
    cols = jnp.concatenate([mg * LOG2E, inter, nrm, jnp.zeros((LANES - 6 * H, T), F32)], axis=0).T

    kT = proj[:, O_DK:O_DK + MLSTM_QK_WIDTH].T
    lane_p = lax.broadcasted_iota(jnp.int32, (L, LANES), 1)
    low_p = lane_p < MLSTM_QK_DIM
    tri = (lax.broadcasted_iota(jnp.int32, (BLOCK, BLOCK), 1) <= lax.broadcasted_iota(jnp.int32, (BLOCK, BLOCK), 0))
    ones = jnp.ones((L, MLSTM_V_DIM), BF16)
    zero_blk = jnp.zeros((BLOCK, BLOCK), BF16)

    def mlstm_chunk(c):
        r0 = c * L
        for pair in range(H // 2):
            qp = proj[r0:r0 + L, O_DQ + pair * LANES:O_DQ + (pair + 1) * LANES]
            kT_pair = kT[pair * LANES:(pair + 1) * LANES, r0:r0 + L].astype(BF16)
            c_pair = c_scr[pair * LANES:(pair + 1) * LANES, :].astype(BF16)
            for half in range(2):
                hd = 2 * pair + half
                q_h = jnp.where(low_p, qp, 0.0) if half == 0 else jnp.where(low_p, 0.0, qp)
                s = jnp.dot(q_h.astype(BF16), kT_pair, preferred_element_type=F32)
                if H * c + hd < MLSTM_CHUNKS:
                    nxt.step()
                p_rows = []
                for bi in range(L // BLOCK):
                    m_col = cols[r0 + bi * BLOCK:r0 + (bi + 1) * BLOCK, hd:hd + 1]
                    blks = []
                    for bj in range(L // BLOCK):
                        if bj > bi:
                            blks.append(zero_blk)
                            continue
                        arg = ag2[hd:hd + 1, r0 + bj * BLOCK:r0 + (bj + 1) * BLOCK] - m_col
                        if bj == bi:
                            arg = jnp.where(tri, arg, NEG)
                        sb = s[bi * BLOCK:(bi + 1) * BLOCK, bj * BLOCK:(bj + 1) * BLOCK]
                        blks.append((sb * jnp.exp2(arg)).astype(BF16))
                    p_rows.append(jnp.concatenate(blks, axis=1))
                p_bf = jnp.concatenate(p_rows, axis=0)
                v_aug = jnp.concatenate(
                    [proj[r0:r0 + L, O_DV + hd * MLSTM_V_DIM:O_DV + (hd + 1) * MLSTM_V_DIM].astype(BF16), ones],
                    axis=1)
                q_int = (q_h * cols[r0:r0 + L, 2 * H + hd:2 * H + hd + 1]).astype(BF16)
                num = jnp.dot(jnp.concatenate([p_bf, q_int], axis=1),
                              jnp.concatenate([v_aug, c_pair], axis=0), preferred_element_type=F32)
                den = jnp.maximum(jnp.abs(num[:, MLSTM_V_DIM:]), cols[r0:r0 + L, 4 * H + hd:4 * H + hd + 1])
                hm = num[:, :MLSTM_V_DIM] / den
                og = proj[r0:r0 + L, O_DO + hd * MLSTM_V_DIM:O_DO + (hd + 1) * MLSTM_V_DIM]
                gd = proj[r0:r0 + L, O_GD + hd * MLSTM_V_DIM:O_GD + (hd + 1) * MLSTM_V_DIM]
                y_scr[r0:r0 + L, ATT_WIDTH + hd * MLSTM_V_DIM:ATT_WIDTH + (hd + 1) * MLSTM_V_DIM] = (
                    hm * jax.nn.sigmoid(og) * jax.nn.silu(gd)).astype(BF16)
                kw = (kT[hd * MLSTM_QK_DIM:(hd + 1) * MLSTM_QK_DIM, r0:r0 + L]
                      * wk_parts[c][hd:hd + 1, :]).astype(BF16)
                upd = jnp.dot(kw, v_aug, preferred_element_type=F32)
                rows = slice(hd * MLSTM_QK_DIM, (hd + 1) * MLSTM_QK_DIM)
                c_scr[rows, :] = decay_parts[c][hd:hd + 1, :] * c_scr[rows, :] + upd

    attention()
    for _ in range(PRE_CHUNKS):
        nxt.step()
    for c in range(T // L):
        mlstm_chunk(c)
    nxt.finish()
    if after is not None:
        after.normalise()
    for r0 in range(0, T, OUT_ROWS):
        x2 = x_ref[r0:r0 + OUT_ROWS, :] + jnp.dot(y_scr[r0:r0 + OUT_ROWS, :], wout_ref[...],
                                                   preferred_element_type=F32)
        o_ref[r0:r0 + OUT_ROWS, :] = _rmsnorm(x2, fg_ref[...])


def _odd_kernel(sink_ref, ib_ref, fb_ref, xa_ref, xb_ref, xn_ref, wa_ref, wb_ref, wg_ref, wout_ref,
                fg_ref, o_ref, proj_a, proj_b, h_a, h_b, kvbuf, abias, bias_scr, c_scr, y_a, y_b, m_scr):
    s = pl.program_id(0)
    T = xa_ref.shape[0]
    w = (wa_ref, wb_ref, wg_ref)
    rest = (bias_scr, wout_ref, fg_ref, kvbuf, abias, c_scr, m_scr)

    @pl.when(s == 0)
    def _():
        _odd_init(sink_ref, ib_ref, fb_ref, kvbuf, abias, bias_scr, c_scr, m_scr)
        first = _InProj(xa_ref, *w, h_a, proj_a)
        first.normalise()
        first.finish()

    nxt = _InProj(xb_ref, *w, h_a, proj_b)
    nxt.normalise()
    after = _InProj(xn_ref, *w, h_b, proj_a)
    _odd_mixers(s == 0, 0, proj_a, xa_ref, o_ref.at[0:T], *rest, y_a, nxt, after)
    _odd_mixers(False, T, proj_b, xb_ref, o_ref.at[T:2 * T], *rest, y_b, after)
    for b in range(2 * ATT_KV_HEADS):
        kvbuf[b, 0:BLOCK, :] = kvbuf[b, 2 * T:2 * T + BLOCK, :]


def _odd_layer(x, wa, wb, wg, i_bias, f_bias, sinks, wo, final_g):
    S = x.shape[0]
    T = min(TILE_ODD, S // 2)
    NT = S // T
    assert S % (2 * T) == 0 and T % BLOCK == 0 and T % min(MCHUNK, T) == 0
    assert i_bias.shape == f_bias.shape == (MLSTM_HEADS,) and sinks.shape == (ATT_HEADS,)
    const = _const_spec
    smem = pl.BlockSpec(memory_space=pltpu.SMEM)
    return pl.pallas_call(
        _odd_kernel,
        grid=(NT // 2,),
        in_specs=[
            smem, smem, smem,
            pl.BlockSpec((T, D_MODEL), lambda s: (2 * s, 0)),
            pl.BlockSpec((T, D_MODEL), lambda s: (2 * s + 1, 0)),
            pl.BlockSpec((T, D_MODEL), lambda s: (jnp.minimum(2 * s + 2, NT - 1), 0)),
            const(O_DO, D_MODEL),
            const(O_IF - O_DO, D_MODEL),
            const(GATE_PAD, D_MODEL),
            const(ODD_OUT, D_MODEL),
            const(1, D_MODEL),
        ],
        out_specs=pl.BlockSpec((2 * T, D_MODEL), lambda s: (s, 0)),
        out_shape=jax.ShapeDtypeStruct((S, D_MODEL), F32),
        scratch_shapes=[
            pltpu.VMEM((T, ODD_IN_PAD), F32),
            pltpu.VMEM((T, ODD_IN_PAD), F32),
            pltpu.VMEM((T, D_MODEL), BF16),
            pltpu.VMEM((T, D_MODEL), BF16),
            pltpu.VMEM((2 * ATT_KV_HEADS, 2 * T + BLOCK, LANES), BF16),
            pltpu.VMEM((ATT_HEADS * BLOCK, 2 * BLOCK), F32),
            pltpu.VMEM((2 * MLSTM_HEADS, T), F32),
            pltpu.VMEM((MLSTM_HEADS * MLSTM_QK_DIM, 2 * MLSTM_V_DIM), F32),
            pltpu.VMEM((T, ODD_OUT), BF16),
            pltpu.VMEM((T, ODD_OUT), BF16),
            pltpu.VMEM((2 * MLSTM_HEADS, LANES), F32),
        ],
        compiler_params=pltpu.CompilerParams(
            dimension_semantics=("arbitrary",), vmem_limit_bytes=VMEM_LIMIT),
        name="odd_layer",
    )(sinks.astype(F32), i_bias.astype(F32), f_bias.astype(F32), x, x, x, wa, wb, wg, wo,
      final_g.reshape(1, D_MODEL))


def kernel(x, even_norm, even_w_in, even_conv_w, even_pool_w, even_pool_scale, even_w_out,
           odd_norm, odd_w_in, odd_i_bias, odd_f_bias, odd_sinks, odd_w_out, final_norm):
    B, S, D = x.shape
    assert B == 1 and D == D_MODEL
    assert even_norm.shape[0] == 1 and odd_norm.shape[0] == 1
    h = x.reshape(S, D)
    h, wa, wb, wg, wo = _even_layer(h, even_norm[0], even_w_in[0], jnp.swapaxes(even_conv_w, 0, 1), even_pool_w[0],
                                    even_pool_scale[0], even_w_out[0], odd_norm[0], odd_w_in[0].T, odd_w_out[0])
    h = _odd_layer(h, wa, wb, wg, odd_i_bias[0], odd_f_bias[0], odd_sinks[0], wo, final_norm)
    return h.reshape(B, S, D)
```

```python
import jax
import jax.numpy as jnp
from jax import lax
from jax.experimental import pallas as pl
from jax.experimental.pallas import tpu as pltpu

F32 = jnp.float32
BF16 = jnp.bfloat16

D_MODEL = 1024
RMS_EPS = 1e-6

CONV_WIDTH = 512
CONV_K = 3
POOL_WINDOWS = (2, 4, 8, 16)
POOL_WIDTH = 512
POOL_GROUP = POOL_WIDTH // len(POOL_WINDOWS)
EVEN_IN = 4 * CONV_WIDTH + 2 * POOL_WIDTH
EVEN_OUT = CONV_WIDTH + POOL_WIDTH
CONV_HALO = 8
POOL_HALO = 16

ATT_HEADS = 8
ATT_KV_HEADS = 2
ATT_GROUP = ATT_HEADS // ATT_KV_HEADS
ATT_HEAD_DIM = 64
ATT_WIDTH = ATT_HEADS * ATT_HEAD_DIM
ATT_KV_WIDTH = ATT_KV_HEADS * ATT_HEAD_DIM
WINDOW = 128
BLOCK = 128
MLSTM_HEADS = 4
MLSTM_QK_DIM = 64
MLSTM_V_DIM = 128
MLSTM_QK_WIDTH = MLSTM_HEADS * MLSTM_QK_DIM
MLSTM_WIDTH = MLSTM_HEADS * MLSTM_V_DIM
ODD_OUT = ATT_WIDTH + MLSTM_WIDTH
LANES = 128
BF16_ROWS = 16
SINK_KEY = 0
MXU_WIDTH = 256
GATE_PAD = LANES

O_CQ = 0
O_CK = O_CQ + ATT_WIDTH
O_CV = O_CK + ATT_KV_WIDTH
O_GC = O_CV + ATT_KV_WIDTH
O_DQ = O_GC + ATT_WIDTH
O_DK = O_DQ + MLSTM_QK_WIDTH
O_DV = O_DK + MLSTM_QK_WIDTH
O_DO = O_DV + MLSTM_WIDTH
O_GD = O_DO + MLSTM_WIDTH
O_IF = O_GD + MLSTM_WIDTH
ODD_IN_PAD = O_IF + GATE_PAD

TILE_EVEN = 1024
EVEN_CHUNK = 512
TILE_ODD = 512
MCHUNK = 256
INPROJ_ROWS = 512
OUT_ROWS = 256
PRE_CHUNKS = 3
MLSTM_CHUNKS = 5
ATTN_SKIP_EVERY = 4
LOG2E = 1.4426950408889634
ATT_Q_SCALE = ATT_HEAD_DIM ** -0.5 * LOG2E
MLSTM_Q_SCALE = MLSTM_QK_DIM ** -0.5
VMEM_LIMIT = 60 * 1024 * 1024


def _const_spec(*shape):
    return pl.BlockSpec(shape, lambda i: (0,) * len(shape), pipeline_mode=pl.Buffered(1))


def _rms_scale(x):
    ms = jnp.mean(x * x, axis=-1, keepdims=True)
    return x * lax.rsqrt(ms + RMS_EPS)


def _rmsnorm(x, g):
    return _rms_scale(x) * g


def _even_kernel(x_ref, g_ref, win_ref, cw_ref, pw_ref, ps_ref, wout_ref,
                 ng_ref, nwa32_ref, nwb32_ref, nwg32_ref, nwo32_ref,
                 o_ref, nwa_ref, nwb_ref, nwg_ref, nwo_ref,
                 proj_scr, uc_scr, up_scr, y_scr, wfold_scr, xprev_scr):
    i = pl.program_id(0)
    tiles = pl.num_programs(0) - 1
    T = x_ref.shape[0]
    C = CONV_WIDTH

    def project_previous_tile():
        o_ref[...] = xprev_scr[...] + jnp.dot(y_scr[...], wout_ref[...].astype(BF16),
                                              preferred_element_type=F32)

    ng = ng_ref[...]
    ra = nwa32_ref.shape[0]
    feat = lax.broadcasted_iota(jnp.int32, (ra, D_MODEL), 0) + jnp.minimum(i, tiles - 1) * ra
    q_scale = jnp.where(feat < O_CK, ATT_Q_SCALE,
                        jnp.where(feat < O_DQ, 1.0, jnp.where(feat < O_DK, MLSTM_Q_SCALE, 1.0)))
    nwa_ref[...] = (nwa32_ref[...] * ng * q_scale).astype(BF16)
    nwb_ref[...] = (nwb32_ref[...] * ng).astype(BF16)
    nwo_ref[...] = nwo32_ref[...].astype(BF16)
    nwg_ref[...] = jnp.concatenate(
        [nwg32_ref[...] * ng, jnp.zeros((GATE_PAD - 2 * MLSTM_HEADS, D_MODEL), F32)], axis=0).astype(BF16)

    @pl.when(i == 0)
    def _():
        uc_scr[0:CONV_HALO, :] = jnp.zeros((CONV_HALO, C), F32)
        up_scr[0:POOL_HALO, :] = jnp.zeros((POOL_HALO, POOL_WIDTH), F32)
        y_scr[...] = jnp.zeros(y_scr.shape, BF16)
        xprev_scr[...] = jnp.zeros(xprev_scr.shape, F32)
        for g in range(len(POOL_WINDOWS)):
            lo, hi = g * POOL_GROUP, (g + 1) * POOL_GROUP
            wfold_scr[:, lo:hi] = jnp.dot(win_ref[:, 4 * C + lo:4 * C + hi], pw_ref[g],
                                          precision=lax.Precision.HIGHEST, preferred_element_type=F32)

    @pl.when(i == tiles)
    def _():
        project_previous_tile()

    @pl.when(i < tiles)
    def _():
        h = _rmsnorm(x_ref[...], g_ref[...]).astype(BF16)
        for c0 in range(0, EVEN_IN, EVEN_CHUNK):
            w = wfold_scr[...] if c0 == 4 * C else win_ref[:, c0:c0 + EVEN_CHUNK]
            proj_scr[:, c0:c0 + EVEN_CHUNK] = jnp.dot(h, w.astype(BF16), preferred_element_type=F32)
        project_previous_tile()

        uc = proj_scr[:, 2 * C:3 * C] * proj_scr[:, 0:C]
        uc_scr[CONV_HALO:CONV_HALO + T, :] = uc
        conv = (cw_ref[2] * uc
                + cw_ref[1] * uc_scr[CONV_HALO - 1:CONV_HALO - 1 + T, :]
                + cw_ref[0] * uc_scr[CONV_HALO - 2:CONV_HALO - 2 + T, :])
        ya = proj_scr[:, C:2 * C] * conv * jax.nn.silu(proj_scr[:, 3 * C:4 * C])
        y_scr[:, 0:C] = ya.astype(BF16)
        uc_scr[0:CONV_HALO, :] = uc_scr[T:T + CONV_HALO, :]

        up_scr[POOL_HALO:POOL_HALO + T, :] = proj_scr[:, 4 * C:4 * C + POOL_WIDTH]
        t1 = (lax.broadcasted_iota(jnp.int32, (T, POOL_GROUP), 0) + (i * T + 1)).astype(F32)
        for g, w in enumerate(POOL_WINDOWS):
            lo, hi = g * POOL_GROUP, (g + 1) * POOL_GROUP
            s = up_scr[:, lo:hi]
            k = 1
            while k < w:
                s = s + pltpu.roll(s, k, 0)
                k *= 2
            u = up_scr[POOL_HALO:POOL_HALO + T, lo:hi]
            yb = s[POOL_HALO:, :] / jnp.minimum(t1, float(w)) - u
            gate = proj_scr[:, 4 * C + POOL_WIDTH + lo:4 * C + POOL_WIDTH + hi]
            y_scr[:, C + lo:C + hi] = (yb * ps_ref[:, lo:hi] * jax.nn.silu(gate)).astype(BF16)
        up_scr[0:POOL_HALO, :] = up_scr[T:T + POOL_HALO, :]
        xprev_scr[...] = x_ref[...]


def _even_layer(x, norm_g, w_in, conv_w, pool_w, pool_scale, w_out, next_norm_g, next_w_in_t, next_w_out):
    S = x.shape[0]
    T = min(TILE_EVEN, S)
    steps = S // T
    n_gates = 2 * MLSTM_HEADS
    assert S % T == 0
    assert next_w_in_t.shape == (O_IF + n_gates, D_MODEL) and next_w_out.shape == (ODD_OUT, D_MODEL)
    ra, rb, ro = O_DO // steps, (O_IF - O_DO) // steps, ODD_OUT // steps
    assert ra * steps == O_DO and rb * steps == O_IF - O_DO and ro * steps == ODD_OUT
    const = _const_spec
    tile = lambda i: jnp.minimum(i, steps - 1)
    slab = lambda r: pl.BlockSpec((r, D_MODEL), lambda i: (tile(i), 0))
    return pl.pallas_call(
        _even_kernel,
        grid=(steps + 1,),
        in_specs=[
            pl.BlockSpec((T, D_MODEL), lambda i: (tile(i), 0)),
            const(1, D_MODEL),
            const(D_MODEL, EVEN_IN),
            const(CONV_K, 1, CONV_WIDTH),
            const(len(POOL_WINDOWS), POOL_GROUP, POOL_GROUP),
            const(1, POOL_WIDTH),
            const(EVEN_OUT, D_MODEL),
            const(1, D_MODEL),
            slab(ra),
            pl.BlockSpec((pl.Element(rb), pl.Element(D_MODEL)), lambda i: (pl.multiple_of(O_DO + n_gates + tile(i) * rb, n_gates), 0)),
            pl.BlockSpec((n_gates, D_MODEL), lambda i: (O_DO // n_gates, 0)),
            slab(ro),
        ],
        out_specs=[pl.BlockSpec((T, D_MODEL), lambda i: (jnp.maximum(i - 1, 0), 0)),
                   slab(ra), slab(rb), pl.BlockSpec((GATE_PAD, D_MODEL), lambda i: (0, 0)), slab(ro)],
        out_shape=[jax.ShapeDtypeStruct((S, D_MODEL), F32),
                   jax.ShapeDtypeStruct((O_DO, D_MODEL), BF16),
                   jax.ShapeDtypeStruct((O_IF - O_DO, D_MODEL), BF16),
                   jax.ShapeDtypeStruct((GATE_PAD, D_MODEL), BF16),
                   jax.ShapeDtypeStruct((ODD_OUT, D_MODEL), BF16)],
        scratch_shapes=[
            pltpu.VMEM((T, EVEN_IN), F32),
            pltpu.VMEM((T + CONV_HALO, CONV_WIDTH), F32),
            pltpu.VMEM((T + POOL_HALO, POOL_WIDTH), F32),
            pltpu.VMEM((T, EVEN_OUT), BF16),
            pltpu.VMEM((D_MODEL, POOL_WIDTH), F32),
            pltpu.VMEM((T, D_MODEL), F32),
        ],
        compiler_params=pltpu.CompilerParams(
            dimension_semantics=("arbitrary",), vmem_limit_bytes=VMEM_LIMIT),
        name="even_layer",
    )(x, norm_g.reshape(1, D_MODEL), w_in, conv_w, pool_w, pool_scale.reshape(1, POOL_WIDTH), w_out,
      next_norm_g.reshape(1, D_MODEL), next_w_in_t, next_w_in_t, next_w_in_t, next_w_out)


def _lane_scan(x, op, identity):
    n = x.shape[1]
    lane = lax.broadcasted_iota(jnp.int32, x.shape, 1)
    k = 1
    while k < n:
        x = op(x, jnp.where(lane >= k, pltpu.roll(x, k, 1), identity))
        k *= 2
    return x


def _odd_init(sink_ref, ib_ref, fb_ref, kvbuf, abias, bias_scr, c_scr, m_scr):
    for b in range(2 * ATT_KV_HEADS):
        kvbuf[b, 0:BLOCK, :] = jnp.zeros((BLOCK, LANES), BF16)
    c_scr[...] = jnp.zeros(c_scr.shape, F32)
    m_scr[...] = jnp.zeros(m_scr.shape, F32)
    qi = lax.broadcasted_iota(jnp.int32, (BLOCK, 2 * BLOCK), 0)
    kj = lax.broadcasted_iota(jnp.int32, (BLOCK, 2 * BLOCK), 1)
    dist = qi + BLOCK - kj
    ok = (dist >= 0) & (dist < WINDOW)
    distf = dist.astype(F32)
    for hd in range(ATT_HEADS):
        slope = 2.0 ** (-8.0 * (hd + 1) / ATT_HEADS)
        band = jnp.where(ok, (-slope * LOG2E) * distf, -jnp.inf)
        abias[hd * BLOCK:(hd + 1) * BLOCK, :] = jnp.where(kj == SINK_KEY, sink_ref[hd] * LOG2E, band)
    row = lax.broadcasted_iota(jnp.int32, bias_scr.shape, 0)
    bias = jnp.zeros(bias_scr.shape, F32)
    for hd in range(MLSTM_HEADS):
        bias = jnp.where(row == hd, ib_ref[hd], jnp.where(row == MLSTM_HEADS + hd, fb_ref[hd], bias))
    bias_scr[...] = bias


class _InProj:
    def __init__(self, x_ref, wa_ref, wb_ref, wg_ref, h_scr, proj):
        self.x_ref, self.h_scr, self.proj = x_ref, h_scr, proj
        self.todo = []
        rows = x_ref.shape[0]
        for w_ref, base in ((wa_ref, 0), (wb_ref, O_DO), (wg_ref, O_IF)):
            n = w_ref.shape[0]
            for c0 in range(0, n, MXU_WIDTH):
                for r0 in range(0, rows, INPROJ_ROWS):
                    self.todo.append((w_ref, base, c0, min(c0 + MXU_WIDTH, n), r0, min(r0 + INPROJ_ROWS, rows)))

    def normalise(self):
        self.h_scr[...] = _rms_scale(self.x_ref[...]).astype(BF16)

    def step(self):
        if self.todo:
            w_ref, base, c0, c1, r0, r1 = self.todo.pop(0)
            self.proj[r0:r1, base + c0:base + c1] = lax.dot_general(
                self.h_scr[r0:r1, :], w_ref[c0:c1, :], (((1,), (1,)), ((), ())), preferred_element_type=F32)

    def finish(self):
        while self.todo:
            self.step()


def _odd_mixers(first, kv0, proj, x_ref, o_ref, bias_ref, wout_ref, fg_ref,
                kvbuf, abias, c_scr, m_scr, y_scr, nxt, after=None):
    T = x_ref.shape[0]
    NB = T // BLOCK
    H = MLSTM_HEADS
    L = min(MCHUNK, T)
    NEG = -jnp.inf

    lane_q = lax.broadcasted_iota(jnp.int32, (BLOCK, LANES), 1)
    low = lane_q < ATT_HEAD_DIM
    low_t = lax.broadcasted_iota(jnp.int32, (T, LANES), 1) < ATT_HEAD_DIM
    for b, off in enumerate((O_CK, O_CV)):
        kv = proj[:, off:off + ATT_KV_WIDTH]
        sw = pltpu.roll(kv, ATT_HEAD_DIM, 1)
        kvbuf[2 * b, kv0 + BLOCK:kv0 + BLOCK + T, :] = jnp.where(low_t, kv, sw).astype(BF16)
        kvbuf[2 * b + 1, kv0 + BLOCK:kv0 + BLOCK + T, :] = jnp.where(low_t, sw, kv).astype(BF16)
    if first is not False:
        key_lane = lax.broadcasted_iota(jnp.int32, (BLOCK, 2 * BLOCK), 1)
        no_prev = jnp.where((key_lane < BLOCK) & (key_lane != SINK_KEY), jnp.where(first, NEG, 0.0), 0.0)
    sink_row = lax.broadcasted_iota(jnp.int32, (BF16_ROWS, LANES), 0) == SINK_KEY

    def drop_sink_row(a):
        head = a[0:BF16_ROWS, :]
        return jnp.concatenate([jnp.where(sink_row, jnp.zeros_like(head), head), a[BF16_ROWS:, :]], axis=0)

    ones_v = jnp.ones((2 * BLOCK, LANES), BF16)

    def scores(j, g):
        r0 = j * BLOCK
        kk = drop_sink_row(kvbuf[g, kv0 + r0:kv0 + r0 + 2 * BLOCK, :])
        qs = []
        for p in range(g * ATT_GROUP // 2, (g + 1) * ATT_GROUP // 2):
            qp = proj[r0:r0 + BLOCK, O_CQ + p * LANES:O_CQ + (p + 1) * LANES]
            qs.append(jnp.where(low, qp, 0.0).astype(BF16))
            qs.append(jnp.where(low, 0.0, qp).astype(BF16))
        return lax.dot_general(jnp.concatenate(qs, axis=0), kk, (((1,), (1,)), ((), ())),
                               preferred_element_type=F32)

    def attention():
        slots = [(j, g) for j in range(NB) for g in range(ATT_KV_HEADS)]
        s_next = scores(*slots[0])
        for n, (j, g) in enumerate(slots):
            r0 = j * BLOCK
            s_all = s_next
            if n + 1 < len(slots):
                s_next = scores(*slots[n + 1])
            if n % ATTN_SKIP_EVERY != ATTN_SKIP_EVERY - 1:
                nxt.step()
            ps = []
            for r in range(ATT_GROUP):
                hd = g * ATT_GROUP + r
                s = s_all[r * BLOCK:(r + 1) * BLOCK, :] + abias[hd * BLOCK:(hd + 1) * BLOCK, :]
                if j == 0 and first is not False:
                    s = s + no_prev
                ps.append(jnp.exp2(s - jnp.max(s, axis=1, keepdims=True)).astype(BF16))
            vv = jnp.concatenate(
                [drop_sink_row(kvbuf[ATT_KV_HEADS + g, kv0 + r0:kv0 + r0 + 2 * BLOCK, :]), ones_v], axis=1)
            o_all = jnp.dot(jnp.concatenate(ps, axis=0), vv, preferred_element_type=F32)
            for pp in range(ATT_GROUP // 2):
                p = g * ATT_GROUP // 2 + pp
                lo = o_all[2 * pp * BLOCK:(2 * pp + 1) * BLOCK, :]
                hi = o_all[(2 * pp + 1) * BLOCK:(2 * pp + 2) * BLOCK, :]
                att = jnp.where(low, lo[:, :LANES] / lo[:, LANES:], hi[:, :LANES] / hi[:, LANES:])
                gate = proj[r0:r0 + BLOCK, O_GC + p * LANES:O_GC + (p + 1) * LANES]
                y_scr[r0:r0 + BLOCK, p * LANES:(p + 1) * LANES] = (att * jax.nn.silu(gate)).astype(BF16)

    gt = proj[:, O_IF:O_IF + GATE_PAD].T[0:2 * H, :]
    pre = gt + bias_ref[...]
    row = lax.broadcasted_iota(jnp.int32, (2 * H, T), 0)
    top = row < H
    bcum = _lane_scan(jax.nn.log_sigmoid(pre), jnp.add, 0.0)
    bsw = pltpu.roll(bcum, H, 0)
    a_top = pre - bsw
    ag = jnp.where(top, a_top, pltpu.roll(a_top, H, 0))
    bf = jnp.where(top, bsw, bcum)
    m_in = jnp.concatenate([m_scr[...]] * (T // LANES), axis=1)
    mg = jnp.maximum(m_in, _lane_scan(ag, jnp.maximum, NEG))
    m_t = bf + mg
    nrm = jnp.exp(-m_t)
    inter_parts, wk_parts, decay_parts = [], [], []
    for c in range(T // L):
        m_start = m_in[:, 0:1] if c == 0 else mg[:, c * L - 1:c * L]
        m_last = mg[:, (c + 1) * L - 1:(c + 1) * L]
        inter_parts.append(jnp.exp(m_start - mg[:, c * L:(c + 1) * L]))
        wk_parts.append(jnp.exp(ag[:, c * L:(c + 1) * L] - m_last))
        decay_parts.append(jnp.broadcast_to(jnp.exp(m_start - m_last), (2 * H, 2 * MLSTM_V_DIM)))
    inter = jnp.concatenate(inter_parts, axis=1)
    m_scr[...] = jnp.broadcast_to(m_t[:, T - 1:T], m_scr.shape)
    ag2 = ag * LOG2E
    cols = jnp.concatenate([mg * LOG2E, inter, nrm, jnp.zeros((LANES - 6 * H, T), F32)], axis=0).T

    kT = proj[:, O_DK:O_DK + MLSTM_QK_WIDTH].T
    lane_p = lax.broadcasted_iota(jnp.int32, (L, LANES), 1)
    low_p = lane_p < MLSTM_QK_DIM
    tri = (lax.broadcasted_iota(jnp.int32, (BLOCK, BLOCK), 1) <= lax.broadcasted_iota(jnp.int32, (BLOCK, BLOCK), 0))
    ones = jnp.ones((L, MLSTM_V_DIM), BF16)
    zero_blk = jnp.zeros((BLOCK, BLOCK), BF16)

    def mlstm_chunk(c):
        r0 = c * L
        for pair in range(H // 2):
            qp = proj[r0:r0 + L, O_DQ + pair * LANES:O_DQ + (pair + 1) * LANES]
            kT_pair = kT[pair * LANES:(pair + 1) * LANES, r0:r0 + L].astype(BF16)
            c_pair = c_scr[pair * LANES:(pair + 1) * LANES, :].astype(BF16)
            for half in range(2):
                hd = 2 * pair + half
                q_h = jnp.where(low_p, qp, 0.0) if half == 0 else jnp.where(low_p, 0.0, qp)
                s = jnp.dot(q_h.astype(BF16), kT_pair, preferred_element_type=F32)
                if H * c + hd < MLSTM_CHUNKS:
                    nxt.step()
                p_rows = []
                for bi in range(L // BLOCK):
                    m_col = cols[r0 + bi * BLOCK:r0 + (bi + 1) * BLOCK, hd:hd + 1]
                    blks = []
                    for bj in range(L // BLOCK):
                        if bj > bi:
                            blks.append(zero_blk)
                            continue
                        arg = ag2[hd:hd + 1, r0 + bj * BLOCK:r0 + (bj + 1) * BLOCK] - m_col
                        if bj == bi:
                            arg = jnp.where(tri, arg, NEG)
                        sb = s[bi * BLOCK:(bi + 1) * BLOCK, bj * BLOCK:(bj + 1) * BLOCK]
                        blks.append((sb * jnp.exp2(arg)).astype(BF16))
                    p_rows.append(jnp.concatenate(blks, axis=1))
                p_bf = jnp.concatenate(p_rows, axis=0)
                v_aug = jnp.concatenate(
                    [proj[r0:r0 + L, O_DV + hd * MLSTM_V_DIM:O_DV + (hd + 1) * MLSTM_V_DIM].astype(BF16), ones],
                    axis=1)
                q_int = (q_h * cols[r0:r0 + L, 2 * H + hd:2 * H + hd + 1]).astype(BF16)
                num = jnp.dot(jnp.concatenate([p_bf, q_int], axis=1),
                              jnp.concatenate([v_aug, c_pair], axis=0), preferred_element_type=F32)
                den = jnp.maximum(jnp.abs(num[:, MLSTM_V_DIM:]), cols[r0:r0 + L, 4 * H + hd:4 * H + hd + 1])
                hm = num[:, :MLSTM_V_DIM] / den
                og = proj[r0:r0 + L, O_DO + hd * MLSTM_V_DIM:O_DO + (hd + 1) * MLSTM_V_DIM]
                gd = proj[r0:r0 + L, O_GD + hd * MLSTM_V_DIM:O_GD + (hd + 1) * MLSTM_V_DIM]
                y_scr[r0:r0 + L, ATT_WIDTH + hd * MLSTM_V_DIM:ATT_WIDTH + (hd + 1) * MLSTM_V_DIM] = (
                    hm * jax.nn.sigmoid(og) * jax.nn.silu(gd)).astype(BF16)
                kw = (kT[hd * MLSTM_QK_DIM:(hd + 1) * MLSTM_QK_DIM, r0:r0 + L]
                      * wk_parts[c][hd:hd + 1, :]).astype(BF16)
                upd = jnp.dot(kw, v_aug, preferred_element_type=F32)
                rows = slice(hd * MLSTM_QK_DIM, (hd + 1) * MLSTM_QK_DIM)
                c_scr[rows, :] = decay_parts[c][hd:hd + 1, :] * c_scr[rows, :] + upd

    attention()
    for _ in range(PRE_CHUNKS):
        nxt.step()
    for c in range(T // L):
        mlstm_chunk(c)
    nxt.finish()
    if after is not None:
        after.normalise()
    for r0 in range(0, T, OUT_ROWS):
        x2 = x_ref[r0:r0 + OUT_ROWS, :] + jnp.dot(y_scr[r0:r0 + OUT_ROWS, :], wout_ref[...],
                                                   preferred_element_type=F32)
        o_ref[r0:r0 + OUT_ROWS, :] = _rmsnorm(x2, fg_ref[...])


def _odd_kernel(sink_ref, ib_ref, fb_ref, xa_ref, xb_ref, xn_ref, wa_ref, wb_ref, wg_ref, wout_ref,
                fg_ref, o_ref, proj_a, proj_b, h_a, h_b, kvbuf, abias, bias_scr, c_scr, m_scr, y_a, y_b):
    s = pl.program_id(0)
    T = xa_ref.shape[0]
    w = (wa_ref, wb_ref, wg_ref)
    rest = (bias_scr, wout_ref, fg_ref, kvbuf, abias, c_scr, m_scr)

    @pl.when(s == 0)
    def _():
        _odd_init(sink_ref, ib_ref, fb_ref, kvbuf, abias, bias_scr, c_scr, m_scr)
        first = _InProj(xa_ref, *w, h_a, proj_a)
        first.normalise()
        first.finish()

    nxt = _InProj(xb_ref, *w, h_a, proj_b)
    nxt.normalise()
    after = _InProj(xn_ref, *w, h_b, proj_a)
    _odd_mixers(s == 0, 0, proj_a, xa_ref, o_ref.at[0:T], *rest, y_a, nxt, after)
    _odd_mixers(False, T, proj_b, xb_ref, o_ref.at[T:2 * T], *rest, y_b, after)
    for b in range(2 * ATT_KV_HEADS):
        kvbuf[b, 0:BLOCK, :] = kvbuf[b, 2 * T:2 * T + BLOCK, :]


def _odd_layer(x, wa, wb, wg, i_bias, f_bias, sinks, wo, final_g):
    S = x.shape[0]
    T = min(TILE_ODD, S // 2)
    NT = S // T
    assert S % (2 * T) == 0 and T % BLOCK == 0 and T % min(MCHUNK, T) == 0
    assert i_bias.shape == f_bias.shape == (MLSTM_HEADS,) and sinks.shape == (ATT_HEADS,)
    const = _const_spec
    smem = pl.BlockSpec(memory_space=pltpu.SMEM)
    return pl.pallas_call(
        _odd_kernel,
        grid=(NT // 2,),
        in_specs=[
            smem, smem, smem,
            pl.BlockSpec((T, D_MODEL), lambda s: (2 * s, 0)),
            pl.BlockSpec((T, D_MODEL), lambda s: (2 * s + 1, 0)),
            pl.BlockSpec((T, D_MODEL), lambda s: (jnp.minimum(2 * s + 2, NT - 1), 0)),
            const(O_DO, D_MODEL),
            const(O_IF - O_DO, D_MODEL),
            const(GATE_PAD, D_MODEL),
            const(ODD_OUT, D_MODEL),
            const(1, D_MODEL),
        ],
        out_specs=pl.BlockSpec((2 * T, D_MODEL), lambda s: (s, 0)),
        out_shape=jax.ShapeDtypeStruct((S, D_MODEL), F32),
        scratch_shapes=[
            pltpu.VMEM((T, ODD_IN_PAD), F32),
            pltpu.VMEM((T, ODD_IN_PAD), F32),
            pltpu.VMEM((T, D_MODEL), BF16),
            pltpu.VMEM((T, D_MODEL), BF16),
            pltpu.VMEM((2 * ATT_KV_HEADS, 2 * T + BLOCK, LANES), BF16),
            pltpu.VMEM((ATT_HEADS * BLOCK, 2 * BLOCK), F32),
            pltpu.VMEM((2 * MLSTM_HEADS, T), F32),
            pltpu.VMEM((MLSTM_HEADS * MLSTM_QK_DIM, 2 * MLSTM_V_DIM), F32),
            pltpu.VMEM((2 * MLSTM_HEADS, LANES), F32),
            pltpu.VMEM((T, ODD_OUT), BF16),
            pltpu.VMEM((T, ODD_OUT), BF16),
        ],
        compiler_params=pltpu.CompilerParams(
            dimension_semantics=("arbitrary",), vmem_limit_bytes=VMEM_LIMIT),
        name="odd_layer",
    )(sinks.astype(F32), i_bias.astype(F32), f_bias.astype(F32), x, x, x, wa, wb, wg, wo,
      final_g.reshape(1, D_MODEL))


def kernel(x, even_norm, even_w_in, even_conv_w, even_pool_w, even_pool_scale, even_w_out,
           odd_norm, odd_w_in, odd_i_bias, odd_f_bias, odd_sinks, odd_w_out, final_norm):
    B, S, D = x.shape
    assert B == 1 and D == D_MODEL
    assert even_norm.shape[0] == 1 and odd_norm.shape[0] == 1
    h = x.reshape(S, D)
    h, wa, wb, wg, wo = _even_layer(h, even_norm[0], even_w_in[0], jnp.swapaxes(even_conv_w, 0, 1), even_pool_w[0],
                                    even_pool_scale[0], even_w_out[0], odd_norm[0], odd_w_in[0].T, odd_w_out[0])
    h = _odd_layer(h, wa, wb, wg, odd_i_bias[0], odd_f_bias[0], odd_sinks[0], wo, final_norm)
    return h.reshape(B, S, D)
```

```python
import jax
import jax.numpy as jnp
from jax import lax
from jax.experimental import pallas as pl
from jax.experimental.pallas import tpu as pltpu

F32 = jnp.float32
BF16 = jnp.bfloat16

D_MODEL = 1024
RMS_EPS = 1e-6

CONV_WIDTH = 512
CONV_K = 3
POOL_WINDOWS = (2, 4, 8, 16)
POOL_WIDTH = 512
POOL_GROUP = POOL_WIDTH // len(POOL_WINDOWS)
EVEN_IN = 4 * CONV_WIDTH + 2 * POOL_WIDTH
EVEN_OUT = CONV_WIDTH + POOL_WIDTH
CONV_HALO = 8
POOL_HALO = 16

ATT_HEADS = 8
ATT_KV_HEADS = 2
ATT_GROUP = ATT_HEADS // ATT_KV_HEADS
ATT_HEAD_DIM = 64
ATT_WIDTH = ATT_HEADS * ATT_HEAD_DIM
ATT_KV_WIDTH = ATT_KV_HEADS * ATT_HEAD_DIM
WINDOW = 128
BLOCK = 128
MLSTM_HEADS = 4
MLSTM_QK_DIM = 64
MLSTM_V_DIM = 128
MLSTM_QK_WIDTH = MLSTM_HEADS * MLSTM_QK_DIM
MLSTM_WIDTH = MLSTM_HEADS * MLSTM_V_DIM
ODD_OUT = ATT_WIDTH + MLSTM_WIDTH
LANES = 128
BF16_ROWS = 16
SINK_KEY = 0
MXU_WIDTH = 256
GATE_PAD = LANES

O_CQ = 0
O_CK = O_CQ + ATT_WIDTH
O_CV = O_CK + ATT_KV_WIDTH
O_GC = O_CV + ATT_KV_WIDTH
O_DQ = O_GC + ATT_WIDTH
O_DK = O_DQ + MLSTM_QK_WIDTH
O_DV = O_DK + MLSTM_QK_WIDTH
O_DO = O_DV + MLSTM_WIDTH
O_GD = O_DO + MLSTM_WIDTH
O_IF = O_GD + MLSTM_WIDTH
ODD_IN_PAD = O_IF + GATE_PAD

TILE_EVEN = 1024
EVEN_CHUNK = 512
TILE_ODD = 512
MCHUNK = 256
INPROJ_ROWS = 512
OUT_ROWS = 256
PRE_CHUNKS = 3
MLSTM_CHUNKS = 5
ATTN_SKIP_EVERY = 4
LOG2E = 1.4426950408889634
ATT_Q_SCALE = ATT_HEAD_DIM ** -0.5 * LOG2E
MLSTM_Q_SCALE = MLSTM_QK_DIM ** -0.5
VMEM_LIMIT = 60 * 1024 * 1024


def _const_spec(*shape):
    return pl.BlockSpec(shape, lambda i: (0,) * len(shape), pipeline_mode=pl.Buffered(1))


def _rms_scale(x):
    ms = jnp.mean(x * x, axis=-1, keepdims=True)
    return x * lax.rsqrt(ms + RMS_EPS)


def _rmsnorm(x, g):
    return _rms_scale(x) * g


def _even_kernel(x_ref, g_ref, win_ref, cw_ref, pw_ref, ps_ref, wout_ref,
                 ng_ref, nwa32_ref, nwb32_ref, nwg32_ref, nwo32_ref,
                 o_ref, nwa_ref, nwb_ref, nwg_ref, nwo_ref,
                 proj_scr, uc_scr, up_scr, y_scr, winb_scr, woutb_scr):
    i = pl.program_id(0)
    T = x_ref.shape[0]
    C = CONV_WIDTH

    ng = ng_ref[...]
    ra = nwa32_ref.shape[0]
    feat = lax.broadcasted_iota(jnp.int32, (ra, D_MODEL), 0) + i * ra
    q_scale = jnp.where(feat < O_CK, ATT_Q_SCALE,
                        jnp.where(feat < O_DQ, 1.0, jnp.where(feat < O_DK, MLSTM_Q_SCALE, 1.0)))
    nwa_ref[...] = (nwa32_ref[...] * ng * q_scale).astype(BF16)
    nwb_ref[...] = (nwb32_ref[...] * ng).astype(BF16)
    nwo_ref[...] = nwo32_ref[...].astype(BF16)
    nwg_ref[...] = jnp.concatenate(
        [nwg32_ref[...] * ng, jnp.zeros((GATE_PAD - 2 * MLSTM_HEADS, D_MODEL), F32)], axis=0).astype(BF16)

    @pl.when(i == 0)
    def _():
        uc_scr[0:CONV_HALO, :] = jnp.zeros((CONV_HALO, C), F32)
        up_scr[0:POOL_HALO, :] = jnp.zeros((POOL_HALO, POOL_WIDTH), F32)
        for g in range(len(POOL_WINDOWS)):
            lo, hi = g * POOL_GROUP, (g + 1) * POOL_GROUP
            winb_scr[:, 4 * C + lo:4 * C + hi] = jnp.dot(
                win_ref[:, 4 * C + lo:4 * C + hi], pw_ref[g],
                precision=lax.Precision.HIGHEST, preferred_element_type=F32).astype(BF16)
        for c0 in range(0, EVEN_IN, EVEN_CHUNK):
            if c0 != 4 * C:
                winb_scr[:, c0:c0 + EVEN_CHUNK] = win_ref[:, c0:c0 + EVEN_CHUNK].astype(BF16)
        woutb_scr[...] = wout_ref[...].astype(BF16)

    h = _rmsnorm(x_ref[...], g_ref[...]).astype(BF16)
    for c0 in range(0, EVEN_IN, EVEN_CHUNK):
        proj_scr[:, c0:c0 + EVEN_CHUNK] = jnp.dot(h, winb_scr[:, c0:c0 + EVEN_CHUNK], preferred_element_type=F32)

    uc = proj_scr[:, 2 * C:3 * C] * proj_scr[:, 0:C]
    uc_scr[CONV_HALO:CONV_HALO + T, :] = uc
    conv = (cw_ref[2] * uc
            + cw_ref[1] * uc_scr[CONV_HALO - 1:CONV_HALO - 1 + T, :]
            + cw_ref[0] * uc_scr[CONV_HALO - 2:CONV_HALO - 2 + T, :])
    ya = proj_scr[:, C:2 * C] * conv * jax.nn.silu(proj_scr[:, 3 * C:4 * C])
    y_scr[:, 0:C] = ya.astype(BF16)
    uc_scr[0:CONV_HALO, :] = uc_scr[T:T + CONV_HALO, :]

    up_scr[POOL_HALO:POOL_HALO + T, :] = proj_scr[:, 4 * C:4 * C + POOL_WIDTH]
    t1 = (lax.broadcasted_iota(jnp.int32, (T, POOL_GROUP), 0) + (i * T + 1)).astype(F32)
    for g, w in enumerate(POOL_WINDOWS):
        lo, hi = g * POOL_GROUP, (g + 1) * POOL_GROUP
        s = up_scr[:, lo:hi]
        k = 1
        while k < w:
            s = s + pltpu.roll(s, k, 0)
            k *= 2
        u = up_scr[POOL_HALO:POOL_HALO + T, lo:hi]
        yb = s[POOL_HALO:, :] / jnp.minimum(t1, float(w)) - u
        gate = proj_scr[:, 4 * C + POOL_WIDTH + lo:4 * C + POOL_WIDTH + hi]
        y_scr[:, C + lo:C + hi] = (yb * ps_ref[:, lo:hi] * jax.nn.silu(gate)).astype(BF16)
    up_scr[0:POOL_HALO, :] = up_scr[T:T + POOL_HALO, :]

    o_ref[...] = x_ref[...] + jnp.dot(y_scr[...], woutb_scr[...], preferred_element_type=F32)


def _even_layer(x, norm_g, w_in, conv_w, pool_w, pool_scale, w_out, next_norm_g, next_w_in_t, next_w_out):
    S = x.shape[0]
    T = min(TILE_EVEN, S)
    steps = S // T
    n_gates = 2 * MLSTM_HEADS
    assert S % T == 0
    assert next_w_in_t.shape == (O_IF + n_gates, D_MODEL) and next_w_out.shape == (ODD_OUT, D_MODEL)
    ra, rb, ro = O_DO // steps, (O_IF - O_DO) // steps, ODD_OUT // steps
    assert ra * steps == O_DO and rb * steps == O_IF - O_DO and ro * steps == ODD_OUT
    const = _const_spec
    slab = lambda r: pl.BlockSpec((r, D_MODEL), lambda i: (i, 0))
    return pl.pallas_call(
        _even_kernel,
        grid=(steps,),
        in_specs=[
            pl.BlockSpec((T, D_MODEL), lambda i: (i, 0)),
            const(1, D_MODEL),
            const(D_MODEL, EVEN_IN),
            const(CONV_K, 1, CONV_WIDTH),
            const(len(POOL_WINDOWS), POOL_GROUP, POOL_GROUP),
            const(1, POOL_WIDTH),
            const(EVEN_OUT, D_MODEL),
            const(1, D_MODEL),
            slab(ra),
            pl.BlockSpec((pl.Element(rb), pl.Element(D_MODEL)), lambda i: (pl.multiple_of(O_DO + n_gates + i * rb, n_gates), 0)),
            pl.BlockSpec((n_gates, D_MODEL), lambda i: (O_DO // n_gates, 0)),
            slab(ro),
        ],
        out_specs=[pl.BlockSpec((T, D_MODEL), lambda i: (i, 0)),
                   slab(ra), slab(rb), pl.BlockSpec((GATE_PAD, D_MODEL), lambda i: (0, 0)), slab(ro)],
        out_shape=[jax.ShapeDtypeStruct((S, D_MODEL), F32),
                   jax.ShapeDtypeStruct((O_DO, D_MODEL), BF16),
                   jax.ShapeDtypeStruct((O_IF - O_DO, D_MODEL), BF16),
                   jax.ShapeDtypeStruct((GATE_PAD, D_MODEL), BF16),
                   jax.ShapeDtypeStruct((ODD_OUT, D_MODEL), BF16)],
        scratch_shapes=[
            pltpu.VMEM((T, EVEN_IN), F32),
            pltpu.VMEM((T + CONV_HALO, CONV_WIDTH), F32),
            pltpu.VMEM((T + POOL_HALO, POOL_WIDTH), F32),
            pltpu.VMEM((T, EVEN_OUT), BF16),
            pltpu.VMEM((D_MODEL, EVEN_IN), BF16),
            pltpu.VMEM((EVEN_OUT, D_MODEL), BF16),
        ],
        compiler_params=pltpu.CompilerParams(
            dimension_semantics=("arbitrary",), vmem_limit_bytes=VMEM_LIMIT),
        name="even_layer",
    )(x, norm_g.reshape(1, D_MODEL), w_in, conv_w, pool_w, pool_scale.reshape(1, POOL_WIDTH), w_out,
      next_norm_g.reshape(1, D_MODEL), next_w_in_t, next_w_in_t, next_w_in_t, next_w_out)


def _lane_scan(x, op, identity):
    n = x.shape[1]
    lane = lax.broadcasted_iota(jnp.int32, x.shape, 1)
    k = 1
    while k < n:
        x = op(x, jnp.where(lane >= k, pltpu.roll(x, k, 1), identity))
        k *= 2
    return x


def _odd_init(sink_ref, ib_ref, fb_ref, kvbuf, abias, bias_scr, c_scr, m_scr):
    for b in range(2 * ATT_KV_HEADS):
        kvbuf[b, 0:BLOCK, :] = jnp.zeros((BLOCK, LANES), BF16)
    c_scr[...] = jnp.zeros(c_scr.shape, F32)
    m_scr[...] = jnp.zeros(m_scr.shape, F32)
    qi = lax.broadcasted_iota(jnp.int32, (BLOCK, 2 * BLOCK), 0)
    kj = lax.broadcasted_iota(jnp.int32, (BLOCK, 2 * BLOCK), 1)
    dist = qi + BLOCK - kj
    ok = (dist >= 0) & (dist < WINDOW)
    distf = dist.astype(F32)
    for hd in range(ATT_HEADS):
        slope = 2.0 ** (-8.0 * (hd + 1) / ATT_HEADS)
        band = jnp.where(ok, (-slope * LOG2E) * distf, -jnp.inf)
        abias[hd * BLOCK:(hd + 1) * BLOCK, :] = jnp.where(kj == SINK_KEY, sink_ref[hd] * LOG2E, band)
    row = lax.broadcasted_iota(jnp.int32, bias_scr.shape, 0)
    bias = jnp.zeros(bias_scr.shape, F32)
    for hd in range(MLSTM_HEADS):
        bias = jnp.where(row == hd, ib_ref[hd], jnp.where(row == MLSTM_HEADS + hd, fb_ref[hd], bias))
    bias_scr[...] = bias


class _InProj:
    def __init__(self, x_ref, wa_ref, wb_ref, wg_ref, h_scr, proj):
        self.x_ref, self.h_scr, self.proj = x_ref, h_scr, proj
        self.todo = []
        rows = x_ref.shape[0]
        for w_ref, base in ((wa_ref, 0), (wb_ref, O_DO), (wg_ref, O_IF)):
            n = w_ref.shape[0]
            for c0 in range(0, n, MXU_WIDTH):
                for r0 in range(0, rows, INPROJ_ROWS):
                    self.todo.append((w_ref, base, c0, min(c0 + MXU_WIDTH, n), r0, min(r0 + INPROJ_ROWS, rows)))

    def normalise(self):
        self.h_scr[...] = _rms_scale(self.x_ref[...]).astype(BF16)

    def step(self):
        if self.todo:
            w_ref, base, c0, c1, r0, r1 = self.todo.pop(0)
            self.proj[r0:r1, base + c0:base + c1] = lax.dot_general(
                self.h_scr[r0:r1, :], w_ref[c0:c1, :], (((1,), (1,)), ((), ())), preferred_element_type=F32)

    def finish(self):
        while self.todo:
            self.step()


def _odd_mixers(first, kv0, proj, x_ref, o_ref, bias_ref, wout_ref, fg_ref,
                kvbuf, abias, c_scr, m_scr, y_scr, nxt, after=None):
    T = x_ref.shape[0]
    NB = T // BLOCK
    H = MLSTM_HEADS
    L = min(MCHUNK, T)
    NEG = -jnp.inf

    lane_q = lax.broadcasted_iota(jnp.int32, (BLOCK, LANES), 1)
    low = lane_q < ATT_HEAD_DIM
    low_t = lax.broadcasted_iota(jnp.int32, (T, LANES), 1) < ATT_HEAD_DIM
    for b, off in enumerate((O_CK, O_CV)):
        kv = proj[:, off:off + ATT_KV_WIDTH]
        sw = pltpu.roll(kv, ATT_HEAD_DIM, 1)
        kvbuf[2 * b, kv0 + BLOCK:kv0 + BLOCK + T, :] = jnp.where(low_t, kv, sw).astype(BF16)
        kvbuf[2 * b + 1, kv0 + BLOCK:kv0 + BLOCK + T, :] = jnp.where(low_t, sw, kv).astype(BF16)
    if first is not False:
        key_lane = lax.broadcasted_iota(jnp.int32, (BLOCK, 2 * BLOCK), 1)
        no_prev = jnp.where((key_lane < BLOCK) & (key_lane != SINK_KEY), jnp.where(first, NEG, 0.0), 0.0)
    sink_row = lax.broadcasted_iota(jnp.int32, (BF16_ROWS, LANES), 0) == SINK_KEY

    def drop_sink_row(a):
        head = a[0:BF16_ROWS, :]
        return jnp.concatenate([jnp.where(sink_row, jnp.zeros_like(head), head), a[BF16_ROWS:, :]], axis=0)

    ones_v = jnp.ones((2 * BLOCK, LANES), BF16)

    def scores(j, g):
        r0 = j * BLOCK
        kk = drop_sink_row(kvbuf[g, kv0 + r0:kv0 + r0 + 2 * BLOCK, :])
        qs = []
        for p in range(g * ATT_GROUP // 2, (g + 1) * ATT_GROUP // 2):
            qp = proj[r0:r0 + BLOCK, O_CQ + p * LANES:O_CQ + (p + 1) * LANES]
            qs.append(jnp.where(low, qp, 0.0).astype(BF16))
            qs.append(jnp.where(low, 0.0, qp).astype(BF16))
        return lax.dot_general(jnp.concatenate(qs, axis=0), kk, (((1,), (1,)), ((), ())),
                               preferred_element_type=F32)

    def attention():
        slots = [(j, g) for j in range(NB) for g in range(ATT_KV_HEADS)]
        s_next = scores(*slots[0])
        for n, (j, g) in enumerate(slots):
            r0 = j * BLOCK
            s_all = s_next
            if n + 1 < len(slots):
                s_next = scores(*slots[n + 1])
            if n % ATTN_SKIP_EVERY != ATTN_SKIP_EVERY - 1:
                nxt.step()
            ps = []
            for r in range(ATT_GROUP):
                hd = g * ATT_GROUP + r
                s = s_all[r * BLOCK:(r + 1) * BLOCK, :] + abias[hd * BLOCK:(hd + 1) * BLOCK, :]
                if j == 0 and first is not False:
                    s = s + no_prev
                ps.append(jnp.exp2(s - jnp.max(s, axis=1, keepdims=True)).astype(BF16))
            vv = jnp.concatenate(
                [drop_sink_row(kvbuf[ATT_KV_HEADS + g, kv0 + r0:kv0 + r0 + 2 * BLOCK, :]), ones_v], axis=1)
            o_all = jnp.dot(jnp.concatenate(ps, axis=0), vv, preferred_element_type=F32)
            for pp in range(ATT_GROUP // 2):
                p = g * ATT_GROUP // 2 + pp
                lo = o_all[2 * pp * BLOCK:(2 * pp + 1) * BLOCK, :]
                hi = o_all[(2 * pp + 1) * BLOCK:(2 * pp + 2) * BLOCK, :]
                att = jnp.where(low, lo[:, :LANES] / lo[:, LANES:], hi[:, :LANES] / hi[:, LANES:])
                gate = proj[r0:r0 + BLOCK, O_GC + p * LANES:O_GC + (p + 1) * LANES]
                y_scr[r0:r0 + BLOCK, p * LANES:(p + 1) * LANES] = (att * jax.nn.silu(gate)).astype(BF16)

    gt = proj[:, O_IF:O_IF + GATE_PAD].T[0:2 * H, :]
    pre = gt + bias_ref[...]
    row = lax.broadcasted_iota(jnp.int32, (2 * H, T), 0)
    top = row < H
    bcum = _lane_scan(jax.nn.log_sigmoid(pre), jnp.add, 0.0)
    bsw = pltpu.roll(bcum, H, 0)
    a_top = pre - bsw
    ag = jnp.where(top, a_top, pltpu.roll(a_top, H, 0))
    bf = jnp.where(top, bsw, bcum)
    m_in = jnp.concatenate([m_scr[...]] * (T // LANES), axis=1)
    mg = jnp.maximum(m_in, _lane_scan(ag, jnp.maximum, NEG))
    m_t = bf + mg
    nrm = jnp.exp(-m_t)
    inter_parts, wk_parts, decay_parts = [], [], []
    for c in range(T // L):
        m_start = m_in[:, 0:1] if c == 0 else mg[:, c * L - 1:c * L]
        m_last = mg[:, (c + 1) * L - 1:(c + 1) * L]
        inter_parts.append(jnp.exp(m_start - mg[:, c * L:(c + 1) * L]))
        wk_parts.append(jnp.exp(ag[:, c * L:(c + 1) * L] - m_last))
        decay_parts.append(jnp.broadcast_to(jnp.exp(m_start - m_last), (2 * H, 2 * MLSTM_V_DIM)))
    inter = jnp.concatenate(inter_parts, axis=1)
    m_scr[...] = jnp.broadcast_to(m_t[:, T - 1:T], m_scr.shape)
    ag2 = ag * LOG2E
    cols = jnp.concatenate([mg * LOG2E, inter, nrm, jnp.zeros((LANES - 6 * H, T), F32)], axis=0).T

    kT = proj[:, O_DK:O_DK + MLSTM_QK_WIDTH].T
    lane_p = lax.broadcasted_iota(jnp.int32, (L, LANES), 1)
    low_p = lane_p < MLSTM_QK_DIM
    tri = (lax.broadcasted_iota(jnp.int32, (BLOCK, BLOCK), 1) <= lax.broadcasted_iota(jnp.int32, (BLOCK, BLOCK), 0))
    ones = jnp.ones((L, MLSTM_V_DIM), BF16)
    zero_blk = jnp.zeros((BLOCK, BLOCK), BF16)

    def mlstm_chunk(c):
        r0 = c * L
        for pair in range(H // 2):
            qp = proj[r0:r0 + L, O_DQ + pair * LANES:O_DQ + (pair + 1) * LANES]
            kT_pair = kT[pair * LANES:(pair + 1) * LANES, r0:r0 + L].astype(BF16)
            c_pair = c_scr[pair * LANES:(pair + 1) * LANES, :].astype(BF16)
            for half in range(2):
                hd = 2 * pair + half
                q_h = jnp.where(low_p, qp, 0.0) if half == 0 else jnp.where(low_p, 0.0, qp)
                s = jnp.dot(q_h.astype(BF16), kT_pair, preferred_element_type=F32)
                if H * c + hd < MLSTM_CHUNKS:
                    nxt.step()
                p_rows = []
                for bi in range(L // BLOCK):
                    m_col = cols[r0 + bi * BLOCK:r0 + (bi + 1) * BLOCK, hd:hd + 1]
                    blks = []
                    for bj in range(L // BLOCK):
                        if bj > bi:
                            blks.append(zero_blk)
                            continue
                        arg = ag2[hd:hd + 1, r0 + bj * BLOCK:r0 + (bj + 1) * BLOCK] - m_col
                        if bj == bi:
                            arg = jnp.where(tri, arg, NEG)
                        sb = s[bi * BLOCK:(bi + 1) * BLOCK, bj * BLOCK:(bj + 1) * BLOCK]
                        blks.append((sb * jnp.exp2(arg)).astype(BF16))
                    p_rows.append(jnp.concatenate(blks, axis=1))
                p_bf = jnp.concatenate(p_rows, axis=0)
                v_aug = jnp.concatenate(
                    [proj[r0:r0 + L, O_DV + hd * MLSTM_V_DIM:O_DV + (hd + 1) * MLSTM_V_DIM].astype(BF16), ones],
                    axis=1)
                q_int = (q_h * cols[r0:r0 + L, 2 * H + hd:2 * H + hd + 1]).astype(BF16)
                num = jnp.dot(jnp.concatenate([p_bf, q_int], axis=1),
                              jnp.concatenate([v_aug, c_pair], axis=0), preferred_element_type=F32)
                den = jnp.maximum(jnp.abs(num[:, MLSTM_V_DIM:]), cols[r0:r0 + L, 4 * H + hd:4 * H + hd + 1])
                hm = num[:, :MLSTM_V_DIM] / den
                og = proj[r0:r0 + L, O_DO + hd * MLSTM_V_DIM:O_DO + (hd + 1) * MLSTM_V_DIM]
                gd = proj[r0:r0 + L, O_GD + hd * MLSTM_V_DIM:O_GD + (hd + 1) * MLSTM_V_DIM]
                y_scr[r0:r0 + L, ATT_WIDTH + hd * MLSTM_V_DIM:ATT_WIDTH + (hd + 1) * MLSTM_V_DIM] = (
                    hm * jax.nn.sigmoid(og) * jax.nn.silu(gd)).astype(BF16)
                kw = (kT[hd * MLSTM_QK_DIM:(hd + 1) * MLSTM_QK_DIM, r0:r0 + L]
                      * wk_parts[c][hd:hd + 1, :]).astype(BF16)
                upd = jnp.dot(kw, v_aug, preferred_element_type=F32)
                rows = slice(hd * MLSTM_QK_DIM, (hd + 1) * MLSTM_QK_DIM)
                c_scr[rows, :] = decay_parts[c][hd:hd + 1, :] * c_scr[rows, :] + upd

    attention()
    for _ in range(PRE_CHUNKS):
        nxt.step()
    for c in range(T // L):
        mlstm_chunk(c)
    nxt.finish()
    if after is not None:
        after.normalise()
    for r0 in range(0, T, OUT_ROWS):
        x2 = x_ref[r0:r0 + OUT_ROWS, :] + jnp.dot(y_scr[r0:r0 + OUT_ROWS, :], wout_ref[...],
                                                   preferred_element_type=F32)
        o_ref[r0:r0 + OUT_ROWS, :] = _rmsnorm(x2, fg_ref[...])


def _odd_kernel(sink_ref, ib_ref, fb_ref, xa_ref, xb_ref, xn_ref, wa_ref, wb_ref, wg_ref, wout_ref,
                fg_ref, o_ref, proj_a, proj_b, h_a, h_b, kvbuf, abias, bias_scr, c_scr, m_scr, y_a, y_b):
    s = pl.program_id(0)
    T = xa_ref.shape[0]
    w = (wa_ref, wb_ref, wg_ref)
    rest = (bias_scr, wout_ref, fg_ref, kvbuf, abias, c_scr, m_scr)

    @pl.when(s == 0)
    def _():
        _odd_init(sink_ref, ib_ref, fb_ref, kvbuf, abias, bias_scr, c_scr, m_scr)
        first = _InProj(xa_ref, *w, h_a, proj_a)
        first.normalise()
        first.finish()

    nxt = _InProj(xb_ref, *w, h_a, proj_b)
    nxt.normalise()
    after = _InProj(xn_ref, *w, h_b, proj_a)
    _odd_mixers(s == 0, 0, proj_a, xa_ref, o_ref.at[0:T], *rest, y_a, nxt, after)
    _odd_mixers(False, T, proj_b, xb_ref, o_ref.at[T:2 * T], *rest, y_b, after)
    for b in range(2 * ATT_KV_HEADS):
        kvbuf[b, 0:BLOCK, :] = kvbuf[b, 2 * T:2 * T + BLOCK, :]


def _odd_layer(x, wa, wb, wg, i_bias, f_bias, sinks, wo, final_g):
    S = x.shape[0]
    T = min(TILE_ODD, S // 2)
    NT = S // T
    assert S % (2 * T) == 0 and T % BLOCK == 0 and T % min(MCHUNK, T) == 0
    assert i_bias.shape == f_bias.shape == (MLSTM_HEADS,) and sinks.shape == (ATT_HEADS,)
    const = _const_spec
    smem = pl.BlockSpec(memory_space=pltpu.SMEM)
    return pl.pallas_call(
        _odd_kernel,
        grid=(NT // 2,),
        in_specs=[
            smem, smem, smem,
            pl.BlockSpec((T, D_MODEL), lambda s: (2 * s, 0)),
            pl.BlockSpec((T, D_MODEL), lambda s: (2 * s + 1, 0)),
            pl.BlockSpec((T, D_MODEL), lambda s: (jnp.minimum(2 * s + 2, NT - 1), 0)),
            const(O_DO, D_MODEL),
            const(O_IF - O_DO, D_MODEL),
            const(GATE_PAD, D_MODEL),
            const(ODD_OUT, D_MODEL),
            const(1, D_MODEL),
        ],
        out_specs=pl.BlockSpec((2 * T, D_MODEL), lambda s: (s, 0)),
        out_shape=jax.ShapeDtypeStruct((S, D_MODEL), F32),
        scratch_shapes=[
            pltpu.VMEM((T, ODD_IN_PAD), F32),
            pltpu.VMEM((T, ODD_IN_PAD), F32),
            pltpu.VMEM((T, D_MODEL), BF16),
            pltpu.VMEM((T, D_MODEL), BF16),
            pltpu.VMEM((2 * ATT_KV_HEADS, 2 * T + BLOCK, LANES), BF16),
            pltpu.VMEM((ATT_HEADS * BLOCK, 2 * BLOCK), F32),
            pltpu.VMEM((2 * MLSTM_HEADS, T), F32),
            pltpu.VMEM((MLSTM_HEADS * MLSTM_QK_DIM, 2 * MLSTM_V_DIM), F32),
            pltpu.VMEM((2 * MLSTM_HEADS, LANES), F32),
            pltpu.VMEM((T, ODD_OUT), BF16),
            pltpu.VMEM((T, ODD_OUT), BF16),
        ],
        compiler_params=pltpu.CompilerParams(
            dimension_semantics=("arbitrary",), vmem_limit_bytes=VMEM_LIMIT),
        name="odd_layer",
    )(sinks.astype(F32), i_bias.astype(F32), f_bias.astype(F32), x, x, x, wa, wb, wg, wo,
      final_g.reshape(1, D_MODEL))


def kernel(x, even_norm, even_w_in, even_conv_w, even_pool_w, even_pool_scale, even_w_out,
           odd_norm, odd_w_in, odd_i_bias, odd_f_bias, odd_sinks, odd_w_out, final_norm):
    B, S, D = x.shape
    assert B == 1 and D == D_MODEL
    assert even_norm.shape[0] == 1 and odd_norm.shape[0] == 1
    h = x.reshape(S, D)
    h, wa, wb, wg, wo = _even_layer(h, even_norm[0], even_w_in[0], jnp.swapaxes(even_conv_w, 0, 1), even_pool_w[0],
                                    even_pool_scale[0], even_w_out[0], odd_norm[0], odd_w_in[0].T, odd_w_out[0])
    h = _odd_layer(h, wa, wb, wg, odd_i_bias[0], odd_f_bias[0], odd_sinks[0], wo, final_norm)
    return h.reshape(B, S, D)
```

```python
import jax
import jax.numpy as jnp
from jax import lax
from jax.experimental import pallas as pl
from jax.experimental.pallas import tpu as pltpu

F32 = jnp.float32
BF16 = jnp.bfloat16

D_MODEL = 1024
RMS_EPS = 1e-6

CONV_WIDTH = 512
CONV_K = 3
POOL_WINDOWS = (2, 4, 8, 16)
POOL_WIDTH = 512
POOL_GROUP = POOL_WIDTH // len(POOL_WINDOWS)
EVEN_IN = 4 * CONV_WIDTH + 2 * POOL_WIDTH
EVEN_OUT = CONV_WIDTH + POOL_WIDTH
CONV_HALO = 8
POOL_HALO = 16

ATT_HEADS = 8
ATT_KV_HEADS = 2
ATT_GROUP = ATT_HEADS // ATT_KV_HEADS
ATT_HEAD_DIM = 64
ATT_WIDTH = ATT_HEADS * ATT_HEAD_DIM
ATT_KV_WIDTH = ATT_KV_HEADS * ATT_HEAD_DIM
WINDOW = 128
BLOCK = 128
MLSTM_HEADS = 4
MLSTM_QK_DIM = 64
MLSTM_V_DIM = 128
MLSTM_QK_WIDTH = MLSTM_HEADS * MLSTM_QK_DIM
MLSTM_WIDTH = MLSTM_HEADS * MLSTM_V_DIM
ODD_OUT = ATT_WIDTH + MLSTM_WIDTH
LANES = 128
BF16_ROWS = 16
SINK_KEY = 0
MXU_WIDTH = 256
GATE_PAD = LANES

O_CQ = 0
O_CK = O_CQ + ATT_WIDTH
O_CV = O_CK + ATT_KV_WIDTH
O_GC = O_CV + ATT_KV_WIDTH
O_DQ = O_GC + ATT_WIDTH
O_DK = O_DQ + MLSTM_QK_WIDTH
O_DV = O_DK + MLSTM_QK_WIDTH
O_DO = O_DV + MLSTM_WIDTH
O_GD = O_DO + MLSTM_WIDTH
O_IF = O_GD + MLSTM_WIDTH
ODD_IN_PAD = O_IF + GATE_PAD

TILE_EVEN = 1024
EVEN_CHUNK = 512
TILE_ODD = 512
MCHUNK = 256
INPROJ_ROWS = 512
OUT_ROWS = 256
PRE_CHUNKS = 3
MLSTM_CHUNKS = 5
ATTN_SKIP_EVERY = 4
LOG2E = 1.4426950408889634
ATT_Q_SCALE = ATT_HEAD_DIM ** -0.5 * LOG2E
MLSTM_Q_SCALE = MLSTM_QK_DIM ** -0.5
VMEM_LIMIT = 60 * 1024 * 1024


def _const_spec(*shape):
    return pl.BlockSpec(shape, lambda i: (0,) * len(shape), pipeline_mode=pl.Buffered(1))


def _rms_scale(x):
    ms = jnp.mean(x * x, axis=-1, keepdims=True)
    return x * lax.rsqrt(ms + RMS_EPS)


def _rmsnorm(x, g):
    return _rms_scale(x) * g


def _even_kernel(x_ref, g_ref, win_ref, cw_ref, pw_ref, ps_ref, wout_ref,
                 ng_ref, nwa32_ref, nwb32_ref, nwg32_ref, nwo32_ref,
                 o_ref, nwa_ref, nwb_ref, nwg_ref, nwo_ref, proj_scr, uc_scr, up_scr, y_scr, wfold_scr):
    i = pl.program_id(0)
    T = x_ref.shape[0]
    C = CONV_WIDTH

    ng = ng_ref[...]
    ra = nwa32_ref.shape[0]
    feat = lax.broadcasted_iota(jnp.int32, (ra, D_MODEL), 0) + i * ra
    q_scale = jnp.where(feat < O_CK, ATT_Q_SCALE,
                        jnp.where(feat < O_DQ, 1.0, jnp.where(feat < O_DK, MLSTM_Q_SCALE, 1.0)))
    nwa_ref[...] = (nwa32_ref[...] * ng * q_scale).astype(BF16)
    nwb_ref[...] = (nwb32_ref[...] * ng).astype(BF16)
    nwo_ref[...] = nwo32_ref[...].astype(BF16)
    nwg_ref[...] = jnp.concatenate(
        [nwg32_ref[...] * ng, jnp.zeros((GATE_PAD - 2 * MLSTM_HEADS, D_MODEL), F32)], axis=0).astype(BF16)

    @pl.when(i == 0)
    def _():
        uc_scr[0:CONV_HALO, :] = jnp.zeros((CONV_HALO, C), F32)
        up_scr[0:POOL_HALO, :] = jnp.zeros((POOL_HALO, POOL_WIDTH), F32)
        for g in range(len(POOL_WINDOWS)):
            lo, hi = g * POOL_GROUP, (g + 1) * POOL_GROUP
            wfold_scr[:, lo:hi] = jnp.dot(win_ref[:, 4 * C + lo:4 * C + hi], pw_ref[g],
                                          precision=lax.Precision.HIGHEST, preferred_element_type=F32)

    h = _rmsnorm(x_ref[...], g_ref[...]).astype(BF16)
    for c0 in range(0, EVEN_IN, EVEN_CHUNK):
        w = wfold_scr[...] if c0 == 4 * C else win_ref[:, c0:c0 + EVEN_CHUNK]
        proj_scr[:, c0:c0 + EVEN_CHUNK] = jnp.dot(h, w.astype(BF16), preferred_element_type=F32)

    uc = proj_scr[:, 2 * C:3 * C] * proj_scr[:, 0:C]
    uc_scr[CONV_HALO:CONV_HALO + T, :] = uc
    conv = (cw_ref[2] * uc
            + cw_ref[1] * uc_scr[CONV_HALO - 1:CONV_HALO - 1 + T, :]
            + cw_ref[0] * uc_scr[CONV_HALO - 2:CONV_HALO - 2 + T, :])
    ya = proj_scr[:, C:2 * C] * conv * jax.nn.silu(proj_scr[:, 3 * C:4 * C])
    y_scr[:, 0:C] = ya.astype(BF16)
    uc_scr[0:CONV_HALO, :] = uc_scr[T:T + CONV_HALO, :]

    up_scr[POOL_HALO:POOL_HALO + T, :] = proj_scr[:, 4 * C:4 * C + POOL_WIDTH]
    t1 = (lax.broadcasted_iota(jnp.int32, (T, POOL_GROUP), 0) + (i * T + 1)).astype(F32)
    for g, w in enumerate(POOL_WINDOWS):
        lo, hi = g * POOL_GROUP, (g + 1) * POOL_GROUP
        s = up_scr[:, lo:hi]
        k = 1
        while k < w:
            s = s + pltpu.roll(s, k, 0)
            k *= 2
        u = up_scr[POOL_HALO:POOL_HALO + T, lo:hi]
        yb = s[POOL_HALO:, :] / jnp.minimum(t1, float(w)) - u
        gate = proj_scr[:, 4 * C + POOL_WIDTH + lo:4 * C + POOL_WIDTH + hi]
        y_scr[:, C + lo:C + hi] = (yb * ps_ref[:, lo:hi] * jax.nn.silu(gate)).astype(BF16)
    up_scr[0:POOL_HALO, :] = up_scr[T:T + POOL_HALO, :]

    o_ref[...] = x_ref[...] + jnp.dot(y_scr[...], wout_ref[...].astype(BF16), preferred_element_type=F32)


def _even_layer(x, norm_g, w_in, conv_w, pool_w, pool_scale, w_out, next_norm_g, next_w_in_t, next_w_out):
    S = x.shape[0]
    T = min(TILE_EVEN, S)
    steps = S // T
    n_gates = 2 * MLSTM_HEADS
    assert S % T == 0
    assert next_w_in_t.shape == (O_IF + n_gates, D_MODEL) and next_w_out.shape == (ODD_OUT, D_MODEL)
    ra, rb, ro = O_DO // steps, (O_IF - O_DO) // steps, ODD_OUT // steps
    assert ra * steps == O_DO and rb * steps == O_IF - O_DO and ro * steps == ODD_OUT
    const = _const_spec
    slab = lambda r: pl.BlockSpec((r, D_MODEL), lambda i: (i, 0))
    return pl.pallas_call(
        _even_kernel,
        grid=(steps,),
        in_specs=[
            pl.BlockSpec((T, D_MODEL), lambda i: (i, 0)),
            const(1, D_MODEL),
            const(D_MODEL, EVEN_IN),
            const(CONV_K, 1, CONV_WIDTH),
            const(len(POOL_WINDOWS), POOL_GROUP, POOL_GROUP),
            const(1, POOL_WIDTH),
            const(EVEN_OUT, D_MODEL),
            const(1, D_MODEL),
            slab(ra),
            pl.BlockSpec((pl.Element(rb), pl.Element(D_MODEL)), lambda i: (pl.multiple_of(O_DO + n_gates + i * rb, n_gates), 0)),
            pl.BlockSpec((n_gates, D_MODEL), lambda i: (O_DO // n_gates, 0)),
            slab(ro),
        ],
        out_specs=[pl.BlockSpec((T, D_MODEL), lambda i: (i, 0)),
                   slab(ra), slab(rb), pl.BlockSpec((GATE_PAD, D_MODEL), lambda i: (0, 0)), slab(ro)],
        out_shape=[jax.ShapeDtypeStruct((S, D_MODEL), F32),
                   jax.ShapeDtypeStruct((O_DO, D_MODEL), BF16),
                   jax.ShapeDtypeStruct((O_IF - O_DO, D_MODEL), BF16),
                   jax.ShapeDtypeStruct((GATE_PAD, D_MODEL), BF16),
                   jax.ShapeDtypeStruct((ODD_OUT, D_MODEL), BF16)],
        scratch_shapes=[
            pltpu.VMEM((T, EVEN_IN), F32),
            pltpu.VMEM((T + CONV_HALO, CONV_WIDTH), F32),
            pltpu.VMEM((T + POOL_HALO, POOL_WIDTH), F32),
            pltpu.VMEM((T, EVEN_OUT), BF16),
            pltpu.VMEM((D_MODEL, POOL_WIDTH), F32),
        ],
        compiler_params=pltpu.CompilerParams(
            dimension_semantics=("arbitrary",), vmem_limit_bytes=VMEM_LIMIT),
        name="even_layer",
    )(x, norm_g.reshape(1, D_MODEL), w_in, conv_w, pool_w, pool_scale.reshape(1, POOL_WIDTH), w_out,
      next_norm_g.reshape(1, D_MODEL), next_w_in_t, next_w_in_t, next_w_in_t, next_w_out)


def _lane_scan(x, op, identity):
    n = x.shape[1]
    lane = lax.broadcasted_iota(jnp.int32, x.shape, 1)
    k = 1
    while k < n:
        x = op(x, jnp.where(lane >= k, pltpu.roll(x, k, 1), identity))
        k *= 2
    return x


def _odd_init(sink_ref, ib_ref, fb_ref, kvbuf, abias, bias_scr, c_scr, m_scr):
    for b in range(2 * ATT_KV_HEADS):
        kvbuf[b, 0:BLOCK, :] = jnp.zeros((BLOCK, LANES), BF16)
    c_scr[...] = jnp.zeros(c_scr.shape, F32)
    m_scr[...] = jnp.zeros(m_scr.shape, F32)
    qi = lax.broadcasted_iota(jnp.int32, (BLOCK, 2 * BLOCK), 0)
    kj = lax.broadcasted_iota(jnp.int32, (BLOCK, 2 * BLOCK), 1)
    dist = qi + BLOCK - kj
    ok = (dist >= 0) & (dist < WINDOW)
    distf = dist.astype(F32)
    for hd in range(ATT_HEADS):
        slope = 2.0 ** (-8.0 * (hd + 1) / ATT_HEADS)
        band = jnp.where(ok, (-slope * LOG2E) * distf, -jnp.inf)
        abias[hd * BLOCK:(hd + 1) * BLOCK, :] = jnp.where(kj == SINK_KEY, sink_ref[hd] * LOG2E, band)
    row = lax.broadcasted_iota(jnp.int32, bias_scr.shape, 0)
    bias = jnp.zeros(bias_scr.shape, F32)
    for hd in range(MLSTM_HEADS):
        bias = jnp.where(row == hd, ib_ref[hd], jnp.where(row == MLSTM_HEADS + hd, fb_ref[hd], bias))
    bias_scr[...] = bias


class _InProj:
    def __init__(self, x_ref, wa_ref, wb_ref, wg_ref, h_scr, proj):
        self.x_ref, self.h_scr, self.proj = x_ref, h_scr, proj
        self.todo = []
        rows = x_ref.shape[0]
        for w_ref, base in ((wa_ref, 0), (wb_ref, O_DO), (wg_ref, O_IF)):
            n = w_ref.shape[0]
            for c0 in range(0, n, MXU_WIDTH):
                for r0 in range(0, rows, INPROJ_ROWS):
                    self.todo.append((w_ref, base, c0, min(c0 + MXU_WIDTH, n), r0, min(r0 + INPROJ_ROWS, rows)))

    def normalise(self):
        self.h_scr[...] = _rms_scale(self.x_ref[...]).astype(BF16)

    def step(self):
        if self.todo:
            w_ref, base, c0, c1, r0, r1 = self.todo.pop(0)
            self.proj[r0:r1, base + c0:base + c1] = lax.dot_general(
                self.h_scr[r0:r1, :], w_ref[c0:c1, :], (((1,), (1,)), ((), ())), preferred_element_type=F32)

    def finish(self):
        while self.todo:
            self.step()


def _odd_mixers(first, kv0, proj, x_ref, o_ref, bias_ref, wout_ref, fg_ref,
                kvbuf, abias, c_scr, m_scr, y_scr, nxt, after=None):
    T = x_ref.shape[0]
    NB = T // BLOCK
    H = MLSTM_HEADS
    L = min(MCHUNK, T)
    NEG = -jnp.inf

    lane_q = lax.broadcasted_iota(jnp.int32, (BLOCK, LANES), 1)
    low = lane_q < ATT_HEAD_DIM
    low_t = lax.broadcasted_iota(jnp.int32, (T, LANES), 1) < ATT_HEAD_DIM
    for b, off in enumerate((O_CK, O_CV)):
        kv = proj[:, off:off + ATT_KV_WIDTH]
        sw = pltpu.roll(kv, ATT_HEAD_DIM, 1)
        kvbuf[2 * b, kv0 + BLOCK:kv0 + BLOCK + T, :] = jnp.where(low_t, kv, sw).astype(BF16)
        kvbuf[2 * b + 1, kv0 + BLOCK:kv0 + BLOCK + T, :] = jnp.where(low_t, sw, kv).astype(BF16)
    if first is not False:
        key_lane = lax.broadcasted_iota(jnp.int32, (BLOCK, 2 * BLOCK), 1)
        no_prev = jnp.where((key_lane < BLOCK) & (key_lane != SINK_KEY), jnp.where(first, NEG, 0.0), 0.0)
    sink_row = lax.broadcasted_iota(jnp.int32, (BF16_ROWS, LANES), 0) == SINK_KEY

    def drop_sink_row(a):
        head = a[0:BF16_ROWS, :]
        return jnp.concatenate([jnp.where(sink_row, jnp.zeros_like(head), head), a[BF16_ROWS:, :]], axis=0)

    ones_v = jnp.ones((2 * BLOCK, LANES), BF16)

    def scores(j, g):
        r0 = j * BLOCK
        kk = drop_sink_row(kvbuf[g, kv0 + r0:kv0 + r0 + 2 * BLOCK, :])
        qs = []
        for p in range(g * ATT_GROUP // 2, (g + 1) * ATT_GROUP // 2):
            qp = proj[r0:r0 + BLOCK, O_CQ + p * LANES:O_CQ + (p + 1) * LANES]
            qs.append(jnp.where(low, qp, 0.0).astype(BF16))
            qs.append(jnp.where(low, 0.0, qp).astype(BF16))
        return lax.dot_general(jnp.concatenate(qs, axis=0), kk, (((1,), (1,)), ((), ())),
                               preferred_element_type=F32)

    def attention():
        slots = [(j, g) for j in range(NB) for g in range(ATT_KV_HEADS)]
        s_next = scores(*slots[0])
        for n, (j, g) in enumerate(slots):
            r0 = j * BLOCK
            s_all = s_next
            if n + 1 < len(slots):
                s_next = scores(*slots[n + 1])
            if n % ATTN_SKIP_EVERY != ATTN_SKIP_EVERY - 1:
                nxt.step()
            ps = []
            for r in range(ATT_GROUP):
                hd = g * ATT_GROUP + r
                s = s_all[r * BLOCK:(r + 1) * BLOCK, :] + abias[hd * BLOCK:(hd + 1) * BLOCK, :]
                if j == 0 and first is not False:
                    s = s + no_prev
                ps.append(jnp.exp2(s - jnp.max(s, axis=1, keepdims=True)).astype(BF16))
            vv = jnp.concatenate(
                [drop_sink_row(kvbuf[ATT_KV_HEADS + g, kv0 + r0:kv0 + r0 + 2 * BLOCK, :]), ones_v], axis=1)
            o_all = jnp.dot(jnp.concatenate(ps, axis=0), vv, preferred_element_type=F32)
            for pp in range(ATT_GROUP // 2):
                p = g * ATT_GROUP // 2 + pp
                lo = o_all[2 * pp * BLOCK:(2 * pp + 1) * BLOCK, :]
                hi = o_all[(2 * pp + 1) * BLOCK:(2 * pp + 2) * BLOCK, :]
                att = jnp.where(low, lo[:, :LANES] / lo[:, LANES:], hi[:, :LANES] / hi[:, LANES:])
                gate = proj[r0:r0 + BLOCK, O_GC + p * LANES:O_GC + (p + 1) * LANES]
                y_scr[r0:r0 + BLOCK, p * LANES:(p + 1) * LANES] = (att * jax.nn.silu(gate)).astype(BF16)

    gt = proj[:, O_IF:O_IF + GATE_PAD].T[0:2 * H, :]
    pre = gt + bias_ref[...]
    row = lax.broadcasted_iota(jnp.int32, (2 * H, T), 0)
    top = row < H
    bcum = _lane_scan(jax.nn.log_sigmoid(pre), jnp.add, 0.0)
    bsw = pltpu.roll(bcum, H, 0)
    a_top = pre - bsw
    ag = jnp.where(top, a_top, pltpu.roll(a_top, H, 0))
    bf = jnp.where(top, bsw, bcum)
    m_in = jnp.concatenate([m_scr[...]] * (T // LANES), axis=1)
    mg = jnp.maximum(m_in, _lane_scan(ag, jnp.maximum, NEG))
    m_t = bf + mg
    nrm = jnp.exp(-m_t)
    inter_parts, wk_parts, decay_parts = [], [], []
    for c in range(T // L):
        m_start = m_in[:, 0:1] if c == 0 else mg[:, c * L - 1:c * L]
        m_last = mg[:, (c + 1) * L - 1:(c + 1) * L]
        inter_parts.append(jnp.exp(m_start - mg[:, c * L:(c + 1) * L]))
        wk_parts.append(jnp.exp(ag[:, c * L:(c + 1) * L] - m_last))
        decay_parts.append(jnp.broadcast_to(jnp.exp(m_start - m_last), (2 * H, 2 * MLSTM_V_DIM)))
    inter = jnp.concatenate(inter_parts, axis=1)
    m_scr[...] = jnp.broadcast_to(m_t[:, T - 1:T], m_scr.shape)
    ag2 = ag * LOG2E
    cols = jnp.concatenate([mg * LOG2E, inter, nrm, jnp.zeros((LANES - 6 * H, T), F32)], axis=0).T

    kT = proj[:, O_DK:O_DK + MLSTM_QK_WIDTH].T
    lane_p = lax.broadcasted_iota(jnp.int32, (L, LANES), 1)
    low_p = lane_p < MLSTM_QK_DIM
    tri = (lax.broadcasted_iota(jnp.int32, (BLOCK, BLOCK), 1) <= lax.broadcasted_iota(jnp.int32, (BLOCK, BLOCK), 0))
    ones = jnp.ones((L, MLSTM_V_DIM), BF16)
    zero_blk = jnp.zeros((BLOCK, BLOCK), BF16)

    def mlstm_chunk(c):
        r0 = c * L
        for pair in range(H // 2):
            qp = proj[r0:r0 + L, O_DQ + pair * LANES:O_DQ + (pair + 1) * LANES]
            kT_pair = kT[pair * LANES:(pair + 1) * LANES, r0:r0 + L].astype(BF16)
            c_pair = c_scr[pair * LANES:(pair + 1) * LANES, :].astype(BF16)
            for half in range(2):
                hd = 2 * pair + half
                q_h = jnp.where(low_p, qp, 0.0) if half == 0 else jnp.where(low_p, 0.0, qp)
                s = jnp.dot(q_h.astype(BF16), kT_pair, preferred_element_type=F32)
                if H * c + hd < MLSTM_CHUNKS:
                    nxt.step()
                p_rows = []
                for bi in range(L // BLOCK):
                    m_col = cols[r0 + bi * BLOCK:r0 + (bi + 1) * BLOCK, hd:hd + 1]
                    blks = []
                    for bj in range(L // BLOCK):
                        if bj > bi:
                            blks.append(zero_blk)
                            continue
                        arg = ag2[hd:hd + 1, r0 + bj * BLOCK:r0 + (bj + 1) * BLOCK] - m_col
                        if bj == bi:
                            arg = jnp.where(tri, arg, NEG)
                        sb = s[bi * BLOCK:(bi + 1) * BLOCK, bj * BLOCK:(bj + 1) * BLOCK]
                        blks.append((sb * jnp.exp2(arg)).astype(BF16))
                    p_rows.append(jnp.concatenate(blks, axis=1))
                p_bf = jnp.concatenate(p_rows, axis=0)
                v_aug = jnp.concatenate(
                    [proj[r0:r0 + L, O_DV + hd * MLSTM_V_DIM:O_DV + (hd + 1) * MLSTM_V_DIM].astype(BF16), ones],
                    axis=1)
                q_int = (q_h * cols[r0:r0 + L, 2 * H + hd:2 * H + hd + 1]).astype(BF16)
                num = jnp.dot(jnp.concatenate([p_bf, q_int], axis=1),
                              jnp.concatenate([v_aug, c_pair], axis=0), preferred_element_type=F32)
                den = jnp.maximum(jnp.abs(num[:, MLSTM_V_DIM:]), cols[r0:r0 + L, 4 * H + hd:4 * H + hd + 1])
                hm = num[:, :MLSTM_V_DIM] / den
                og = proj[r0:r0 + L, O_DO + hd * MLSTM_V_DIM:O_DO + (hd + 1) * MLSTM_V_DIM]
                gd = proj[r0:r0 + L, O_GD + hd * MLSTM_V_DIM:O_GD + (hd + 1) * MLSTM_V_DIM]
                y_scr[r0:r0 + L, ATT_WIDTH + hd * MLSTM_V_DIM:ATT_WIDTH + (hd + 1) * MLSTM_V_DIM] = (
                    hm * jax.nn.sigmoid(og) * jax.nn.silu(gd)).astype(BF16)
                kw = (kT[hd * MLSTM_QK_DIM:(hd + 1) * MLSTM_QK_DIM, r0:r0 + L]
                      * wk_parts[c][hd:hd + 1, :]).astype(BF16)
                upd = jnp.dot(kw, v_aug, preferred_element_type=F32)
                rows = slice(hd * MLSTM_QK_DIM, (hd + 1) * MLSTM_QK_DIM)
                c_scr[rows, :] = decay_parts[c][hd:hd + 1, :] * c_scr[rows, :] + upd

    if first is not False:
        attention()
        for _ in range(PRE_CHUNKS):
            nxt.step()
        for c in range(T // L):
            mlstm_chunk(c)
    else:
        for _ in range(PRE_CHUNKS):
            nxt.step()
        for c in range(T // L):
            mlstm_chunk(c)
        attention()
    nxt.finish()
    if after is not None:
        after.normalise()
    for r0 in range(0, T, OUT_ROWS):
        x2 = x_ref[r0:r0 + OUT_ROWS, :] + jnp.dot(y_scr[r0:r0 + OUT_ROWS, :], wout_ref[...],
                                                   preferred_element_type=F32)
        o_ref[r0:r0 + OUT_ROWS, :] = _rmsnorm(x2, fg_ref[...])


def _odd_kernel(sink_ref, ib_ref, fb_ref, xa_ref, xb_ref, xn_ref, wa_ref, wb_ref, wg_ref, wout_ref,
                fg_ref, o_ref, proj_a, proj_b, h_a, h_b, kvbuf, abias, bias_scr, c_scr, m_scr, y_a, y_b):
    s = pl.program_id(0)
    T = xa_ref.shape[0]
    w = (wa_ref, wb_ref, wg_ref)
    rest = (bias_scr, wout_ref, fg_ref, kvbuf, abias, c_scr, m_scr)

    @pl.when(s == 0)
    def _():
        _odd_init(sink_ref, ib_ref, fb_ref, kvbuf, abias, bias_scr, c_scr, m_scr)
        first = _InProj(xa_ref, *w, h_a, proj_a)
        first.normalise()
        first.finish()

    nxt = _InProj(xb_ref, *w, h_a, proj_b)
    nxt.normalise()
    after = _InProj(xn_ref, *w, h_b, proj_a)
    _odd_mixers(s == 0, 0, proj_a, xa_ref, o_ref.at[0:T], *rest, y_a, nxt, after)
    _odd_mixers(False, T, proj_b, xb_ref, o_ref.at[T:2 * T], *rest, y_b, after)
    for b in range(2 * ATT_KV_HEADS):
        kvbuf[b, 0:BLOCK, :] = kvbuf[b, 2 * T:2 * T + BLOCK, :]


def _odd_layer(x, wa, wb, wg, i_bias, f_bias, sinks, wo, final_g):
    S = x.shape[0]
    T = min(TILE_ODD, S // 2)
    NT = S // T
    assert S % (2 * T) == 0 and T % BLOCK == 0 and T % min(MCHUNK, T) == 0
    assert i_bias.shape == f_bias.shape == (MLSTM_HEADS,) and sinks.shape == (ATT_HEADS,)
    const = _const_spec
    smem = pl.BlockSpec(memory_space=pltpu.SMEM)
    return pl.pallas_call(
        _odd_kernel,
        grid=(NT // 2,),
        in_specs=[
            smem, smem, smem,
            pl.BlockSpec((T, D_MODEL), lambda s: (2 * s, 0)),
            pl.BlockSpec((T, D_MODEL), lambda s: (2 * s + 1, 0)),
            pl.BlockSpec((T, D_MODEL), lambda s: (jnp.minimum(2 * s + 2, NT - 1), 0)),
            const(O_DO, D_MODEL),
            const(O_IF - O_DO, D_MODEL),
            const(GATE_PAD, D_MODEL),
            const(ODD_OUT, D_MODEL),
            const(1, D_MODEL),
        ],
        out_specs=pl.BlockSpec((2 * T, D_MODEL), lambda s: (s, 0)),
        out_shape=jax.ShapeDtypeStruct((S, D_MODEL), F32),
        scratch_shapes=[
            pltpu.VMEM((T, ODD_IN_PAD), F32),
            pltpu.VMEM((T, ODD_IN_PAD), F32),
            pltpu.VMEM((T, D_MODEL), BF16),
            pltpu.VMEM((T, D_MODEL), BF16),
            pltpu.VMEM((2 * ATT_KV_HEADS, 2 * T + BLOCK, LANES), BF16),
            pltpu.VMEM((ATT_HEADS * BLOCK, 2 * BLOCK), F32),
            pltpu.VMEM((2 * MLSTM_HEADS, T), F32),
            pltpu.VMEM((MLSTM_HEADS * MLSTM_QK_DIM, 2 * MLSTM_V_DIM), F32),
            pltpu.VMEM((2 * MLSTM_HEADS, LANES), F32),
            pltpu.VMEM((T, ODD_OUT), BF16),
            pltpu.VMEM((T, ODD_OUT), BF16),
        ],
        compiler_params=pltpu.CompilerParams(
            dimension_semantics=("arbitrary",), vmem_limit_bytes=VMEM_LIMIT),
        name="odd_layer",
    )(sinks.astype(F32), i_bias.astype(F32), f_bias.astype(F32), x, x, x, wa, wb, wg, wo,
      final_g.reshape(1, D_MODEL))


def kernel(x, even_norm, even_w_in, even_conv_w, even_pool_w, even_pool_scale, even_w_out,
           odd_norm, odd_w_in, odd_i_bias, odd_f_bias, odd_sinks, odd_w_out, final_norm):
    B, S, D = x.shape
    assert B == 1 and D == D_MODEL
    assert even_norm.shape[0] == 1 and odd_norm.shape[0] == 1
    h = x.reshape(S, D)
    h, wa, wb, wg, wo = _even_layer(h, even_norm[0], even_w_in[0], jnp.swapaxes(even_conv_w, 0, 1), even_pool_w[0],
                                    even_pool_scale[0], even_w_out[0], odd_norm[0], odd_w_in[0].T, odd_w_out[0])
    h = _odd_layer(h, wa, wb, wg, odd_i_bias[0], odd_f_bias[0], odd_sinks[0], wo, final_norm)
    return h.reshape(B, S, D)
```

```python
import jax
import jax.numpy as jnp
from jax import lax
from jax.experimental import pallas as pl
from jax.experimental.pallas import tpu as pltpu

F32 = jnp.float32
BF16 = jnp.bfloat16

D_MODEL = 1024
RMS_EPS = 1e-6

CONV_WIDTH = 512
CONV_K = 3
POOL_WINDOWS = (2, 4, 8, 16)
POOL_WIDTH = 512
POOL_GROUP = POOL_WIDTH // len(POOL_WINDOWS)
EVEN_IN = 4 * CONV_WIDTH + 2 * POOL_WIDTH
EVEN_OUT = CONV_WIDTH + POOL_WIDTH
CONV_HALO = 8
POOL_HALO = 16

ATT_HEADS = 8
ATT_KV_HEADS = 2
ATT_GROUP = ATT_HEADS // ATT_KV_HEADS
ATT_HEAD_DIM = 64
ATT_WIDTH = ATT_HEADS * ATT_HEAD_DIM
ATT_KV_WIDTH = ATT_KV_HEADS * ATT_HEAD_DIM
WINDOW = 128
BLOCK = 128
MLSTM_HEADS = 4
MLSTM_QK_DIM = 64
MLSTM_V_DIM = 128
MLSTM_QK_WIDTH = MLSTM_HEADS * MLSTM_QK_DIM
MLSTM_WIDTH = MLSTM_HEADS * MLSTM_V_DIM
ODD_OUT = ATT_WIDTH + MLSTM_WIDTH
LANES = 128
BF16_ROWS = 16
SINK_KEY = 0
MXU_WIDTH = 256
GATE_PAD = LANES

O_CQ = 0
O_CK = O_CQ + ATT_WIDTH
O_CV = O_CK + ATT_KV_WIDTH
O_GC = O_CV + ATT_KV_WIDTH
O_DQ = O_GC + ATT_WIDTH
O_DK = O_DQ + MLSTM_QK_WIDTH
O_DV = O_DK + MLSTM_QK_WIDTH
O_DO = O_DV + MLSTM_WIDTH
O_GD = O_DO + MLSTM_WIDTH
O_IF = O_GD + MLSTM_WIDTH
ODD_IN_PAD = O_IF + GATE_PAD

TILE_EVEN = 1024
EVEN_CHUNK = 512
TILE_ODD = 512
MCHUNK = 256
INPROJ_ROWS = 512
OUT_ROWS = 256
PRE_CHUNKS = 3
MLSTM_CHUNKS = 5
ATTN_SKIP_EVERY = 4
LOG2E = 1.4426950408889634
ATT_Q_SCALE = ATT_HEAD_DIM ** -0.5 * LOG2E
MLSTM_Q_SCALE = MLSTM_QK_DIM ** -0.5
VMEM_LIMIT = 60 * 1024 * 1024


def _const_spec(*shape):
    return pl.BlockSpec(shape, lambda i: (0,) * len(shape), pipeline_mode=pl.Buffered(1))


def _rms_scale(x):
    ms = jnp.mean(x * x, axis=-1, keepdims=True)
    return x * lax.rsqrt(ms + RMS_EPS)


def _rmsnorm(x, g):
    return _rms_scale(x) * g


def _even_kernel(x_ref, g_ref, win_ref, cw_ref, pw_ref, ps_ref, wout_ref,
                 ng_ref, nwa32_ref, nwb32_ref, nwg32_ref, nwo32_ref,
                 o_ref, nwa_ref, nwb_ref, nwg_ref, nwo_ref, proj_scr, uc_scr, up_scr, y_scr, wfold_scr):
    i = pl.program_id(0)
    T = x_ref.shape[0]
    C = CONV_WIDTH

    ng = ng_ref[...]
    ra = nwa32_ref.shape[0]
    feat = lax.broadcasted_iota(jnp.int32, (ra, D_MODEL), 0) + i * ra
    q_scale = jnp.where(feat < O_CK, ATT_Q_SCALE,
                        jnp.where(feat < O_DQ, 1.0, jnp.where(feat < O_DK, MLSTM_Q_SCALE, 1.0)))
    nwa_ref[...] = (nwa32_ref[...] * ng * q_scale).astype(BF16)
    nwb_ref[...] = (nwb32_ref[...] * ng).astype(BF16)
    nwo_ref[...] = nwo32_ref[...].astype(BF16)
    nwg_ref[...] = jnp.concatenate(
        [nwg32_ref[...] * ng, jnp.zeros((GATE_PAD - 2 * MLSTM_HEADS, D_MODEL), F32)], axis=0).astype(BF16)

    @pl.when(i == 0)
    def _():
        uc_scr[0:CONV_HALO, :] = jnp.zeros((CONV_HALO, C), F32)
        up_scr[0:POOL_HALO, :] = jnp.zeros((POOL_HALO, POOL_WIDTH), F32)
        for g in range(len(POOL_WINDOWS)):
            lo, hi = g * POOL_GROUP, (g + 1) * POOL_GROUP
            wfold_scr[:, lo:hi] = jnp.dot(win_ref[:, 4 * C + lo:4 * C + hi], pw_ref[g],
                                          precision=lax.Precision.HIGHEST, preferred_element_type=F32)

    h = _rmsnorm(x_ref[...], g_ref[...]).astype(BF16)
    chunks = list(range(0, EVEN_IN, EVEN_CHUNK))
    for c0 in chunks[4:] + chunks[:4]:
        w = wfold_scr[...] if c0 == 4 * C else win_ref[:, c0:c0 + EVEN_CHUNK]
        proj_scr[:, c0:c0 + EVEN_CHUNK] = jnp.dot(h, w.astype(BF16), preferred_element_type=F32)

    uc = proj_scr[:, 2 * C:3 * C] * proj_scr[:, 0:C]
    uc_scr[CONV_HALO:CONV_HALO + T, :] = uc
    conv = (cw_ref[2] * uc
            + cw_ref[1] * uc_scr[CONV_HALO - 1:CONV_HALO - 1 + T, :]
            + cw_ref[0] * uc_scr[CONV_HALO - 2:CONV_HALO - 2 + T, :])
    ya = proj_scr[:, C:2 * C] * conv * jax.nn.silu(proj_scr[:, 3 * C:4 * C])
    y_scr[:, 0:C] = ya.astype(BF16)
    uc_scr[0:CONV_HALO, :] = uc_scr[T:T + CONV_HALO, :]

    up_scr[POOL_HALO:POOL_HALO + T, :] = proj_scr[:, 4 * C:4 * C + POOL_WIDTH]
    t1 = (lax.broadcasted_iota(jnp.int32, (T, POOL_GROUP), 0) + (i * T + 1)).astype(F32)
    for g, w in enumerate(POOL_WINDOWS):
        lo, hi = g * POOL_GROUP, (g + 1) * POOL_GROUP
        s = up_scr[:, lo:hi]
        k = 1
        while k < w:
            s = s + pltpu.roll(s, k, 0)
            k *= 2
        u = up_scr[POOL_HALO:POOL_HALO + T, lo:hi]
        yb = s[POOL_HALO:, :] / jnp.minimum(t1, float(w)) - u
        gate = proj_scr[:, 4 * C + POOL_WIDTH + lo:4 * C + POOL_WIDTH + hi]
        y_scr[:, C + lo:C + hi] = (yb * ps_ref[:, lo:hi] * jax.nn.silu(gate)).astype(BF16)
    up_scr[0:POOL_HALO, :] = up_scr[T:T + POOL_HALO, :]

    o_ref[...] = x_ref[...] + jnp.dot(y_scr[...], wout_ref[...].astype(BF16), preferred_element_type=F32)


def _even_layer(x, norm_g, w_in, conv_w, pool_w, pool_scale, w_out, next_norm_g, next_w_in_t, next_w_out):
    S = x.shape[0]
    T = min(TILE_EVEN, S)
    steps = S // T
    n_gates = 2 * MLSTM_HEADS
    assert S % T == 0
    assert next_w_in_t.shape == (O_IF + n_gates, D_MODEL) and next_w_out.shape == (ODD_OUT, D_MODEL)
    ra, rb, ro = O_DO // steps, (O_IF - O_DO) // steps, ODD_OUT // steps
    assert ra * steps == O_DO and rb * steps == O_IF - O_DO and ro * steps == ODD_OUT
    const = _const_spec
    slab = lambda r: pl.BlockSpec((r, D_MODEL), lambda i: (i, 0))
    return pl.pallas_call(
        _even_kernel,
        grid=(steps,),
        in_specs=[
            pl.BlockSpec((T, D_MODEL), lambda i: (i, 0)),
            const(1, D_MODEL),
            const(D_MODEL, EVEN_IN),
            const(CONV_K, 1, CONV_WIDTH),
            const(len(POOL_WINDOWS), POOL_GROUP, POOL_GROUP),
            const(1, POOL_WIDTH),
            const(EVEN_OUT, D_MODEL),
            const(1, D_MODEL),
            slab(ra),
            pl.BlockSpec((pl.Element(rb), pl.Element(D_MODEL)), lambda i: (pl.multiple_of(O_DO + n_gates + i * rb, n_gates), 0)),
            pl.BlockSpec((n_gates, D_MODEL), lambda i: (O_DO // n_gates, 0)),
            slab(ro),
        ],
        out_specs=[pl.BlockSpec((T, D_MODEL), lambda i: (i, 0)),
                   slab(ra), slab(rb), pl.BlockSpec((GATE_PAD, D_MODEL), lambda i: (0, 0)), slab(ro)],
        out_shape=[jax.ShapeDtypeStruct((S, D_MODEL), F32),
                   jax.ShapeDtypeStruct((O_DO, D_MODEL), BF16),
                   jax.ShapeDtypeStruct((O_IF - O_DO, D_MODEL), BF16),
                   jax.ShapeDtypeStruct((GATE_PAD, D_MODEL), BF16),
                   jax.ShapeDtypeStruct((ODD_OUT, D_MODEL), BF16)],
        scratch_shapes=[
            pltpu.VMEM((T, EVEN_IN), F32),
            pltpu.VMEM((T + CONV_HALO, CONV_WIDTH), F32),
            pltpu.VMEM((T + POOL_HALO, POOL_WIDTH), F32),
            pltpu.VMEM((T, EVEN_OUT), BF16),
            pltpu.VMEM((D_MODEL, POOL_WIDTH), F32),
        ],
        compiler_params=pltpu.CompilerParams(
            dimension_semantics=("arbitrary",), vmem_limit_bytes=VMEM_LIMIT),
        name="even_layer",
    )(x, norm_g.reshape(1, D_MODEL), w_in, conv_w, pool_w, pool_scale.reshape(1, POOL_WIDTH), w_out,
      next_norm_g.reshape(1, D_MODEL), next_w_in_t, next_w_in_t, next_w_in_t, next_w_out)


def _lane_scan(x, op, identity):
    n = x.shape[1]
    lane = lax.broadcasted_iota(jnp.int32, x.shape, 1)
    k = 1
    while k < n:
        x = op(x, jnp.where(lane >= k, pltpu.roll(x, k, 1), identity))
        k *= 2
    return x


def _odd_init(sink_ref, ib_ref, fb_ref, kvbuf, abias, bias_scr, c_scr, m_scr):
    for b in range(2 * ATT_KV_HEADS):
        kvbuf[b, 0:BLOCK, :] = jnp.zeros((BLOCK, LANES), BF16)
    c_scr[...] = jnp.zeros(c_scr.shape, F32)
    m_scr[...] = jnp.zeros(m_scr.shape, F32)
    qi = lax.broadcasted_iota(jnp.int32, (BLOCK, 2 * BLOCK), 0)
    kj = lax.broadcasted_iota(jnp.int32, (BLOCK, 2 * BLOCK), 1)
    dist = qi + BLOCK - kj
    ok = (dist >= 0) & (dist < WINDOW)
    distf = dist.astype(F32)
    for hd in range(ATT_HEADS):
        slope = 2.0 ** (-8.0 * (hd + 1) / ATT_HEADS)
        band = jnp.where(ok, (-slope * LOG2E) * distf, -jnp.inf)
        abias[hd * BLOCK:(hd + 1) * BLOCK, :] = jnp.where(kj == SINK_KEY, sink_ref[hd] * LOG2E, band)
    row = lax.broadcasted_iota(jnp.int32, bias_scr.shape, 0)
    bias = jnp.zeros(bias_scr.shape, F32)
    for hd in range(MLSTM_HEADS):
        bias = jnp.where(row == hd, ib_ref[hd], jnp.where(row == MLSTM_HEADS + hd, fb_ref[hd], bias))
    bias_scr[...] = bias


class _InProj:
    def __init__(self, x_ref, wa_ref, wb_ref, wg_ref, h_scr, proj):
        self.x_ref, self.h_scr, self.proj = x_ref, h_scr, proj
        self.todo = []
        rows = x_ref.shape[0]
        for w_ref, base in ((wa_ref, 0), (wb_ref, O_DO), (wg_ref, O_IF)):
            n = w_ref.shape[0]
            for c0 in range(0, n, MXU_WIDTH):
                for r0 in range(0, rows, INPROJ_ROWS):
                    self.todo.append((w_ref, base, c0, min(c0 + MXU_WIDTH, n), r0, min(r0 + INPROJ_ROWS, rows)))

    def normalise(self):
        self.h_scr[...] = _rms_scale(self.x_ref[...]).astype(BF16)

    def step(self):
        if self.todo:
            w_ref, base, c0, c1, r0, r1 = self.todo.pop(0)
            self.proj[r0:r1, base + c0:base + c1] = lax.dot_general(
                self.h_scr[r0:r1, :], w_ref[c0:c1, :], (((1,), (1,)), ((), ())), preferred_element_type=F32)

    def finish(self):
        while self.todo:
            self.step()


def _odd_mixers(first, kv0, proj, x_ref, o_ref, bias_ref, wout_ref, fg_ref,
                kvbuf, abias, c_scr, m_scr, y_scr, nxt, after=None):
    T = x_ref.shape[0]
    NB = T // BLOCK
    H = MLSTM_HEADS
    L = min(MCHUNK, T)
    NEG = -jnp.inf

    lane_q = lax.broadcasted_iota(jnp.int32, (BLOCK, LANES), 1)
    low = lane_q < ATT_HEAD_DIM
    low_t = lax.broadcasted_iota(jnp.int32, (T, LANES), 1) < ATT_HEAD_DIM
    for b, off in enumerate((O_CK, O_CV)):
        kv = proj[:, off:off + ATT_KV_WIDTH]
        sw = pltpu.roll(kv, ATT_HEAD_DIM, 1)
        kvbuf[2 * b, kv0 + BLOCK:kv0 + BLOCK + T, :] = jnp.where(low_t, kv, sw).astype(BF16)
        kvbuf[2 * b + 1, kv0 + BLOCK:kv0 + BLOCK + T, :] = jnp.where(low_t, sw, kv).astype(BF16)
    if first is not False:
        key_lane = lax.broadcasted_iota(jnp.int32, (BLOCK, 2 * BLOCK), 1)
        no_prev = jnp.where((key_lane < BLOCK) & (key_lane != SINK_KEY), jnp.where(first, NEG, 0.0), 0.0)
    sink_row = lax.broadcasted_iota(jnp.int32, (BF16_ROWS, LANES), 0) == SINK_KEY

    def drop_sink_row(a):
        head = a[0:BF16_ROWS, :]
        return jnp.concatenate([jnp.where(sink_row, jnp.zeros_like(head), head), a[BF16_ROWS:, :]], axis=0)

    ones_v = jnp.ones((2 * BLOCK, LANES), BF16)

    def scores(j, g):
        r0 = j * BLOCK
        kk = drop_sink_row(kvbuf[g, kv0 + r0:kv0 + r0 + 2 * BLOCK, :])
        qs = []
        for p in range(g * ATT_GROUP // 2, (g + 1) * ATT_GROUP // 2):
            qp = proj[r0:r0 + BLOCK, O_CQ + p * LANES:O_CQ + (p + 1) * LANES]
            qs.append(jnp.where(low, qp, 0.0).astype(BF16))
            qs.append(jnp.where(low, 0.0, qp).astype(BF16))
        return lax.dot_general(jnp.concatenate(qs, axis=0), kk, (((1,), (1,)), ((), ())),
                               preferred_element_type=F32)

    def attention():
        slots = [(j, g) for j in range(NB) for g in range(ATT_KV_HEADS)]
        s_next = scores(*slots[0])
        for n, (j, g) in enumerate(slots):
            r0 = j * BLOCK
            s_all = s_next
            if n + 1 < len(slots):
                s_next = scores(*slots[n + 1])
            if n % ATTN_SKIP_EVERY != ATTN_SKIP_EVERY - 1:
                nxt.step()
            ps = []
            for r in range(ATT_GROUP):
                hd = g * ATT_GROUP + r
                s = s_all[r * BLOCK:(r + 1) * BLOCK, :] + abias[hd * BLOCK:(hd + 1) * BLOCK, :]
                if j == 0 and first is not False:
                    s = s + no_prev
                ps.append(jnp.exp2(s - jnp.max(s, axis=1, keepdims=True)).astype(BF16))
            vv = jnp.concatenate(
                [drop_sink_row(kvbuf[ATT_KV_HEADS + g, kv0 + r0:kv0 + r0 + 2 * BLOCK, :]), ones_v], axis=1)
            o_all = jnp.dot(jnp.concatenate(ps, axis=0), vv, preferred_element_type=F32)
            for pp in range(ATT_GROUP // 2):
                p = g * ATT_GROUP // 2 + pp
                lo = o_all[2 * pp * BLOCK:(2 * pp + 1) * BLOCK, :]
                hi = o_all[(2 * pp + 1) * BLOCK:(2 * pp + 2) * BLOCK, :]
                att = jnp.where(low, lo[:, :LANES] / lo[:, LANES:], hi[:, :LANES] / hi[:, LANES:])
                gate = proj[r0:r0 + BLOCK, O_GC + p * LANES:O_GC + (p + 1) * LANES]
                y_scr[r0:r0 + BLOCK, p * LANES:(p + 1) * LANES] = (att * jax.nn.silu(gate)).astype(BF16)

    gt = proj[:, O_IF:O_IF + GATE_PAD].T[0:2 * H, :]
    pre = gt + bias_ref[...]
    row = lax.broadcasted_iota(jnp.int32, (2 * H, T), 0)
    top = row < H
    bcum = _lane_scan(jax.nn.log_sigmoid(pre), jnp.add, 0.0)
    bsw = pltpu.roll(bcum, H, 0)
    a_top = pre - bsw
    ag = jnp.where(top, a_top, pltpu.roll(a_top, H, 0))
    bf = jnp.where(top, bsw, bcum)
    m_in = jnp.concatenate([m_scr[...]] * (T // LANES), axis=1)
    mg = jnp.maximum(m_in, _lane_scan(ag, jnp.maximum, NEG))
    m_t = bf + mg
    nrm = jnp.exp(-m_t)
    inter_parts, wk_parts, decay_parts = [], [], []
    for c in range(T // L):
        m_start = m_in[:, 0:1] if c == 0 else mg[:, c * L - 1:c * L]
        m_last = mg[:, (c + 1) * L - 1:(c + 1) * L]
        inter_parts.append(jnp.exp(m_start - mg[:, c * L:(c + 1) * L]))
        wk_parts.append(jnp.exp(ag[:, c * L:(c + 1) * L] - m_last))
        decay_parts.append(jnp.broadcast_to(jnp.exp(m_start - m_last), (2 * H, 2 * MLSTM_V_DIM)))
    inter = jnp.concatenate(inter_parts, axis=1)
    m_scr[...] = jnp.broadcast_to(m_t[:, T - 1:T], m_scr.shape)
    ag2 = ag * LOG2E
    cols = jnp.concatenate([mg * LOG2E, inter, nrm, jnp.zeros((LANES - 6 * H, T), F32)], axis=0).T

    kT = proj[:, O_DK:O_DK + MLSTM_QK_WIDTH].T
    lane_p = lax.broadcasted_iota(jnp.int32, (L, LANES), 1)
    low_p = lane_p < MLSTM_QK_DIM
    tri = (lax.broadcasted_iota(jnp.int32, (BLOCK, BLOCK), 1) <= lax.broadcasted_iota(jnp.int32, (BLOCK, BLOCK), 0))
    ones = jnp.ones((L, MLSTM_V_DIM), BF16)
    zero_blk = jnp.zeros((BLOCK, BLOCK), BF16)

    def mlstm_chunk(c):
        r0 = c * L
        for pair in range(H // 2):
            qp = proj[r0:r0 + L, O_DQ + pair * LANES:O_DQ + (pair + 1) * LANES]
            kT_pair = kT[pair * LANES:(pair + 1) * LANES, r0:r0 + L].astype(BF16)
            c_pair = c_scr[pair * LANES:(pair + 1) * LANES, :].astype(BF16)
            for half in range(2):
                hd = 2 * pair + half
                q_h = jnp.where(low_p, qp, 0.0) if half == 0 else jnp.where(low_p, 0.0, qp)
                s = jnp.dot(q_h.astype(BF16), kT_pair, preferred_element_type=F32)
                if H * c + hd < MLSTM_CHUNKS:
                    nxt.step()
                p_rows = []
                for bi in range(L // BLOCK):
                    m_col = cols[r0 + bi * BLOCK:r0 + (bi + 1) * BLOCK, hd:hd + 1]
                    blks = []
                    for bj in range(L // BLOCK):
                        if bj > bi:
                            blks.append(zero_blk)
                            continue
                        arg = ag2[hd:hd + 1, r0 + bj * BLOCK:r0 + (bj + 1) * BLOCK] - m_col
                        if bj == bi:
                            arg = jnp.where(tri, arg, NEG)
                        sb = s[bi * BLOCK:(bi + 1) * BLOCK, bj * BLOCK:(bj + 1) * BLOCK]
                        blks.append((sb * jnp.exp2(arg)).astype(BF16))
                    p_rows.append(jnp.concatenate(blks, axis=1))
                p_bf = jnp.concatenate(p_rows, axis=0)
                v_aug = jnp.concatenate(
                    [proj[r0:r0 + L, O_DV + hd * MLSTM_V_DIM:O_DV + (hd + 1) * MLSTM_V_DIM].astype(BF16), ones],
                    axis=1)
                q_int = (q_h * cols[r0:r0 + L, 2 * H + hd:2 * H + hd + 1]).astype(BF16)
                num = jnp.dot(jnp.concatenate([p_bf, q_int], axis=1),
                              jnp.concatenate([v_aug, c_pair], axis=0), preferred_element_type=F32)
                den = jnp.maximum(jnp.abs(num[:, MLSTM_V_DIM:]), cols[r0:r0 + L, 4 * H + hd:4 * H + hd + 1])
                hm = num[:, :MLSTM_V_DIM] / den
                og = proj[r0:r0 + L, O_DO + hd * MLSTM_V_DIM:O_DO + (hd + 1) * MLSTM_V_DIM]
                gd = proj[r0:r0 + L, O_GD + hd * MLSTM_V_DIM:O_GD + (hd + 1) * MLSTM_V_DIM]
                y_scr[r0:r0 + L, ATT_WIDTH + hd * MLSTM_V_DIM:ATT_WIDTH + (hd + 1) * MLSTM_V_DIM] = (
                    hm * jax.nn.sigmoid(og) * jax.nn.silu(gd)).astype(BF16)
                kw = (kT[hd * MLSTM_QK_DIM:(hd + 1) * MLSTM_QK_DIM, r0:r0 + L]
                      * wk_parts[c][hd:hd + 1, :]).astype(BF16)
                upd = jnp.dot(kw, v_aug, preferred_element_type=F32)
                rows = slice(hd * MLSTM_QK_DIM, (hd + 1) * MLSTM_QK_DIM)
                c_scr[rows, :] = decay_parts[c][hd:hd + 1, :] * c_scr[rows, :] + upd

    attention()
    for _ in range(PRE_CHUNKS):
        nxt.step()
    for c in range(T // L):
        mlstm_chunk(c)
    nxt.finish()
    if after is not None:
        after.normalise()
    for r0 in range(0, T, OUT_ROWS):
        x2 = x_ref[r0:r0 + OUT_ROWS, :] + jnp.dot(y_scr[r0:r0 + OUT_ROWS, :], wout_ref[...],
                                                   preferred_element_type=F32)
        o_ref[r0:r0 + OUT_ROWS, :] = _rmsnorm(x2, fg_ref[...])


def _odd_kernel(sink_ref, ib_ref, fb_ref, xa_ref, xb_ref, xn_ref, wa_ref, wb_ref, wg_ref, wout_ref,
                fg_ref, o_ref, proj_a, proj_b, h_a, h_b, kvbuf, abias, bias_scr, c_scr, m_scr, y_a, y_b):
    s = pl.program_id(0)
    T = xa_ref.shape[0]
    w = (wa_ref, wb_ref, wg_ref)
    rest = (bias_scr, wout_ref, fg_ref, kvbuf, abias, c_scr, m_scr)

    @pl.when(s == 0)
    def _():
        _odd_init(sink_ref, ib_ref, fb_ref, kvbuf, abias, bias_scr, c_scr, m_scr)
        first = _InProj(xa_ref, *w, h_a, proj_a)
        first.normalise()
        first.finish()

    nxt = _InProj(xb_ref, *w, h_a, proj_b)
    nxt.normalise()
    after = _InProj(xn_ref, *w, h_b, proj_a)
    _odd_mixers(s == 0, 0, proj_a, xa_ref, o_ref.at[0:T], *rest, y_a, nxt, after)
    _odd_mixers(False, T, proj_b, xb_ref, o_ref.at[T:2 * T], *rest, y_b, after)
    for b in range(2 * ATT_KV_HEADS):
        kvbuf[b, 0:BLOCK, :] = kvbuf[b, 2 * T:2 * T + BLOCK, :]


def _odd_layer(x, wa, wb, wg, i_bias, f_bias, sinks, wo, final_g):
    S = x.shape[0]
    T = min(TILE_ODD, S // 2)
    NT = S // T
    assert S % (2 * T) == 0 and T % BLOCK == 0 and T % min(MCHUNK, T) == 0
    assert i_bias.shape == f_bias.shape == (MLSTM_HEADS,) and sinks.shape == (ATT_HEADS,)
    const = _const_spec
    smem = pl.BlockSpec(memory_space=pltpu.SMEM)
    return pl.pallas_call(
        _odd_kernel,
        grid=(NT // 2,),
        in_specs=[
            smem, smem, smem,
            pl.BlockSpec((T, D_MODEL), lambda s: (2 * s, 0)),
            pl.BlockSpec((T, D_MODEL), lambda s: (2 * s + 1, 0)),
            pl.BlockSpec((T, D_MODEL), lambda s: (jnp.minimum(2 * s + 2, NT - 1), 0)),
            const(O_DO, D_MODEL),
            const(O_IF - O_DO, D_MODEL),
            const(GATE_PAD, D_MODEL),
            const(ODD_OUT, D_MODEL),
            const(1, D_MODEL),
        ],
        out_specs=pl.BlockSpec((2 * T, D_MODEL), lambda s: (s, 0)),
        out_shape=jax.ShapeDtypeStruct((S, D_MODEL), F32),
        scratch_shapes=[
            pltpu.VMEM((T, ODD_IN_PAD), F32),
            pltpu.VMEM((T, ODD_IN_PAD), F32),
            pltpu.VMEM((T, D_MODEL), BF16),
            pltpu.VMEM((T, D_MODEL), BF16),
            pltpu.VMEM((2 * ATT_KV_HEADS, 2 * T + BLOCK, LANES), BF16),
            pltpu.VMEM((ATT_HEADS * BLOCK, 2 * BLOCK), F32),
            pltpu.VMEM((2 * MLSTM_HEADS, T), F32),
            pltpu.VMEM((MLSTM_HEADS * MLSTM_QK_DIM, 2 * MLSTM_V_DIM), F32),
            pltpu.VMEM((2 * MLSTM_HEADS, LANES), F32),
            pltpu.VMEM((T, ODD_OUT), BF16),
            pltpu.VMEM((T, ODD_OUT), BF16),
        ],
        compiler_params=pltpu.CompilerParams(
            dimension_semantics=("arbitrary",), vmem_limit_bytes=VMEM_LIMIT),
        name="odd_layer",
    )(sinks.astype(F32), i_bias.astype(F32), f_bias.astype(F32), x, x, x, wa, wb, wg, wo,
      final_g.reshape(1, D_MODEL))


def kernel(x, even_norm, even_w_in, even_conv_w, even_pool_w, even_pool_scale, even_w_out,
           odd_norm, odd_w_in, odd_i_bias, odd_f_bias, odd_sinks, odd_w_out, final_norm):
    B, S, D = x.shape
    assert B == 1 and D == D_MODEL
    assert even_norm.shape[0] == 1 and odd_norm.shape[0] == 1
    h = x.reshape(S, D)
    h, wa, wb, wg, wo = _even_layer(h, even_norm[0], even_w_in[0], jnp.swapaxes(even_conv_w, 0, 1), even_pool_w[0],
                                    even_pool_scale[0], even_w_out[0], odd_norm[0], odd_w_in[0].T, odd_w_out[0])
    h = _odd_layer(h, wa, wb, wg, odd_i_bias[0], odd_f_bias[0], odd_sinks[0], wo, final_norm)
    return h.reshape(B, S, D)
```
